```python
import math
import jax, jax.numpy as jnp
from jax import lax
import numpy as np

D_MODEL = 1024
BATCH = 8
SEQ = 4096
DEPTH = 2

HEAD_DIM = 64
MOBA_HEADS = D_MODEL // 128
MOBA_WIDTH = MOBA_HEADS * HEAD_DIM
DIFF_HEADS = D_MODEL // 256
DIFF_QK_WIDTH = DIFF_HEADS * 2 * HEAD_DIM
DIFF_VDIM = 2 * HEAD_DIM
DIFF_WIDTH = DIFF_HEADS * DIFF_VDIM
MIX_WIDTH = MOBA_WIDTH + DIFF_WIDTH
IN_WIDTH = 4 * MOBA_WIDTH + 2 * DIFF_QK_WIDTH + 2 * DIFF_WIDTH
MOBA_BLOCK = 256
MOBA_TOPK = 3
MOBA_QCHUNK = 32
DIFF_QBLOCK = 128
NORM_EPS = 1e-6
NEG_INF = -1e30

kernel_name = "hymba_moba_diffattn_alibi_block"


def rms_norm(x, g):
    xf = x.astype(jnp.float32)
    y = xf * lax.rsqrt(jnp.mean(xf * xf, axis=-1, keepdims=True) + NORM_EPS)
    return (y * g.astype(jnp.float32)).astype(x.dtype)


def alibi_slopes(n_heads):
    return jnp.asarray(2.0 ** (-8.0 * np.arange(1, n_heads + 1) / n_heads), dtype=jnp.float32)


def moba_attention(q, k, v, slopes):
    B, H, S, dh = q.shape
    nb = S // MOBA_BLOCK
    kb = k.reshape(B, H, nb, MOBA_BLOCK, dh)
    vb = v.reshape(B, H, nb, MOBA_BLOCK, dh)
    scale = dh ** -0.5
    topk = min(MOBA_TOPK, nb - 1)
    q_blk = jnp.arange(S) // MOBA_BLOCK
    if topk > 0:
        k_mean = jnp.mean(kb.astype(jnp.float32), axis=3)
        gate = jnp.einsum('bhsd,bhnd->bhsn', q.astype(jnp.float32), k_mean)
        past = jnp.arange(nb)[None, :] < q_blk[:, None]
        gate = jnp.where(past[None, None], gate, NEG_INF)
        _, sel_idx = lax.top_k(gate, topk)
    bi = jnp.arange(B)[:, None, None, None]
    hi = jnp.arange(H)[None, :, None, None]
    slope4 = slopes[None, :, None, None]

    def chunk(c):
        t0 = c * MOBA_QCHUNK
        qc = lax.dynamic_slice_in_dim(q, t0, MOBA_QCHUNK, axis=2)
        tq = t0 + jnp.arange(MOBA_QCHUNK)
        own = t0 // MOBA_BLOCK
        k_own = lax.dynamic_index_in_dim(kb, own, axis=2, keepdims=False)
        v_own = lax.dynamic_index_in_dim(vb, own, axis=2, keepdims=False)
        s_own = own * MOBA_BLOCK + jnp.arange(MOBA_BLOCK)
        dist_own = (tq[:, None] - s_own[None, :]).astype(jnp.float32)
        sc_own = jnp.einsum('bhqd,bhkd->bhqk', qc, k_own).astype(jnp.float32) * scale - slope4 * dist_own
        sc_own = jnp.where((s_own[None, :] <= tq[:, None])[None, None], sc_own, NEG_INF)
        if topk == 0:
            p = jax.nn.softmax(sc_own, axis=-1)
            return jnp.einsum('bhqk,bhkd->bhqd', p.astype(v.dtype), v_own)
        ic = lax.dynamic_slice_in_dim(sel_idx, t0, MOBA_QCHUNK, axis=2)
        k_sel = kb[bi, hi, ic]
        v_sel = vb[bi, hi, ic]
        s_sel = ic[..., None] * MOBA_BLOCK + jnp.arange(MOBA_BLOCK)
        dist_sel = (tq[None, None, :, None, None] - s_sel).astype(jnp.float32)
        sc_sel = jnp.einsum('bhqd,bhqjkd->bhqjk', qc, k_sel).astype(jnp.float32) * scale
        sc_sel = sc_sel - slopes[None, :, None, None, None] * dist_sel
        sc_sel = jnp.where((ic < own)[..., None], sc_sel, NEG_INF)
        n_sel = topk * MOBA_BLOCK
        scores = jnp.concatenate([sc_sel.reshape(B, H, MOBA_QCHUNK, n_sel), sc_own], axis=-1)
        p = jax.nn.softmax(scores, axis=-1).astype(v.dtype)
        out = jnp.einsum('bhqk,bhqkd->bhqd', p[..., :n_sel],
                         v_sel.reshape(B, H, MOBA_QCHUNK, n_sel, dh))
        return out + jnp.einsum('bhqk,bhkd->bhqd', p[..., n_sel:], v_own)

    outs = lax.map(chunk, jnp.arange(S // MOBA_QCHUNK))
    return outs.transpose(1, 2, 0, 3, 4).reshape(B, H, S, dh)


def diff_attention(q, k, v, lam, slopes):
    B, H, S, _, dh = q.shape
    scale = dh ** -0.5
    sk = jnp.arange(S)
    slope5 = slopes[None, :, None, None, None]

    def block(c):
        t0 = c * DIFF_QBLOCK
        qc = lax.dynamic_slice_in_dim(q, t0, DIFF_QBLOCK, axis=2)
        tq = t0 + jnp.arange(DIFF_QBLOCK)
        sc = jnp.einsum('bhqcd,bhkcd->bhcqk', qc, k).astype(jnp.float32) * scale
        dist = (tq[:, None] - sk[None, :]).astype(jnp.float32)
        sc = jnp.where((sk[None, :] <= tq[:, None])[None, None, None], sc - slope5 * dist, NEG_INF)
        p = jax.nn.softmax(sc, axis=-1)
        a = p[:, :, 0] - lam * p[:, :, 1]
        return jnp.einsum('bhqk,bhkd->bhqd', a.astype(v.dtype), v)

    outs = lax.map(block, jnp.arange(S // DIFF_QBLOCK))
    return outs.transpose(1, 2, 0, 3, 4).reshape(B, H, S, v.shape[-1])


def setup_inputs(seed: int = 0) -> dict:
    key = jax.random.key(seed)
    ks = jax.random.split(key, 13)
    f32 = jnp.float32
    nrm = lambda k, shape: jax.random.normal(k, shape, dtype=f32)
    return {
        "x": nrm(ks[0], (BATCH, SEQ, D_MODEL)),
        "norm_g": 1.0 + 0.02 * nrm(ks[1], (DEPTH, D_MODEL)),
        "w_in": nrm(ks[2], (DEPTH, D_MODEL, IN_WIDTH)) * D_MODEL ** -0.5,
        "moba_q_norm": 1.0 + 0.02 * nrm(ks[3], (DEPTH, HEAD_DIM)),
        "moba_k_norm": 1.0 + 0.02 * nrm(ks[4], (DEPTH, HEAD_DIM)),
        "diff_q_norm": 1.0 + 0.02 * nrm(ks[5], (DEPTH, HEAD_DIM)),
        "diff_k_norm": 1.0 + 0.02 * nrm(ks[6], (DEPTH, HEAD_DIM)),
        "lambda_q1": 0.1 * nrm(ks[7], (DEPTH, HEAD_DIM)),
        "lambda_k1": 0.1 * nrm(ks[8], (DEPTH, HEAD_DIM)),
        "lambda_q2": 0.1 * nrm(ks[9], (DEPTH, HEAD_DIM)),
        "lambda_k2": 0.1 * nrm(ks[10], (DEPTH, HEAD_DIM)),
        "diff_subln": 1.0 + 0.02 * nrm(ks[11], (DEPTH, DIFF_VDIM)),
        "w_out": nrm(ks[12], (DEPTH, MIX_WIDTH, D_MODEL)) * MIX_WIDTH ** -0.5,
    }


def reference(x, norm_g, w_in, moba_q_norm, moba_k_norm, diff_q_norm, diff_k_norm,
              lambda_q1, lambda_k1, lambda_q2, lambda_k2, diff_subln, w_out):
    B, S, _ = x.shape
    s_pad = -(-S // MOBA_BLOCK) * MOBA_BLOCK
    moba_slopes = alibi_slopes(MOBA_HEADS)
    diff_slopes = alibi_slopes(DIFF_HEADS)
    split_at = list(np.cumsum([MOBA_WIDTH] * 4 + [DIFF_QK_WIDTH, DIFF_QK_WIDTH, DIFF_WIDTH]))
    for layer in range(DEPTH):
        h = rms_norm(x, norm_g[layer])
        proj = jnp.einsum('bsd,de->bse', h, w_in[layer])
        proj = jnp.pad(proj, ((0, 0), (0, s_pad - S), (0, 0)))
        mq, mk, mv, mg, dq, dk, dv, dg = jnp.split(proj, split_at, axis=-1)

        to_heads = lambda t: t.reshape(B, s_pad, MOBA_HEADS, HEAD_DIM).transpose(0, 2, 1, 3)
        mq = rms_norm(to_heads(mq), moba_q_norm[layer])
        mk = rms_norm(to_heads(mk), moba_k_norm[layer])
        m_out = moba_attention(mq, mk, to_heads(mv), moba_slopes)
        m_out = m_out.transpose(0, 2, 1, 3).reshape(B, s_pad, MOBA_WIDTH) * jax.nn.silu(mg)

        dq = rms_norm(dq.reshape(B, s_pad, DIFF_HEADS, 2, HEAD_DIM).transpose(0, 2, 1, 3, 4), diff_q_norm[layer])
        dk = rms_norm(dk.reshape(B, s_pad, DIFF_HEADS, 2, HEAD_DIM).transpose(0, 2, 1, 3, 4), diff_k_norm[layer])
        dv = dv.reshape(B, s_pad, DIFF_HEADS, DIFF_VDIM).transpose(0, 2, 1, 3)
        lambda_init = 0.8 - 0.6 * math.exp(-0.3 * layer)
        lam = (jnp.exp(jnp.sum(lambda_q1[layer] * lambda_k1[layer]))
               - jnp.exp(jnp.sum(lambda_q2[layer] * lambda_k2[layer])) + lambda_init)
        d_out = diff_attention(dq, dk, dv, lam, diff_slopes)
        d_out = rms_norm(d_out, diff_subln[layer]) * (1.0 - lambda_init)
        d_out = d_out.transpose(0, 2, 1, 3).reshape(B, s_pad, DIFF_WIDTH) * jax.nn.silu(dg)

        mixed = jnp.concatenate([m_out, d_out], axis=-1)[:, :S]
        x = x + jnp.einsum('bse,ed->bsd', mixed, w_out[layer])
    return x
```

```python
import functools
import math

import numpy as np
import jax
import jax.numpy as jnp
from jax import lax
from jax.experimental import pallas as pl
from jax.experimental.pallas import tpu as pltpu

HEAD_DIM = 64
MOBA_BLOCK = 256
MOBA_TOPK = 3
NORM_EPS = 1e-6
NEG_INF = -1e30

LANES = 128
MXU_TILE = 256
ATT_TILE = MOBA_BLOCK
PROJ_ROWS = 512

F32 = jnp.float32
BF16 = jnp.bfloat16


def _dot(a, b):
    return jnp.dot(a, b, preferred_element_type=F32)


def _dot_nt(a, b):
    return lax.dot_general(a, b, (((1,), (1,)), ((), ())), preferred_element_type=F32)


def _alibi_slopes(n_heads):
    return np.asarray(2.0 ** (-8.0 * np.arange(1, n_heads + 1) / n_heads), dtype=np.float32)


def _bias_tables(n_heads, n_tiles):
    slopes = jnp.asarray(_alibi_slopes(n_heads))
    k_loc = lax.broadcasted_iota(jnp.int32, (ATT_TILE, ATT_TILE), 0)
    q_loc = lax.broadcasted_iota(jnp.int32, (ATT_TILE, ATT_TILE), 1)
    rel = (k_loc - q_loc).astype(F32)
    off = slopes[:, None, None] * rel[None]
    diag = jnp.where((k_loc <= q_loc)[None], off, NEG_INF)
    tiles = jnp.stack([off, diag], axis=1)
    d = jnp.arange(n_tiles, dtype=F32)
    offs = -slopes[:, None] * (ATT_TILE * d)[None, :]
    offs = jnp.broadcast_to(offs[:, :, None], (n_heads, n_tiles, ATT_TILE))
    return tiles, offs


def _inproj_kernel(x_ref, g_ref, wa_ref, wvt_ref, gsum_ref, wqm_ref, wkm_ref, wqd_ref, wkd_ref,
                   qm_ref, km_ref, dq_ref, dk_ref, gate_ref, vt_ref, kmean_ref, *, width):
    t = pl.program_id(1)
    rows = x_ref.shape[1]
    blocks = rows // MOBA_BLOCK
    x = x_ref[0]
    ms = jnp.mean(x * x, axis=-1, keepdims=True)
    h = (x * lax.rsqrt(ms + NORM_EPS) * g_ref[...]).astype(BF16)

    gsum = gsum_ref[...]

    def head_norm(y, w):
        sq = (y * y).astype(BF16)
        parts = [_dot(sq[:, c * MXU_TILE:(c + 1) * MXU_TILE], gsum) for c in range(width // MXU_TILE)]
        ssq = jnp.concatenate(parts, axis=1)
        return y * lax.rsqrt(ssq * (1.0 / HEAD_DIM) + NORM_EPS) * w

    def silu(y):
        return y * (1.0 / (1.0 + jnp.exp(-y)))

    y = _dot(h, wa_ref[:, 0 * width:1 * width])
    qm_ref[0] = head_norm(y, wqm_ref[...]).astype(BF16)

    y = _dot(h, wa_ref[:, 1 * width:2 * width])
    kn = head_norm(y, wkm_ref[...])
    km_ref[0] = kn.astype(BF16)
    for s in range(blocks):
        kmean_ref[0, pl.ds(t * blocks + s, 1), :] = jnp.mean(
            kn[s * MOBA_BLOCK:(s + 1) * MOBA_BLOCK], axis=0, keepdims=True)

    y = _dot(h, wa_ref[:, 2 * width:3 * width])
    gate_ref[0, :, 0:width] = silu(y).astype(BF16)

    y = _dot(h, wa_ref[:, 3 * width:4 * width])
    dq_ref[0] = head_norm(y, wqd_ref[...]).astype(BF16)

    y = _dot(h, wa_ref[:, 4 * width:5 * width])
    dk_ref[0] = head_norm(y, wkd_ref[...]).astype(BF16)

    y = _dot(h, wa_ref[:, 5 * width:6 * width])
    gate_ref[0, :, width:2 * width] = silu(y).astype(BF16)

    vt = _dot_nt(wvt_ref[...], h)
    for s in range(blocks):
        vt_ref[0, s] = vt[:, s * MOBA_BLOCK:(s + 1) * MOBA_BLOCK].astype(BF16)


def _inproj(x, g, wa, wvt, gsum, wqm, wkm, wqd, wkd):
    B, S, D = x.shape
    width = wa.shape[1] // 6
    rows = min(PROJ_ROWS, S)
    nblk = S // MOBA_BLOCK
    blocks = rows // MOBA_BLOCK
    full = lambda shape: pl.BlockSpec(shape, lambda b, t: (0,) * len(shape))
    row_spec = lambda w: pl.BlockSpec((1, rows, w), lambda b, t: (b, t, 0))
    act = lambda w: jax.ShapeDtypeStruct((B, S, w), BF16)
    return pl.pallas_call(
        functools.partial(_inproj_kernel, width=width),
        grid=(B, S // rows),
        in_specs=[row_spec(D), full((1, D)), full(wa.shape), full(wvt.shape), full(gsum.shape),
                  full((1, width)), full((1, width)), full((1, width)), full((1, width))],
        out_specs=[row_spec(width), row_spec(width), row_spec(width), row_spec(width),
                   row_spec(2 * width),
                   pl.BlockSpec((1, blocks, 2 * width, MOBA_BLOCK), lambda b, t: (b, t, 0, 0)),
                   pl.BlockSpec((1, nblk, width), lambda b, t: (b, 0, 0))],
        out_shape=[act(width), act(width), act(width), act(width), act(2 * width),
                   jax.ShapeDtypeStruct((B, nblk, 2 * width, MOBA_BLOCK), BF16),
                   jax.ShapeDtypeStruct((B, nblk, width), F32)],
        compiler_params=pltpu.CompilerParams(dimension_semantics=("arbitrary", "arbitrary")),
        name="inproj",
    )(x, g, wa, wvt, gsum, wqm, wkm, wqd, wkd)


def _softmax_step(s, off, vt, m_ref, l_ref, acc_ref, first):
    m_tile = jnp.max(s, axis=0, keepdims=True)
    if first:
        m_new = m_tile
        p = jnp.exp(s - m_new)
        l_ref[...] = jnp.sum(p, axis=0, keepdims=True)
        acc_ref[...] = _dot(vt, p.astype(BF16))
    else:
        m_old = m_ref[...]
        m_new = jnp.maximum(m_old, m_tile + off)
        p = jnp.exp(s - (m_new - off))
        alpha = jnp.exp(m_old - m_new)
        l_ref[...] = alpha * l_ref[...] + jnp.sum(p, axis=0, keepdims=True)
        acc_ref[...] = alpha * acc_ref[...] + _dot(vt, p.astype(BF16))
    m_ref[...] = m_new


def _moba_kernel(q_ref, k_ref, vt_ref, kmean_ref, bias_ref, offs_ref, o_ref,
                 sel_ref, m_ref, l_ref, acc_ref):
    i = pl.program_id(2)
    nblk = kmean_ref.shape[1]
    q2 = q_ref[0]
    lane = lax.broadcasted_iota(jnp.int32, q2.shape, 1)
    zero = jnp.zeros_like(q2)
    qh = [jnp.where(lane < HEAD_DIM, q2, zero), jnp.where(lane >= HEAD_DIM, q2, zero)]

    km = kmean_ref[0]
    km_hi = km.astype(BF16)
    km_lo = (km - km_hi.astype(F32)).astype(BF16)
    blk = lax.broadcasted_iota(jnp.int32, (nblk, ATT_TILE), 0)
    valid = blk < i
    for hh in range(2):
        g = _dot_nt(km_hi, qh[hh]) + _dot_nt(km_lo, qh[hh])
        g = jnp.where(valid, g, -jnp.inf)
        sel = jnp.zeros(g.shape, dtype=jnp.bool_)
        for _ in range(min(MOBA_TOPK, nblk - 1)):
            mx = jnp.max(g, axis=0, keepdims=True)
            first = jnp.min(jnp.where(g == mx, blk, nblk), axis=0, keepdims=True)
            pick = blk == first
            sel = jnp.logical_or(sel, pick)
            g = jnp.where(pick, -jnp.inf, g)
        sel = jnp.logical_and(sel, valid)
        sel_ref[hh] = jnp.where(sel, 0.0, NEG_INF).astype(F32)

    def tile(j, first):
        kt = k_ref[0, pl.ds(pl.multiple_of(j * ATT_TILE, ATT_TILE), ATT_TILE), :]
        vt = vt_ref[0, j]
        for hh in range(2):
            s = _dot_nt(kt, qh[hh])
            if first:
                s = s + bias_ref[hh, 1]
                off = None
            else:
                s = s + bias_ref[hh, 0]
                off = offs_ref[hh, pl.ds(i - j, 1), :] + sel_ref[hh, pl.ds(j, 1), :]
            _softmax_step(s, off, vt[hh * HEAD_DIM:(hh + 1) * HEAD_DIM], m_ref.at[hh], l_ref.at[hh],
                          acc_ref.at[hh], first)

    tile(i, True)

    def body(j, carry):
        tile(j, False)
        return carry

    lax.fori_loop(0, i, body, 0)

    o_t = jnp.concatenate([acc_ref[hh] * (1.0 / l_ref[hh]) for hh in range(2)], axis=0)
    o_ref[0] = o_t.T.astype(BF16)


def _moba(qm, km, vt, kmean, bias, offs):
    B, S, width = qm.shape
    pairs = width // LANES
    nt = S // ATT_TILE
    nblk = kmean.shape[1]
    return pl.pallas_call(
        _moba_kernel,
        grid=(B, pairs, nt),
        in_specs=[pl.BlockSpec((1, ATT_TILE, LANES), lambda b, p, i: (b, i, p)),
                  pl.BlockSpec((1, S, LANES), lambda b, p, i: (b, 0, p)),
                  pl.BlockSpec((1, nblk, LANES, ATT_TILE), lambda b, p, i: (b, 0, p, 0)),
                  pl.BlockSpec((1, nblk, LANES), lambda b, p, i: (b, 0, p)),
                  pl.BlockSpec((2, 2, ATT_TILE, ATT_TILE), lambda b, p, i: (p, 0, 0, 0)),
                  pl.BlockSpec((2, nt, ATT_TILE), lambda b, p, i: (p, 0, 0))],
        out_specs=pl.BlockSpec((1, ATT_TILE, LANES), lambda b, p, i: (b, i, p)),
        out_shape=jax.ShapeDtypeStruct((B, S, width), BF16),
        scratch_shapes=[pltpu.VMEM((2, nblk, ATT_TILE), F32),
                        pltpu.VMEM((2, 1, ATT_TILE), F32),
                        pltpu.VMEM((2, 1, ATT_TILE), F32),
                        pltpu.VMEM((2, HEAD_DIM, ATT_TILE), F32)],
        compiler_params=pltpu.CompilerParams(
            dimension_semantics=("arbitrary", "arbitrary", "arbitrary")),
        name="moba_attn",
    )(qm, km, vt, kmean, bias, offs)


def _diff_kernel(q_ref, k_ref, vt_ref, bias_ref, offs_ref, lamv_ref, subw_ref, o_ref,
                 m_ref, l_ref, acc_ref, *, lambda_init):
    i = pl.program_id(2)
    q2 = q_ref[0]
    lane = lax.broadcasted_iota(jnp.int32, q2.shape, 1)
    zero = jnp.zeros_like(q2)
    qc = [jnp.where(lane < HEAD_DIM, q2, zero), jnp.where(lane >= HEAD_DIM, q2, zero)]

    def tile(j, first):
        kt = k_ref[0, pl.ds(pl.multiple_of(j * ATT_TILE, ATT_TILE), ATT_TILE), :]
        vt = vt_ref[0, j]
        off = None if first else offs_ref[0, pl.ds(i - j, 1), :]
        bias = bias_ref[0, 1] if first else bias_ref[0, 0]
        for c in range(2):
            s = _dot_nt(kt, qc[c]) + bias
            _softmax_step(s, off, vt, m_ref.at[c], l_ref.at[c], acc_ref.at[c], first)

    tile(i, True)

    def body(j, carry):
        tile(j, False)
        return carry

    lax.fori_loop(0, i, body, 0)

    lv = lamv_ref[...]
    lam = (jnp.exp(jnp.sum(lv[0:1] * lv[1:2], axis=-1, keepdims=True))
           - jnp.exp(jnp.sum(lv[2:3] * lv[3:4], axis=-1, keepdims=True)) + lambda_init)
    o_t = acc_ref[0] * (1.0 / l_ref[0]) - lam * (acc_ref[1] * (1.0 / l_ref[1]))
    ms = jnp.mean(o_t * o_t, axis=0, keepdims=True)
    o_n = (o_t * lax.rsqrt(ms + NORM_EPS)).T
    o_ref[0] = (o_n * subw_ref[...] * (1.0 - lambda_init)).astype(BF16)


def _diff(dq, dk, vt, bias, offs, lamv, subw, lambda_init, v_row_block0):
    B, S, width = dq.shape
    heads = width // LANES
    nt = S // ATT_TILE
    nblk = vt.shape[1]
    return pl.pallas_call(
        functools.partial(_diff_kernel, lambda_init=lambda_init),
        grid=(B, heads, nt),
        in_specs=[pl.BlockSpec((1, ATT_TILE, LANES), lambda b, h, i: (b, i, h)),
                  pl.BlockSpec((1, S, LANES), lambda b, h, i: (b, 0, h)),
                  pl.BlockSpec((1, nblk, LANES, ATT_TILE), lambda b, h, i: (b, 0, v_row_block0 + h, 0)),
                  pl.BlockSpec((1, 2, ATT_TILE, ATT_TILE), lambda b, h, i: (h, 0, 0, 0)),
                  pl.BlockSpec((1, nt, ATT_TILE), lambda b, h, i: (h, 0, 0)),
                  pl.BlockSpec(lamv.shape, lambda b, h, i: (0, 0)),
                  pl.BlockSpec((1, LANES), lambda b, h, i: (0, 0))],
        out_specs=pl.BlockSpec((1, ATT_TILE, LANES), lambda b, h, i: (b, i, h)),
        out_shape=jax.ShapeDtypeStruct((B, S, width), BF16),
        scratch_shapes=[pltpu.VMEM((2, 1, ATT_TILE), F32),
                        pltpu.VMEM((2, 1, ATT_TILE), F32),
                        pltpu.VMEM((2, LANES, ATT_TILE), F32)],
        compiler_params=pltpu.CompilerParams(
            dimension_semantics=("arbitrary", "arbitrary", "arbitrary")),
        name="diff_attn",
    )(dq, dk, vt, bias, offs, lamv, subw)


def _outproj_kernel(x_ref, m_ref, d_ref, gate_ref, w_ref, o_ref):
    width = m_ref.shape[2]
    g = gate_ref[0].astype(F32)
    mg = (m_ref[0].astype(F32) * g[:, 0:width]).astype(BF16)
    dg = (d_ref[0].astype(F32) * g[:, width:2 * width]).astype(BF16)
    o_ref[0] = x_ref[0] + _dot(mg, w_ref[0:width, :]) + _dot(dg, w_ref[width:2 * width, :])


def _outproj(x, m_att, d_att, gates, w_out):
    B, S, D = x.shape
    width = m_att.shape[2]
    rows = min(PROJ_ROWS, S)
    row_spec = lambda w: pl.BlockSpec((1, rows, w), lambda b, t: (b, t, 0))
    return pl.pallas_call(
        _outproj_kernel,
        grid=(B, S // rows),
        in_specs=[row_spec(D), row_spec(width), row_spec(width), row_spec(2 * width),
                  pl.BlockSpec(w_out.shape, lambda b, t: (0, 0))],
        out_specs=row_spec(D),
        out_shape=jax.ShapeDtypeStruct((B, S, D), F32),
        compiler_params=pltpu.CompilerParams(dimension_semantics=("arbitrary", "arbitrary")),
        name="outproj",
    )(x, m_att, d_att, gates, w_out)


def kernel(x, norm_g, w_in, moba_q_norm, moba_k_norm, diff_q_norm, diff_k_norm,
           lambda_q1, lambda_k1, lambda_q2, lambda_k2, diff_subln, w_out):
    B, S, D = x.shape
    depth = w_in.shape[0]
    width = D // 2
    moba_heads = width // HEAD_DIM
    diff_heads = width // (2 * HEAD_DIM)
    assert S % MOBA_BLOCK == 0 and width % MXU_TILE == 0 and S // MOBA_BLOCK > 1
    nt = S // ATT_TILE
    scale = HEAD_DIM ** -0.5

    m_bias, m_offs = _bias_tables(moba_heads, nt)
    d_bias, d_offs = _bias_tables(diff_heads, nt)
    r = lax.broadcasted_iota(jnp.int32, (MXU_TILE, MXU_TILE), 0) // HEAD_DIM
    c = lax.broadcasted_iota(jnp.int32, (MXU_TILE, MXU_TILE), 1) // HEAD_DIM
    gsum = (r == c).astype(BF16)
    reps = width // HEAD_DIM

    for layer in range(depth):
        w = w_in[layer]
        cols = [w[:, k * width:(k + 1) * width] for k in range(8)]
        wa = jnp.concatenate([cols[0], cols[1], cols[3], cols[4], cols[5], cols[7]], axis=1).astype(BF16)
        wvt = jnp.concatenate([cols[2], cols[6]], axis=1).T.astype(BF16)
        tile_w = lambda v, s: (jnp.tile(v, reps) * s)[None, :].astype(F32)
        qm, km, dq, dk, gates, vt, kmean = _inproj(
            x, norm_g[layer][None, :], wa, wvt, gsum,
            tile_w(moba_q_norm[layer], scale), tile_w(moba_k_norm[layer], 1.0),
            tile_w(diff_q_norm[layer], scale), tile_w(diff_k_norm[layer], 1.0))
        m_att = _moba(qm, km, vt, kmean, m_bias, m_offs)
        lambda_init = 0.8 - 0.6 * math.exp(-0.3 * layer)
        lamv = jnp.stack([lambda_q1[layer], lambda_k1[layer], lambda_q2[layer], lambda_k2[layer]])
        d_att = _diff(dq, dk, vt, d_bias, d_offs, lamv, diff_subln[layer][None, :], lambda_init,
                      width // LANES)
        x = _outproj(x, m_att, d_att, gates, w_out[layer].astype(BF16))
    return x
```

```python
import functools
import math

import numpy as np
import jax
import jax.numpy as jnp
from jax import lax
from jax.experimental import pallas as pl
from jax.experimental.pallas import tpu as pltpu

HEAD_DIM = 64
MOBA_BLOCK = 256
MOBA_TOPK = 3
NORM_EPS = 1e-6
NEG_INF = -1e30

LANES = 128
MXU_TILE = 256
ATT_TILE = MOBA_BLOCK
PROJ_ROWS = 512
LOG2E = math.log2(math.e)

F32 = jnp.float32
BF16 = jnp.bfloat16


def _dot(a, b):
    return jnp.dot(a, b, preferred_element_type=F32)


def _dot_nt(a, b):
    return lax.dot_general(a, b, (((1,), (1,)), ((), ())), preferred_element_type=F32)


def _alibi_slopes(n_heads):
    return np.asarray(2.0 ** (-8.0 * np.arange(1, n_heads + 1) / n_heads), dtype=np.float32)


def _bias_tables(n_heads, n_tiles):
    slopes = jnp.asarray(_alibi_slopes(n_heads) * np.float32(LOG2E))
    k_loc = lax.broadcasted_iota(jnp.int32, (ATT_TILE, ATT_TILE), 0)
    q_loc = lax.broadcasted_iota(jnp.int32, (ATT_TILE, ATT_TILE), 1)
    rel = (k_loc - q_loc).astype(F32)
    off = slopes[:, None, None] * rel[None]
    diag = jnp.where((k_loc <= q_loc)[None], off, NEG_INF)
    tiles = jnp.stack([off, diag], axis=1)
    d = jnp.arange(n_tiles, dtype=F32)
    offs = -slopes[:, None] * (ATT_TILE * d)[None, :]
    offs = jnp.broadcast_to(offs[:, :, None], (n_heads, n_tiles, ATT_TILE))
    return tiles, offs


def _inproj_kernel(x_ref, g_ref, wa_ref, wvt_ref, gsum_ref, wqm_ref, wkm_ref, wqd_ref, wkd_ref,
                   qm_ref, km_ref, dq_ref, dk_ref, gate_ref, vt_ref, kmean_ref, *, width):
    t = pl.program_id(1)
    rows = x_ref.shape[1]
    blocks = rows // MOBA_BLOCK
    x = x_ref[0]
    ms = jnp.mean(x * x, axis=-1, keepdims=True)
    h = (x * lax.rsqrt(ms + NORM_EPS) * g_ref[...]).astype(BF16)

    gsum = gsum_ref[...]

    def head_norm(y, w):
        sq = (y * y).astype(BF16)
        parts = [_dot(sq[:, c * MXU_TILE:(c + 1) * MXU_TILE], gsum) for c in range(width // MXU_TILE)]
        ssq = jnp.concatenate(parts, axis=1)
        return y * lax.rsqrt(ssq * (1.0 / HEAD_DIM) + NORM_EPS) * w

    def silu(y):
        return y * (1.0 / (1.0 + jnp.exp(-y)))

    y = _dot(h, wa_ref[:, 0 * width:1 * width])
    qm_ref[0] = head_norm(y, wqm_ref[...]).astype(BF16)

    y = _dot(h, wa_ref[:, 1 * width:2 * width])
    kn = head_norm(y, wkm_ref[...])
    km_ref[0] = kn.astype(BF16)
    for s in range(blocks):
        kmean_ref[0, pl.ds(t * blocks + s, 1), :] = jnp.mean(
            kn[s * MOBA_BLOCK:(s + 1) * MOBA_BLOCK], axis=0, keepdims=True)

    y = _dot(h, wa_ref[:, 2 * width:3 * width])
    gate_ref[0, :, 0:width] = silu(y).astype(BF16)

    y = _dot(h, wa_ref[:, 3 * width:4 * width])
    dq_ref[0] = head_norm(y, wqd_ref[...]).astype(BF16)

    y = _dot(h, wa_ref[:, 4 * width:5 * width])
    dk_ref[0] = head_norm(y, wkd_ref[...]).astype(BF16)

    y = _dot(h, wa_ref[:, 5 * width:6 * width])
    gate_ref[0, :, width:2 * width] = silu(y).astype(BF16)

    vt = _dot_nt(wvt_ref[...], h)
    for s in range(blocks):
        vt_ref[0, s] = vt[:, s * MOBA_BLOCK:(s + 1) * MOBA_BLOCK].astype(BF16)


def _inproj(x, g, wa, wvt, gsum, wqm, wkm, wqd, wkd):
    B, S, D = x.shape
    width = wa.shape[1] // 6
    rows = min(PROJ_ROWS, S)
    nblk = S // MOBA_BLOCK
    blocks = rows // MOBA_BLOCK
    full = lambda shape: pl.BlockSpec(shape, lambda b, t: (0,) * len(shape))
    row_spec = lambda w: pl.BlockSpec((1, rows, w), lambda b, t: (b, t, 0))
    act = lambda w: jax.ShapeDtypeStruct((B, S, w), BF16)
    return pl.pallas_call(
        functools.partial(_inproj_kernel, width=width),
        grid=(B, S // rows),
        in_specs=[row_spec(D), full((1, D)), full(wa.shape), full(wvt.shape), full(gsum.shape),
                  full((1, width)), full((1, width)), full((1, width)), full((1, width))],
        out_specs=[row_spec(width), row_spec(width), row_spec(width), row_spec(width),
                   row_spec(2 * width),
                   pl.BlockSpec((1, blocks, 2 * width, MOBA_BLOCK), lambda b, t: (b, t, 0, 0)),
                   pl.BlockSpec((1, nblk, width), lambda b, t: (b, 0, 0))],
        out_shape=[act(width), act(width), act(width), act(width), act(2 * width),
                   jax.ShapeDtypeStruct((B, nblk, 2 * width, MOBA_BLOCK), BF16),
                   jax.ShapeDtypeStruct((B, nblk, width), F32)],
        compiler_params=pltpu.CompilerParams(dimension_semantics=("arbitrary", "arbitrary")),
        name="inproj",
    )(x, g, wa, wvt, gsum, wqm, wkm, wqd, wkd)


def _attn_tiles(i, stage1, stage2):
    meta_a = stage1(i, 0, True)
    npairs = i // 2

    def body(p, meta_a):
        j1 = 2 * p
        j_a = jnp.where(p == 0, i, j1 - 1)
        meta_b = stage1(j1, 1, False)
        stage2(j_a, 0, meta_a)
        meta_a = stage1(j1 + 1, 0, False)
        stage2(j1, 1, meta_b)
        return meta_a

    meta_a = lax.fori_loop(0, npairs, body, meta_a)
    j_a = jnp.where(npairs == 0, i, 2 * npairs - 1)

    @pl.when(i % 2 == 1)
    def _():
        meta_b = stage1(i - 1, 1, False)
        stage2(j_a, 0, meta_a)
        stage2(i - 1, 1, meta_b)

    @pl.when(i % 2 == 0)
    def _():
        stage2(j_a, 0, meta_a)


def _fold_tile(s_ref, m_tile_off, off, vt, m_ref, l_ref, acc_ref):
    m_old = m_ref[...]
    m_new = jnp.maximum(m_old, m_tile_off)
    p = jnp.exp2(s_ref[...] - (m_new - off))
    alpha = jnp.exp2(m_old - m_new)
    l_ref[...] = alpha * l_ref[...] + jnp.sum(p, axis=0, keepdims=True)
    acc_ref[...] = alpha * acc_ref[...] + _dot(vt, p.astype(BF16))
    m_ref[...] = m_new


def _split_lanes(q2):
    lane = lax.broadcasted_iota(jnp.int32, q2.shape, 1)
    zero = jnp.zeros_like(q2)
    return [jnp.where(lane < HEAD_DIM, q2, zero), jnp.where(lane >= HEAD_DIM, q2, zero)]


def _init_state(m_ref, l_ref, acc_ref):
    m_ref[...] = jnp.full(m_ref.shape, NEG_INF, F32)
    l_ref[...] = jnp.zeros(l_ref.shape, F32)
    acc_ref[...] = jnp.zeros(acc_ref.shape, F32)


def _moba_kernel(q_ref, k_ref, vt_ref, kmean_ref, bias_ref, offs_ref, o_ref,
                 sel_ref, s_ref, m_ref, l_ref, acc_ref):
    i = pl.program_id(2)
    nblk = kmean_ref.shape[1]
    qh = _split_lanes(q_ref[0])
    _init_state(m_ref, l_ref, acc_ref)

    km = kmean_ref[0]
    km_hi = km.astype(BF16)
    km_lo = (km - km_hi.astype(F32)).astype(BF16)
    blk = lax.broadcasted_iota(jnp.int32, (nblk, ATT_TILE), 0)
    valid = blk < i
    for hh in range(2):
        g = _dot_nt(km_hi, qh[hh]) + _dot_nt(km_lo, qh[hh])
        g = jnp.where(valid, g, -jnp.inf)
        sel = jnp.zeros(g.shape, dtype=jnp.bool_)
        for _ in range(min(MOBA_TOPK, nblk - 1)):
            mx = jnp.max(g, axis=0, keepdims=True)
            first = jnp.min(jnp.where(g == mx, blk, nblk), axis=0, keepdims=True)
            pick = blk == first
            sel = jnp.logical_or(sel, pick)
            g = jnp.where(pick, -jnp.inf, g)
        sel = jnp.logical_and(sel, valid)
        sel_ref[hh] = jnp.where(sel, 0.0, NEG_INF).astype(F32)

    def stage1(j, buf, diag):
        kt = k_ref[0, pl.ds(pl.multiple_of(j * ATT_TILE, ATT_TILE), ATT_TILE), :]
        meta = []
        for hh in range(2):
            s = _dot_nt(kt, qh[hh]) + bias_ref[hh, 1 if diag else 0]
            s_ref[buf, hh] = s
            if diag:
                off = jnp.zeros((1, ATT_TILE), F32)
            else:
                off = offs_ref[hh, pl.ds(i - j, 1), :] + sel_ref[hh, pl.ds(j, 1), :]
            meta += [jnp.max(s, axis=0, keepdims=True) + off, off]
        return tuple(meta)

    def stage2(j, buf, meta):
        vt = vt_ref[0, j]
        for hh in range(2):
            _fold_tile(s_ref.at[buf, hh], meta[2 * hh], meta[2 * hh + 1],
                       vt[hh * HEAD_DIM:(hh + 1) * HEAD_DIM],
                       m_ref.at[hh], l_ref.at[hh], acc_ref.at[hh])

    _attn_tiles(i, stage1, stage2)

    o_t = jnp.concatenate([acc_ref[hh] * (1.0 / l_ref[hh]) for hh in range(2)], axis=0)
    o_ref[0] = o_t.T.astype(BF16)


def _moba(qm, km, vt, kmean, bias, offs):
    B, S, width = qm.shape
    pairs = width // LANES
    nt = S // ATT_TILE
    nblk = kmean.shape[1]
    return pl.pallas_call(
        _moba_kernel,
        grid=(B, pairs, nt),
        in_specs=[pl.BlockSpec((1, ATT_TILE, LANES), lambda b, p, i: (b, i, p)),
                  pl.BlockSpec((1, S, LANES), lambda b, p, i: (b, 0, p)),
                  pl.BlockSpec((1, nblk, LANES, ATT_TILE), lambda b, p, i: (b, 0, p, 0)),
                  pl.BlockSpec((1, nblk, LANES), lambda b, p, i: (b, 0, p)),
                  pl.BlockSpec((2, 2, ATT_TILE, ATT_TILE), lambda b, p, i: (p, 0, 0, 0)),
                  pl.BlockSpec((2, nt, ATT_TILE), lambda b, p, i: (p, 0, 0))],
        out_specs=pl.BlockSpec((1, ATT_TILE, LANES), lambda b, p, i: (b, i, p)),
        out_shape=jax.ShapeDtypeStruct((B, S, width), BF16),
        scratch_shapes=[pltpu.VMEM((2, nblk, ATT_TILE), F32),
                        pltpu.VMEM((2, 2, ATT_TILE, ATT_TILE), F32),
                        pltpu.VMEM((2, 1, ATT_TILE), F32),
                        pltpu.VMEM((2, 1, ATT_TILE), F32),
                        pltpu.VMEM((2, HEAD_DIM, ATT_TILE), F32)],
        compiler_params=pltpu.CompilerParams(
            dimension_semantics=("arbitrary", "arbitrary", "arbitrary")),
        name="moba_attn",
    )(qm, km, vt, kmean, bias, offs)


def _diff_kernel(q_ref, k_ref, vt_ref, bias_ref, offs_ref, lamv_ref, subw_ref, o_ref,
                 s_ref, m_ref, l_ref, acc_ref, *, lambda_init):
    i = pl.program_id(2)
    qc = _split_lanes(q_ref[0])
    _init_state(m_ref, l_ref, acc_ref)

    def stage1(j, buf, diag):
        kt = k_ref[0, pl.ds(pl.multiple_of(j * ATT_TILE, ATT_TILE), ATT_TILE), :]
        bias = bias_ref[0, 1 if diag else 0]
        off = jnp.zeros((1, ATT_TILE), F32) if diag else offs_ref[0, pl.ds(i - j, 1), :]
        meta = []
        for c in range(2):
            s = _dot_nt(kt, qc[c]) + bias
            s_ref[buf, c] = s
            meta += [jnp.max(s, axis=0, keepdims=True) + off, off]
        return tuple(meta)

    def stage2(j, buf, meta):
        vt = vt_ref[0, j]
        for c in range(2):
            _fold_tile(s_ref.at[buf, c], meta[2 * c], meta[2 * c + 1], vt,
                       m_ref.at[c], l_ref.at[c], acc_ref.at[c])

    _attn_tiles(i, stage1, stage2)

    lv = lamv_ref[...]
    lam = (jnp.exp(jnp.sum(lv[0:1] * lv[1:2], axis=-1, keepdims=True))
           - jnp.exp(jnp.sum(lv[2:3] * lv[3:4], axis=-1, keepdims=True)) + lambda_init)
    o_t = acc_ref[0] * (1.0 / l_ref[0]) - lam * (acc_ref[1] * (1.0 / l_ref[1]))
    ms = jnp.mean(o_t * o_t, axis=0, keepdims=True)
    o_n = (o_t * lax.rsqrt(ms + NORM_EPS)).T
    o_ref[0] = (o_n * subw_ref[...] * (1.0 - lambda_init)).astype(BF16)


def _diff(dq, dk, vt, bias, offs, lamv, subw, lambda_init, v_row_block0):
    B, S, width = dq.shape
    heads = width // LANES
    nt = S // ATT_TILE
    nblk = vt.shape[1]
    return pl.pallas_call(
        functools.partial(_diff_kernel, lambda_init=lambda_init),
        grid=(B, heads, nt),
        in_specs=[pl.BlockSpec((1, ATT_TILE, LANES), lambda b, h, i: (b, i, h)),
                  pl.BlockSpec((1, S, LANES), lambda b, h, i: (b, 0, h)),
                  pl.BlockSpec((1, nblk, LANES, ATT_TILE), lambda b, h, i: (b, 0, v_row_block0 + h, 0)),
                  pl.BlockSpec((1, 2, ATT_TILE, ATT_TILE), lambda b, h, i: (h, 0, 0, 0)),
                  pl.BlockSpec((1, nt, ATT_TILE), lambda b, h, i: (h, 0, 0)),
                  pl.BlockSpec(lamv.shape, lambda b, h, i: (0, 0)),
                  pl.BlockSpec((1, LANES), lambda b, h, i: (0, 0))],
        out_specs=pl.BlockSpec((1, ATT_TILE, LANES), lambda b, h, i: (b, i, h)),
        out_shape=jax.ShapeDtypeStruct((B, S, width), BF16),
        scratch_shapes=[pltpu.VMEM((2, 2, ATT_TILE, ATT_TILE), F32),
                        pltpu.VMEM((2, 1, ATT_TILE), F32),
                        pltpu.VMEM((2, 1, ATT_TILE), F32),
                        pltpu.VMEM((2, LANES, ATT_TILE), F32)],
        compiler_params=pltpu.CompilerParams(
            dimension_semantics=("arbitrary", "arbitrary", "arbitrary")),
        name="diff_attn",
    )(dq, dk, vt, bias, offs, lamv, subw)


def _outproj_kernel(x_ref, m_ref, d_ref, gate_ref, w_ref, o_ref):
    width = m_ref.shape[2]
    g = gate_ref[0].astype(F32)
    mg = (m_ref[0].astype(F32) * g[:, 0:width]).astype(BF16)
    dg = (d_ref[0].astype(F32) * g[:, width:2 * width]).astype(BF16)
    o_ref[0] = x_ref[0] + _dot(mg, w_ref[0:width, :]) + _dot(dg, w_ref[width:2 * width, :])


def _outproj(x, m_att, d_att, gates, w_out):
    B, S, D = x.shape
    width = m_att.shape[2]
    rows = min(PROJ_ROWS, S)
    row_spec = lambda w: pl.BlockSpec((1, rows, w), lambda b, t: (b, t, 0))
    return pl.pallas_call(
        _outproj_kernel,
        grid=(B, S // rows),
        in_specs=[row_spec(D), row_spec(width), row_spec(width), row_spec(2 * width),
                  pl.BlockSpec(w_out.shape, lambda b, t: (0, 0))],
        out_specs=row_spec(D),
        out_shape=jax.ShapeDtypeStruct((B, S, D), F32),
        compiler_params=pltpu.CompilerParams(dimension_semantics=("arbitrary", "arbitrary")),
        name="outproj",
    )(x, m_att, d_att, gates, w_out)


def kernel(x, norm_g, w_in, moba_q_norm, moba_k_norm, diff_q_norm, diff_k_norm,
           lambda_q1, lambda_k1, lambda_q2, lambda_k2, diff_subln, w_out):
    B, S, D = x.shape
    depth = w_in.shape[0]
    width = D // 2
    moba_heads = width // HEAD_DIM
    diff_heads = width // (2 * HEAD_DIM)
    assert S % MOBA_BLOCK == 0 and width % MXU_TILE == 0 and S // MOBA_BLOCK > 1
    nt = S // ATT_TILE
    scale = HEAD_DIM ** -0.5 * LOG2E

    m_bias, m_offs = _bias_tables(moba_heads, nt)
    d_bias, d_offs = _bias_tables(diff_heads, nt)
    r = lax.broadcasted_iota(jnp.int32, (MXU_TILE, MXU_TILE), 0) // HEAD_DIM
    c = lax.broadcasted_iota(jnp.int32, (MXU_TILE, MXU_TILE), 1) // HEAD_DIM
    gsum = (r == c).astype(BF16)
    reps = width // HEAD_DIM

    for layer in range(depth):
        w = w_in[layer]
        cols = [w[:, k * width:(k + 1) * width] for k in range(8)]
        wa = jnp.concatenate([cols[0], cols[1], cols[3], cols[4], cols[5], cols[7]], axis=1).astype(BF16)
        wvt = jnp.concatenate([cols[2], cols[6]], axis=1).T.astype(BF16)
        tile_w = lambda v, s: (jnp.tile(v, reps) * s)[None, :].astype(F32)
        qm, km, dq, dk, gates, vt, kmean = _inproj(
            x, norm_g[layer][None, :], wa, wvt, gsum,
            tile_w(moba_q_norm[layer], scale), tile_w(moba_k_norm[layer], 1.0),
            tile_w(diff_q_norm[layer], scale), tile_w(diff_k_norm[layer], 1.0))
        m_att = _moba(qm, km, vt, kmean, m_bias, m_offs)
        lambda_init = 0.8 - 0.6 * math.exp(-0.3 * layer)
        lamv = jnp.stack([lambda_q1[layer], lambda_k1[layer], lambda_q2[layer], lambda_k2[layer]])
        d_att = _diff(dq, dk, vt, d_bias, d_offs, lamv, diff_subln[layer][None, :], lambda_init,
                      width // LANES)
        x = _outproj(x, m_att, d_att, gates, w_out[layer].astype(BF16))
    return x
```

```python
import functools
import math

import numpy as np
import jax
import jax.numpy as jnp
from jax import lax
from jax.experimental import pallas as pl
from jax.experimental.pallas import tpu as pltpu

HEAD_DIM = 64
MOBA_BLOCK = 256
MOBA_TOPK = 3
NORM_EPS = 1e-6
NEG_INF = -1e30

LANES = 128
MXU_TILE = 256
ATT_TILE = MOBA_BLOCK
PROJ_ROWS = 512
LOG2E = math.log2(math.e)

F32 = jnp.float32
BF16 = jnp.bfloat16


def _dot(a, b):
    return jnp.dot(a, b, preferred_element_type=F32)


def _dot_nt(a, b):
    return lax.dot_general(a, b, (((1,), (1,)), ((), ())), preferred_element_type=F32)


def _alibi_slopes(n_heads):
    return np.asarray(2.0 ** (-8.0 * np.arange(1, n_heads + 1) / n_heads), dtype=np.float32)


def _bias_tables(n_heads, n_tiles):
    slopes = jnp.asarray(_alibi_slopes(n_heads) * np.float32(LOG2E))
    k_loc = lax.broadcasted_iota(jnp.int32, (ATT_TILE, ATT_TILE), 0)
    q_loc = lax.broadcasted_iota(jnp.int32, (ATT_TILE, ATT_TILE), 1)
    rel = (k_loc - q_loc).astype(F32)
    off = slopes[:, None, None] * rel[None]
    diag = jnp.where((k_loc <= q_loc)[None], off, NEG_INF)
    tiles = jnp.stack([off, diag], axis=1)
    d = jnp.arange(n_tiles, dtype=F32)
    offs = -slopes[:, None] * (ATT_TILE * d)[None, :]
    offs = jnp.broadcast_to(offs[:, :, None], (n_heads, n_tiles, ATT_TILE))
    return tiles, offs


def _inproj_kernel(x_ref, g_ref, wa_ref, wvt_ref, gsum_ref, wqm_ref, wkm_ref, wqd_ref, wkd_ref,
                   qm_ref, km_ref, dq_ref, dk_ref, gate_ref, vt_ref, kmean_ref, *, width):
    t = pl.program_id(1)
    rows = x_ref.shape[1]
    blocks = rows // MOBA_BLOCK
    x = x_ref[0]
    ms = jnp.mean(x * x, axis=-1, keepdims=True)
    h = (x * lax.rsqrt(ms + NORM_EPS) * g_ref[...]).astype(BF16)

    gsum = gsum_ref[...]
    low_half = lax.broadcasted_iota(jnp.int32, (rows, width), 1) % LANES < HEAD_DIM

    def head_norm(y, w):
        sq = (y * y).astype(BF16)
        parts = [_dot(sq[:, c * MXU_TILE:(c + 1) * MXU_TILE], gsum) for c in range(width // MXU_TILE)]
        ssq = jnp.concatenate(parts, axis=1)
        return y * lax.rsqrt(ssq * (1.0 / HEAD_DIM) + NORM_EPS) * w

    def store_split(ref, q):
        ref[0, :, 0:width] = jnp.where(low_half, q, 0.0).astype(BF16)
        ref[0, :, width:2 * width] = jnp.where(low_half, 0.0, q).astype(BF16)

    def silu(y):
        return y * (1.0 / (1.0 + jnp.exp(-y)))

    y = _dot(h, wa_ref[:, 0 * width:1 * width])
    store_split(qm_ref, head_norm(y, wqm_ref[...]))

    y = _dot(h, wa_ref[:, 1 * width:2 * width])
    kn = head_norm(y, wkm_ref[...])
    km_ref[0] = kn.astype(BF16)
    for s in range(blocks):
        kmean_ref[0, pl.ds(t * blocks + s, 1), :] = jnp.mean(
            kn[s * MOBA_BLOCK:(s + 1) * MOBA_BLOCK], axis=0, keepdims=True)

    y = _dot(h, wa_ref[:, 2 * width:3 * width])
    gate_ref[0, :, 0:width] = silu(y).astype(BF16)

    y = _dot(h, wa_ref[:, 3 * width:4 * width])
    store_split(dq_ref, head_norm(y, wqd_ref[...]))

    y = _dot(h, wa_ref[:, 4 * width:5 * width])
    dk_ref[0] = head_norm(y, wkd_ref[...]).astype(BF16)

    y = _dot(h, wa_ref[:, 5 * width:6 * width])
    gate_ref[0, :, width:2 * width] = silu(y).astype(BF16)

    vt = _dot_nt(wvt_ref[...], h)
    for s in range(blocks):
        vt_ref[0, s] = vt[:, s * MOBA_BLOCK:(s + 1) * MOBA_BLOCK].astype(BF16)


def _inproj(x, g, wa, wvt, gsum, wqm, wkm, wqd, wkd):
    B, S, D = x.shape
    width = wa.shape[1] // 6
    rows = min(PROJ_ROWS, S)
    nblk = S // MOBA_BLOCK
    blocks = rows // MOBA_BLOCK
    full = lambda shape: pl.BlockSpec(shape, lambda b, t: (0,) * len(shape))
    row_spec = lambda w: pl.BlockSpec((1, rows, w), lambda b, t: (b, t, 0))
    act = lambda w: jax.ShapeDtypeStruct((B, S, w), BF16)
    return pl.pallas_call(
        functools.partial(_inproj_kernel, width=width),
        grid=(B, S // rows),
        in_specs=[row_spec(D), full((1, D)), full(wa.shape), full(wvt.shape), full(gsum.shape),
                  full((1, width)), full((1, width)), full((1, width)), full((1, width))],
        out_specs=[row_spec(2 * width), row_spec(width), row_spec(2 * width), row_spec(width),
                   row_spec(2 * width),
                   pl.BlockSpec((1, blocks, 2 * width, MOBA_BLOCK), lambda b, t: (b, t, 0, 0)),
                   pl.BlockSpec((1, nblk, width), lambda b, t: (b, 0, 0))],
        out_shape=[act(2 * width), act(width), act(2 * width), act(width), act(2 * width),
                   jax.ShapeDtypeStruct((B, nblk, 2 * width, MOBA_BLOCK), BF16),
                   jax.ShapeDtypeStruct((B, nblk, width), F32)],
        compiler_params=pltpu.CompilerParams(dimension_semantics=("arbitrary", "arbitrary")),
        name="inproj",
    )(x, g, wa, wvt, gsum, wqm, wkm, wqd, wkd)


def _pair_table(n_tiles):
    rows = [(i, p, int(p == i // 2)) for i in range(n_tiles) for p in range(i // 2 + 1)]
    rows += [(rows[-1][0], rows[-1][1], 0)] * 2
    return np.ascontiguousarray(np.asarray(rows, dtype=np.int32).T)


def _item(i, t):
    t = jnp.asarray(t, jnp.int32)
    is_diag = t == 0
    j = jnp.where(is_diag, i, t - 1)
    is_pad = jnp.logical_and(t > 0, t - 1 >= i)
    return j, is_diag, is_pad


def _rows(i):
    return pl.ds(pl.multiple_of(i * ATT_TILE, ATT_TILE), ATT_TILE)


def _pad_off(is_pad):
    return jnp.where(is_pad, jnp.float32(NEG_INF), jnp.float32(0.0))


def _sweep(tab_ref, n_maps, score_map, soft, value_map, finish):
    n_pairs = tab_ref.shape[1] - 2
    assert n_pairs % 2 == 0
    pair = lambda g: (tab_ref[0, g], tab_ref[1, g])
    row = lambda v: tuple(jnp.full((1, ATT_TILE), v, F32) for _ in range(n_maps))

    def score_all(g, slot):
        return tuple(x for c in range(n_maps) for x in score_map(*pair(g), slot, c))

    def half_step(g, slot, carry):
        colmax, m, l, alpha = carry
        colmax1 = []
        for c in range(n_maps):
            colmax1 += score_map(*pair(g), slot, c)
            value_map(*pair(g - 2), slot, c, alpha[c])
        m1, l1, alpha1 = soft(*pair(g - 1), 1 - slot, colmax, m, l)
        return tuple(colmax1), m1, l1, alpha1

    colmax = score_all(0, 0)
    m, l, alpha = soft(*pair(0), 0, colmax, row(NEG_INF), row(0.0))
    carry = (score_all(1, 1), m, l, alpha)

    def step(h, carry):
        g = 2 + 2 * h
        mid = half_step(g, 0, carry)
        out = half_step(g + 1, 1, mid)

        @pl.when(tab_ref[2, g - 2] == 1)
        def _():
            finish(tab_ref[0, g - 2], carry[2])

        @pl.when(tab_ref[2, g - 1] == 1)
        def _():
            finish(tab_ref[0, g - 1], mid[2])

        return out

    lax.fori_loop(0, n_pairs // 2, step, carry)


def _score_map(i, p, slot, c, q_ref, k_ref, bias_of, s_ref):
    q = q_ref[0, _rows(i), :]
    items = [_item(i, 2 * p + u) for u in range(2)]
    kt = jnp.concatenate([k_ref[0, _rows(j), :] for j, _, _ in items], axis=0)
    s2 = _dot_nt(kt, q)
    colmax = []
    for u, (_, is_diag, _) in enumerate(items):
        s = s2[u * ATT_TILE:(u + 1) * ATT_TILE] + bias_of(is_diag.astype(jnp.int32))
        s_ref[slot, u, c] = s
        colmax.append(jnp.max(s, axis=0, keepdims=True))
    return colmax


def _soft(p, slot, colmax, offs, m, l, s_ref, p_ref):
    n_maps = len(m)
    m_out, l_out, alpha_out = [], [], []
    for c in range(n_maps):
        m_old = jnp.where(p == 0, jnp.float32(NEG_INF), m[c])
        m_new = jnp.maximum(m_old, jnp.maximum(colmax[2 * c] + offs[0][c], colmax[2 * c + 1] + offs[1][c]))
        alpha = jnp.exp2(m_old - m_new)
        l_new = alpha * l[c]
        for u in range(2):
            pu = jnp.exp2(s_ref[slot, u, c] - (m_new - offs[u][c]))
            l_new = l_new + jnp.sum(pu, axis=0, keepdims=True)
            p_ref[slot, u, c] = pu.astype(BF16)
        m_out.append(m_new)
        l_out.append(l_new)
        alpha_out.append(alpha)
    return tuple(m_out), tuple(l_out), tuple(alpha_out)


def _value_map(i, slot, c, alpha_c, vts, p_ref, acc_ref):
    par = i % 2
    acc_ref[par, c] = (alpha_c * acc_ref[par, c] + _dot(vts[0], p_ref[slot, 0, c])
                       + _dot(vts[1], p_ref[slot, 1, c]))


def _moba_kernel(tab_ref, qa_ref, qb_ref, k_ref, vt_ref, kmean_ref, bias_ref, offs_ref, o_ref,
                 sel_ref, s_ref, p_ref, acc_ref):
    nblk = kmean_ref.shape[1]
    n_tiles = k_ref.shape[1] // ATT_TILE
    q_refs = (qa_ref, qb_ref)
    acc_ref[...] = jnp.zeros(acc_ref.shape, F32)

    km = kmean_ref[0]
    km_hi = km.astype(BF16)
    km_lo = (km - km_hi.astype(F32)).astype(BF16)
    blk = lax.broadcasted_iota(jnp.int32, (nblk, ATT_TILE), 0)
    for i in range(n_tiles):
        valid = blk < i
        for hh in range(2):
            q = q_refs[hh][0, i * ATT_TILE:(i + 1) * ATT_TILE, :]
            g = _dot_nt(km_hi, q) + _dot_nt(km_lo, q)
            g = jnp.where(valid, g, -jnp.inf)
            sel = blk == i
            for _ in range(min(MOBA_TOPK, nblk - 1)):
                mx = jnp.max(g, axis=0, keepdims=True)
                first = jnp.min(jnp.where(g == mx, blk, nblk), axis=0, keepdims=True)
                pick = blk == first
                sel = jnp.logical_or(sel, jnp.logical_and(pick, valid))
                g = jnp.where(pick, -jnp.inf, g)
            sel_ref[i, hh] = jnp.where(sel, 0.0, NEG_INF).astype(F32)

    def score_map(i, p, slot, hh):
        return _score_map(i, p, slot, hh, q_refs[hh], k_ref, lambda d: bias_ref[hh, d], s_ref)

    def soft(i, p, slot, colmax, m, l):
        offs = []
        for u in range(2):
            j, _, is_pad = _item(i, 2 * p + u)
            pad = _pad_off(is_pad)
            offs.append([offs_ref[hh, pl.ds(i - j, 1), :] + sel_ref[i, hh, pl.ds(j, 1), :] + pad
                         for hh in range(2)])
        return _soft(p, slot, colmax, offs, m, l, s_ref, p_ref)

    def value_map(i, p, slot, hh, alpha_c):
        vts = []
        for u in range(2):
            j, _, _ = _item(i, 2 * p + u)
            vts.append(vt_ref[0, j, hh * HEAD_DIM:(hh + 1) * HEAD_DIM, :])
        _value_map(i, slot, hh, alpha_c, vts, p_ref, acc_ref)

    def finish(i, l):
        par = i % 2
        o_t = jnp.concatenate([acc_ref[par, hh] * (1.0 / l[hh]) for hh in range(2)], axis=0)
        o_ref[0, _rows(i), :] = o_t.T.astype(BF16)

    _sweep(tab_ref, 2, score_map, soft, value_map, finish)


def _moba(q_split, km, vt, kmean, bias, offs, tab):
    B, S, width = km.shape
    pairs = width // LANES
    nt = S // ATT_TILE
    nblk = kmean.shape[1]
    seq_spec = lambda lane_block0: pl.BlockSpec((1, S, LANES), lambda b, p: (b, 0, lane_block0 + p))
    return pl.pallas_call(
        _moba_kernel,
        grid=(B, pairs),
        in_specs=[pl.BlockSpec(memory_space=pltpu.SMEM),
                  seq_spec(0), seq_spec(pairs), seq_spec(0),
                  pl.BlockSpec((1, nblk, LANES, ATT_TILE), lambda b, p: (b, 0, p, 0)),
                  pl.BlockSpec((1, nblk, LANES), lambda b, p: (b, 0, p)),
                  pl.BlockSpec((2, 2, ATT_TILE, ATT_TILE), lambda b, p: (p, 0, 0, 0)),
                  pl.BlockSpec((2, nt, ATT_TILE), lambda b, p: (p, 0, 0))],
        out_specs=seq_spec(0),
        out_shape=jax.ShapeDtypeStruct((B, S, width), BF16),
        scratch_shapes=[pltpu.VMEM((nt, 2, nblk, ATT_TILE), F32),
                        pltpu.VMEM((2, 2, 2, ATT_TILE, ATT_TILE), F32),
                        pltpu.VMEM((2, 2, 2, ATT_TILE, ATT_TILE), BF16),
                        pltpu.VMEM((2, 2, HEAD_DIM, ATT_TILE), F32)],
        compiler_params=pltpu.CompilerParams(dimension_semantics=("arbitrary", "arbitrary")),
        name="moba_attn",
    )(tab, q_split, q_split, km, vt, kmean, bias, offs)


def _diff_kernel(tab_ref, qa_ref, qb_ref, k_ref, vt_ref, bias_ref, offs_ref, lamv_ref, subw_ref, o_ref,
                 s_ref, p_ref, acc_ref, *, lambda_init):
    q_refs = (qa_ref, qb_ref)
    acc_ref[...] = jnp.zeros(acc_ref.shape, F32)

    def score_map(i, p, slot, c):
        return _score_map(i, p, slot, c, q_refs[c], k_ref, lambda d: bias_ref[0, d], s_ref)

    def soft(i, p, slot, colmax, m, l):
        offs = []
        for u in range(2):
            j, _, is_pad = _item(i, 2 * p + u)
            off = offs_ref[0, pl.ds(i - j, 1), :] + _pad_off(is_pad)
            offs.append([off, off])
        return _soft(p, slot, colmax, offs, m, l, s_ref, p_ref)

    def value_map(i, p, slot, c, alpha_c):
        vts = [vt_ref[0, _item(i, 2 * p + u)[0]] for u in range(2)]
        _value_map(i, slot, c, alpha_c, vts, p_ref, acc_ref)

    def finish(i, l):
        lv = lamv_ref[...]
        lam = (jnp.exp(jnp.sum(lv[0:1] * lv[1:2], axis=-1, keepdims=True))
               - jnp.exp(jnp.sum(lv[2:3] * lv[3:4], axis=-1, keepdims=True)) + lambda_init)
        par = i % 2
        o_t = acc_ref[par, 0] * (1.0 / l[0]) - lam * (acc_ref[par, 1] * (1.0 / l[1]))
        ms = jnp.mean(o_t * o_t, axis=0, keepdims=True)
        o_n = (o_t * lax.rsqrt(ms + NORM_EPS)).T
        o_ref[0, _rows(i), :] = (o_n * subw_ref[...] * (1.0 - lambda_init)).astype(BF16)

    _sweep(tab_ref, 2, score_map, soft, value_map, finish)


def _diff(q_split, dk, vt, bias, offs, lamv, subw, lambda_init, v_row_block0, tab):
    B, S, width = dk.shape
    heads = width // LANES
    nt = S // ATT_TILE
    nblk = vt.shape[1]
    seq_spec = lambda lane_block0: pl.BlockSpec((1, S, LANES), lambda b, h: (b, 0, lane_block0 + h))
    return pl.pallas_call(
        functools.partial(_diff_kernel, lambda_init=lambda_init),
        grid=(B, heads),
        in_specs=[pl.BlockSpec(memory_space=pltpu.SMEM),
                  seq_spec(0), seq_spec(heads), seq_spec(0),
                  pl.BlockSpec((1, nblk, LANES, ATT_TILE), lambda b, h: (b, 0, v_row_block0 + h, 0)),
                  pl.BlockSpec((1, 2, ATT_TILE, ATT_TILE), lambda b, h: (h, 0, 0, 0)),
                  pl.BlockSpec((1, nt, ATT_TILE), lambda b, h: (h, 0, 0)),
                  pl.BlockSpec(lamv.shape, lambda b, h: (0, 0)),
                  pl.BlockSpec((1, LANES), lambda b, h: (0, 0))],
        out_specs=seq_spec(0),
        out_shape=jax.ShapeDtypeStruct((B, S, width), BF16),
        scratch_shapes=[pltpu.VMEM((2, 2, 2, ATT_TILE, ATT_TILE), F32),
                        pltpu.VMEM((2, 2, 2, ATT_TILE, ATT_TILE), BF16),
                        pltpu.VMEM((2, 2, LANES, ATT_TILE), F32)],
        compiler_params=pltpu.CompilerParams(dimension_semantics=("arbitrary", "arbitrary")),
        name="diff_attn",
    )(tab, q_split, q_split, dk, vt, bias, offs, lamv, subw)


def _outproj_kernel(x_ref, m_ref, d_ref, gate_ref, w_ref, o_ref):
    width = m_ref.shape[2]
    g = gate_ref[0].astype(F32)
    mg = (m_ref[0].astype(F32) * g[:, 0:width]).astype(BF16)
    dg = (d_ref[0].astype(F32) * g[:, width:2 * width]).astype(BF16)
    o_ref[0] = x_ref[0] + _dot(mg, w_ref[0:width, :]) + _dot(dg, w_ref[width:2 * width, :])


def _outproj(x, m_att, d_att, gates, w_out):
    B, S, D = x.shape
    width = m_att.shape[2]
    rows = min(PROJ_ROWS, S)
    row_spec = lambda w: pl.BlockSpec((1, rows, w), lambda b, t: (b, t, 0))
    return pl.pallas_call(
        _outproj_kernel,
        grid=(B, S // rows),
        in_specs=[row_spec(D), row_spec(width), row_spec(width), row_spec(2 * width),
                  pl.BlockSpec(w_out.shape, lambda b, t: (0, 0))],
        out_specs=row_spec(D),
        out_shape=jax.ShapeDtypeStruct((B, S, D), F32),
        compiler_params=pltpu.CompilerParams(dimension_semantics=("arbitrary", "arbitrary")),
        name="outproj",
    )(x, m_att, d_att, gates, w_out)


def kernel(x, norm_g, w_in, moba_q_norm, moba_k_norm, diff_q_norm, diff_k_norm,
           lambda_q1, lambda_k1, lambda_q2, lambda_k2, diff_subln, w_out):
    B, S, D = x.shape
    depth = w_in.shape[0]
    width = D // 2
    moba_heads = width // HEAD_DIM
    diff_heads = width // (2 * HEAD_DIM)
    assert S % MOBA_BLOCK == 0 and width % MXU_TILE == 0 and S // MOBA_BLOCK > 1
    nt = S // ATT_TILE
    scale = HEAD_DIM ** -0.5 * LOG2E

    m_bias, m_offs = _bias_tables(moba_heads, nt)
    d_bias, d_offs = _bias_tables(diff_heads, nt)
    r = lax.broadcasted_iota(jnp.int32, (MXU_TILE, MXU_TILE), 0) // HEAD_DIM
    c = lax.broadcasted_iota(jnp.int32, (MXU_TILE, MXU_TILE), 1) // HEAD_DIM
    gsum = (r == c).astype(BF16)
    reps = width // HEAD_DIM
    tab = jnp.asarray(_pair_table(nt))

    for layer in range(depth):
        w = w_in[layer]
        cols = [w[:, k * width:(k + 1) * width] for k in range(8)]
        wa = jnp.concatenate([cols[0], cols[1], cols[3], cols[4], cols[5], cols[7]], axis=1).astype(BF16)
        wvt = jnp.concatenate([cols[2], cols[6]], axis=1).T.astype(BF16)
        tile_w = lambda v, s: (jnp.tile(v, reps) * s)[None, :].astype(F32)
        qm, km, dq, dk, gates, vt, kmean = _inproj(
            x, norm_g[layer][None, :], wa, wvt, gsum,
            tile_w(moba_q_norm[layer], scale), tile_w(moba_k_norm[layer], 1.0),
            tile_w(diff_q_norm[layer], scale), tile_w(diff_k_norm[layer], 1.0))
        m_att = _moba(qm, km, vt, kmean, m_bias, m_offs, tab)
        lambda_init = 0.8 - 0.6 * math.exp(-0.3 * layer)
        lamv = jnp.stack([lambda_q1[layer], lambda_k1[layer], lambda_q2[layer], lambda_k2[layer]])
        d_att = _diff(dq, dk, vt, d_bias, d_offs, lamv, diff_subln[layer][None, :], lambda_init,
                      width // LANES, tab)
        x = _outproj(x, m_att, d_att, gates, w_out[layer].astype(BF16))
    return x
```

```python
import functools
import math

import numpy as np
import jax
import jax.numpy as jnp
from jax import lax
from jax.experimental import pallas as pl
from jax.experimental.pallas import tpu as pltpu

HEAD_DIM = 64
MOBA_BLOCK = 256
MOBA_TOPK = 3
NORM_EPS = 1e-6
NEG_INF = -1e30

LANES = 128
MXU_TILE = 256
ATT_TILE = MOBA_BLOCK
PROJ_ROWS = 512
LOG2E = math.log2(math.e)
SWEEP_UNROLL = 4
BF16_ROWS = 16
ACC_RING = SWEEP_UNROLL

F32 = jnp.float32
BF16 = jnp.bfloat16


def _dot(a, b):
    return jnp.dot(a, b, preferred_element_type=F32)


def _dot_nt(a, b):
    return lax.dot_general(a, b, (((1,), (1,)), ((), ())), preferred_element_type=F32)


def _alibi_slopes(n_heads):
    return np.asarray(2.0 ** (-8.0 * np.arange(1, n_heads + 1) / n_heads), dtype=np.float32)


def _bias_tables(n_heads, n_tiles):
    slopes = jnp.asarray(_alibi_slopes(n_heads) * np.float32(LOG2E))
    k_loc = lax.broadcasted_iota(jnp.int32, (ATT_TILE, ATT_TILE), 0)
    q_loc = lax.broadcasted_iota(jnp.int32, (ATT_TILE, ATT_TILE), 1)
    rel = (k_loc - q_loc).astype(F32)
    off = slopes[:, None, None] * rel[None]
    diag = jnp.where((k_loc <= q_loc)[None], off, NEG_INF)
    tiles = jnp.stack([off, diag], axis=1)
    d = jnp.arange(n_tiles, dtype=F32)
    offs = -slopes[:, None] * (ATT_TILE * d)[None, :]
    offs = jnp.broadcast_to(offs[:, :, None], (n_heads, n_tiles, ATT_TILE))
    return tiles, offs


def _inproj_kernel(x_ref, g_ref, wa_ref, wvt_ref, wgt_ref, gsum_ref, wqm_ref, wkm_ref, wqd_ref, wkd_ref,
                   qm_ref, km_ref, dq_ref, dk_ref, gt_ref, vt_ref, kmean_ref, *, width):
    t = pl.program_id(1)
    rows = x_ref.shape[1]
    blocks = rows // MOBA_BLOCK
    x = x_ref[0]
    ms = jnp.mean(x * x, axis=-1, keepdims=True)
    h = (x * lax.rsqrt(ms + NORM_EPS) * g_ref[...]).astype(BF16)

    gsum = gsum_ref[...]
    low_half = lax.broadcasted_iota(jnp.int32, (rows, width), 1) % LANES < HEAD_DIM

    def head_norm(y, w):
        sq = (y * y).astype(BF16)
        parts = [_dot(sq[:, c * MXU_TILE:(c + 1) * MXU_TILE], gsum) for c in range(width // MXU_TILE)]
        ssq = jnp.concatenate(parts, axis=1)
        return y * lax.rsqrt(ssq * (1.0 / HEAD_DIM) + NORM_EPS) * w

    def store_split(ref, q):
        ref[0, :, 0:width] = jnp.where(low_half, q, 0.0).astype(BF16)
        ref[0, :, width:2 * width] = jnp.where(low_half, 0.0, q).astype(BF16)

    def silu(y):
        return y * (1.0 / (1.0 + jnp.exp(-y)))

    y = _dot(h, wa_ref[:, 0 * width:1 * width])
    store_split(qm_ref, head_norm(y, wqm_ref[...]))

    y = _dot(h, wa_ref[:, 1 * width:2 * width])
    kn = head_norm(y, wkm_ref[...])
    km_ref[0] = kn.astype(BF16)
    for s in range(blocks):
        kmean_ref[0, pl.ds(t * blocks + s, 1), :] = jnp.mean(
            kn[s * MOBA_BLOCK:(s + 1) * MOBA_BLOCK], axis=0, keepdims=True)

    y = _dot(h, wa_ref[:, 2 * width:3 * width])
    store_split(dq_ref, head_norm(y, wqd_ref[...]))

    y = _dot(h, wa_ref[:, 3 * width:4 * width])
    dk_ref[0] = head_norm(y, wkd_ref[...]).astype(BF16)

    vt = _dot_nt(wvt_ref[...], h)
    gt = silu(_dot_nt(wgt_ref[...], h))
    for s in range(blocks):
        vt_ref[0, s] = vt[:, s * MOBA_BLOCK:(s + 1) * MOBA_BLOCK].astype(BF16)
        gt_ref[0, s] = gt[:, s * MOBA_BLOCK:(s + 1) * MOBA_BLOCK].astype(BF16)


def _inproj(x, g, wa, wvt, wgt, gsum, wqm, wkm, wqd, wkd):
    B, S, D = x.shape
    width = wa.shape[1] // 4
    rows = min(PROJ_ROWS, S)
    nblk = S // MOBA_BLOCK
    blocks = rows // MOBA_BLOCK
    full = lambda shape: pl.BlockSpec(shape, lambda b, t: (0,) * len(shape))
    row_spec = lambda w: pl.BlockSpec((1, rows, w), lambda b, t: (b, t, 0))
    act = lambda w: jax.ShapeDtypeStruct((B, S, w), BF16)
    t_spec = pl.BlockSpec((1, blocks, 2 * width, MOBA_BLOCK), lambda b, t: (b, t, 0, 0))
    t_shape = jax.ShapeDtypeStruct((B, nblk, 2 * width, MOBA_BLOCK), BF16)
    return pl.pallas_call(
        functools.partial(_inproj_kernel, width=width),
        grid=(B, S // rows),
        in_specs=[row_spec(D), full((1, D)), full(wa.shape), full(wvt.shape), full(wgt.shape),
                  full(gsum.shape),
                  full((1, width)), full((1, width)), full((1, width)), full((1, width))],
        out_specs=[row_spec(2 * width), row_spec(width), row_spec(2 * width), row_spec(width),
                   t_spec, t_spec,
                   pl.BlockSpec((1, nblk, width), lambda b, t: (b, 0, 0))],
        out_shape=[act(2 * width), act(width), act(2 * width), act(width), t_shape, t_shape,
                   jax.ShapeDtypeStruct((B, nblk, width), F32)],
        compiler_params=pltpu.CompilerParams(dimension_semantics=("arbitrary", "arbitrary")),
        name="inproj",
    )(x, g, wa, wvt, wgt, gsum, wqm, wkm, wqd, wkd)


def _pair_table(n_tiles):
    rows = [(i, p, int(p == i // 2)) for i in range(n_tiles) for p in range(i // 2 + 1)]
    rows += [(rows[-1][0], rows[-1][1], 0)] * 2
    return np.ascontiguousarray(np.asarray(rows, dtype=np.int32).T)


def _item(i, t):
    t = jnp.asarray(t, jnp.int32)
    is_diag = t == 0
    j = jnp.where(is_diag, i, t - 1)
    is_pad = jnp.logical_and(t > 0, t - 1 >= i)
    return j, is_diag, is_pad


def _rows(i):
    return pl.ds(pl.multiple_of(i * ATT_TILE, ATT_TILE), ATT_TILE)


def _pad_off(is_pad):
    return jnp.where(is_pad, jnp.float32(NEG_INF), jnp.float32(0.0))


def _sweep(tab_ref, n_maps, score_map, soft, value_map, finish):
    n_pairs = tab_ref.shape[1] - 2
    assert n_pairs % SWEEP_UNROLL == 0 and SWEEP_UNROLL % 2 == 0
    pair = lambda g: (tab_ref[0, g], tab_ref[1, g])

    def score_all(g, slot):
        return tuple(x for c in range(n_maps) for x in score_map(*pair(g), slot, c))

    def half_step(g, slot, carry):
        colmax, m, alpha = carry
        colmax1 = []
        for c in range(n_maps):
            colmax1 += score_map(*pair(g), slot, c)
            value_map(*pair(g - 2), slot, c, alpha[c])
        m1, alpha1 = soft(*pair(g - 1), 1 - slot, colmax, m)
        return tuple(colmax1), m1, alpha1

    colmax = score_all(0, 0)
    m_init = tuple(jnp.full((1, ATT_TILE), NEG_INF, F32) for _ in range(n_maps))
    m, alpha = soft(*pair(0), 0, colmax, m_init)
    carry = (score_all(1, 1), m, alpha)

    def step(h, carry):
        g = 2 + SWEEP_UNROLL * h
        for k in range(SWEEP_UNROLL):
            carry = half_step(g + k, k % 2, carry)
        for k in range(SWEEP_UNROLL):
            @pl.when(tab_ref[2, g + k - 2] == 1)
            def _(k=k):
                finish(tab_ref[0, g + k - 2])
        return carry

    lax.fori_loop(0, n_pairs // SWEEP_UNROLL, step, carry)


def _score_map(i, p, slot, c, q_ref, k_ref, bias_of, s_ref):
    q = q_ref[0, _rows(i), :]
    items = [_item(i, 2 * p + u) for u in range(2)]
    kt = jnp.concatenate([k_ref[0, _rows(j), :] for j, _, _ in items], axis=0)
    s2 = _dot_nt(kt, q)
    colmax = []
    for u, (_, is_diag, _) in enumerate(items):
        s = s2[u * ATT_TILE:(u + 1) * ATT_TILE] + bias_of(is_diag.astype(jnp.int32))
        s_ref[slot, u, c] = s
        colmax.append(jnp.max(s, axis=0, keepdims=True))
    return colmax


def _soft(p, slot, colmax, offs, m, s_ref, p_ref):
    m_out, alpha_out = [], []
    for c in range(len(m)):
        m_old = jnp.where(p == 0, jnp.float32(NEG_INF), m[c])
        m_new = jnp.maximum(m_old, jnp.maximum(colmax[2 * c] + offs[0][c], colmax[2 * c + 1] + offs[1][c]))
        for u in range(2):
            pu = jnp.exp2(s_ref[slot, u, c] - (m_new - offs[u][c]))
            p_ref[slot, c, u * ATT_TILE:(u + 1) * ATT_TILE, :] = pu.astype(BF16)
        m_out.append(m_new)
        alpha_out.append(jnp.exp2(m_old - m_new))
    return tuple(m_out), tuple(alpha_out)


def _value_map(i, slot, c, alpha_c, vts, p_ref, acc_ref):
    vt = jnp.concatenate(vts, axis=1)
    vt1 = jnp.concatenate([vt, jnp.ones((BF16_ROWS, vt.shape[1]), BF16)], axis=0)
    par = i % ACC_RING
    acc_ref[par, c] = alpha_c * acc_ref[par, c] + _dot(vt1, p_ref[slot, c])


def _moba_kernel(tab_ref, qa_ref, qb_ref, k_ref, vt_ref, gt_ref, kmean_ref, bias_ref, offs_ref, o_ref,
                 sel_ref, s_ref, p_ref, acc_ref):
    nblk = kmean_ref.shape[1]
    n_tiles = k_ref.shape[1] // ATT_TILE
    q_refs = (qa_ref, qb_ref)
    acc_ref[...] = jnp.zeros(acc_ref.shape, F32)

    km = kmean_ref[0]
    km_hi = km.astype(BF16)
    km_lo = (km - km_hi.astype(F32)).astype(BF16)
    blk = lax.broadcasted_iota(jnp.int32, (nblk, ATT_TILE), 0)
    for i in range(n_tiles):
        valid = blk < i
        for hh in range(2):
            q = q_refs[hh][0, i * ATT_TILE:(i + 1) * ATT_TILE, :]
            g = _dot_nt(km_hi, q) + _dot_nt(km_lo, q)
            g = jnp.where(valid, g, -jnp.inf)
            sel = blk == i
            for _ in range(min(MOBA_TOPK, nblk - 1)):
                mx = jnp.max(g, axis=0, keepdims=True)
                first = jnp.min(jnp.where(g == mx, blk, nblk), axis=0, keepdims=True)
                pick = blk == first
                sel = jnp.logical_or(sel, jnp.logical_and(pick, valid))
                g = jnp.where(pick, -jnp.inf, g)
            sel_ref[i, hh] = jnp.where(sel, 0.0, NEG_INF).astype(F32)

    def score_map(i, p, slot, hh):
        return _score_map(i, p, slot, hh, q_refs[hh], k_ref, lambda d: bias_ref[hh, d], s_ref)

    def soft(i, p, slot, colmax, m):
        offs = []
        for u in range(2):
            j, _, is_pad = _item(i, 2 * p + u)
            pad = _pad_off(is_pad)
            offs.append([offs_ref[hh, pl.ds(i - j, 1), :] + sel_ref[i, hh, pl.ds(j, 1), :] + pad
                         for hh in range(2)])
        return _soft(p, slot, colmax, offs, m, s_ref, p_ref)

    def value_map(i, p, slot, hh, alpha_c):
        vts = []
        for u in range(2):
            j, _, _ = _item(i, 2 * p + u)
            vts.append(vt_ref[0, j, hh * HEAD_DIM:(hh + 1) * HEAD_DIM, :])
        _value_map(i, slot, hh, alpha_c, vts, p_ref, acc_ref)

    def finish(i):
        par = i % ACC_RING
        o_t = jnp.concatenate([acc_ref[par, hh, 0:HEAD_DIM] * (1.0 / acc_ref[par, hh, HEAD_DIM:HEAD_DIM + 1])
                               for hh in range(2)], axis=0)
        o_ref[0, i] = (o_t * gt_ref[0, i].astype(F32)).astype(BF16)

    _sweep(tab_ref, 2, score_map, soft, value_map, finish)


def _moba(q_split, km, vt, gt, kmean, bias, offs, tab):
    B, S, width = km.shape
    pairs = width // LANES
    nt = S // ATT_TILE
    nblk = kmean.shape[1]
    seq_spec = lambda lane_block0: pl.BlockSpec((1, S, LANES), lambda b, p: (b, 0, lane_block0 + p))
    t_spec = pl.BlockSpec((1, nblk, LANES, ATT_TILE), lambda b, p: (b, 0, p, 0))
    return pl.pallas_call(
        _moba_kernel,
        grid=(B, pairs),
        in_specs=[pl.BlockSpec(memory_space=pltpu.SMEM),
                  seq_spec(0), seq_spec(pairs), seq_spec(0), t_spec, t_spec,
                  pl.BlockSpec((1, nblk, LANES), lambda b, p: (b, 0, p)),
                  pl.BlockSpec((2, 2, ATT_TILE, ATT_TILE), lambda b, p: (p, 0, 0, 0)),
                  pl.BlockSpec((2, nt, ATT_TILE), lambda b, p: (p, 0, 0))],
        out_specs=t_spec,
        out_shape=jax.ShapeDtypeStruct((B, nt, width, ATT_TILE), BF16),
        scratch_shapes=[pltpu.VMEM((nt, 2, nblk, ATT_TILE), F32),
                        pltpu.VMEM((2, 2, 2, ATT_TILE, ATT_TILE), F32),
                        pltpu.VMEM((2, 2, 2 * ATT_TILE, ATT_TILE), BF16),
                        pltpu.VMEM((ACC_RING, 2, HEAD_DIM + BF16_ROWS, ATT_TILE), F32)],
        compiler_params=pltpu.CompilerParams(dimension_semantics=("arbitrary", "arbitrary")),
        name="moba_attn",
    )(tab, q_split, q_split, km, vt, gt, kmean, bias, offs)


def _diff_kernel(tab_ref, qa_ref, qb_ref, k_ref, vt_ref, gt_ref, bias_ref, offs_ref, lamv_ref, subw_ref,
                 o_ref, s_ref, p_ref, acc_ref, *, lambda_init):
    q_refs = (qa_ref, qb_ref)
    acc_ref[...] = jnp.zeros(acc_ref.shape, F32)
    lv = lamv_ref[...]
    lam = (jnp.exp(jnp.sum(lv[0:1] * lv[1:2], axis=-1, keepdims=True))
           - jnp.exp(jnp.sum(lv[2:3] * lv[3:4], axis=-1, keepdims=True)) + lambda_init)

    def score_map(i, p, slot, c):
        return _score_map(i, p, slot, c, q_refs[c], k_ref, lambda d: bias_ref[0, d], s_ref)

    def soft(i, p, slot, colmax, m):
        offs = []
        for u in range(2):
            j, _, is_pad = _item(i, 2 * p + u)
            off = offs_ref[0, pl.ds(i - j, 1), :] + _pad_off(is_pad)
            offs.append([off, off])
        return _soft(p, slot, colmax, offs, m, s_ref, p_ref)

    def value_map(i, p, slot, c, alpha_c):
        vts = [vt_ref[0, _item(i, 2 * p + u)[0]] for u in range(2)]
        _value_map(i, slot, c, alpha_c, vts, p_ref, acc_ref)

    def finish(i):
        par = i % ACC_RING
        o = [acc_ref[par, c, 0:LANES] * (1.0 / acc_ref[par, c, LANES:LANES + 1]) for c in range(2)]
        o_t = o[0] - lam * o[1]
        ms = jnp.mean(o_t * o_t, axis=0, keepdims=True)
        o_n = o_t * lax.rsqrt(ms + NORM_EPS) * subw_ref[...] * (1.0 - lambda_init)
        o_ref[0, i] = (o_n * gt_ref[0, i].astype(F32)).astype(BF16)

    _sweep(tab_ref, 2, score_map, soft, value_map, finish)


def _diff(q_split, dk, vt, gt, bias, offs, lamv, subw, lambda_init, v_row_block0, tab):
    B, S, width = dk.shape
    heads = width // LANES
    nt = S // ATT_TILE
    nblk = vt.shape[1]
    seq_spec = lambda lane_block0: pl.BlockSpec((1, S, LANES), lambda b, h: (b, 0, lane_block0 + h))
    t_spec = lambda row_block0: pl.BlockSpec((1, nblk, LANES, ATT_TILE), lambda b, h: (b, 0, row_block0 + h, 0))
    return pl.pallas_call(
        functools.partial(_diff_kernel, lambda_init=lambda_init),
        grid=(B, heads),
        in_specs=[pl.BlockSpec(memory_space=pltpu.SMEM),
                  seq_spec(0), seq_spec(heads), seq_spec(0), t_spec(v_row_block0), t_spec(v_row_block0),
                  pl.BlockSpec((1, 2, ATT_TILE, ATT_TILE), lambda b, h: (h, 0, 0, 0)),
                  pl.BlockSpec((1, nt, ATT_TILE), lambda b, h: (h, 0, 0)),
                  pl.BlockSpec(lamv.shape, lambda b, h: (0, 0)),
                  pl.BlockSpec((LANES, ATT_TILE), lambda b, h: (0, 0))],
        out_specs=t_spec(0),
        out_shape=jax.ShapeDtypeStruct((B, nt, width, ATT_TILE), BF16),
        scratch_shapes=[pltpu.VMEM((2, 2, 2, ATT_TILE, ATT_TILE), F32),
                        pltpu.VMEM((2, 2, 2 * ATT_TILE, ATT_TILE), BF16),
                        pltpu.VMEM((ACC_RING, 2, LANES + BF16_ROWS, ATT_TILE), F32)],
        compiler_params=pltpu.CompilerParams(dimension_semantics=("arbitrary", "arbitrary")),
        name="diff_attn",
    )(tab, q_split, q_split, dk, vt, gt, bias, offs, lamv, subw)


def _outproj_kernel(x_ref, m_ref, d_ref, w_ref, o_ref):
    for t in range(m_ref.shape[1]):
        mixed_t = jnp.concatenate([m_ref[0, t], d_ref[0, t]], axis=0)
        y = lax.dot_general(mixed_t, w_ref[...], (((0,), (0,)), ((), ())), preferred_element_type=F32)
        rows = pl.ds(t * ATT_TILE, ATT_TILE)
        o_ref[0, rows, :] = x_ref[0, rows, :] + y


def _outproj(x, m_att, d_att, w_out):
    B, S, D = x.shape
    width = m_att.shape[2]
    rows = min(PROJ_ROWS, S)
    tiles = rows // ATT_TILE
    row_spec = pl.BlockSpec((1, rows, D), lambda b, t: (b, t, 0))
    t_spec = pl.BlockSpec((1, tiles, width, ATT_TILE), lambda b, t: (b, t, 0, 0))
    return pl.pallas_call(
        _outproj_kernel,
        grid=(B, S // rows),
        in_specs=[row_spec, t_spec, t_spec, pl.BlockSpec(w_out.shape, lambda b, t: (0, 0))],
        out_specs=row_spec,
        out_shape=jax.ShapeDtypeStruct((B, S, D), F32),
        compiler_params=pltpu.CompilerParams(dimension_semantics=("arbitrary", "arbitrary")),
        name="outproj",
    )(x, m_att, d_att, w_out)


def kernel(x, norm_g, w_in, moba_q_norm, moba_k_norm, diff_q_norm, diff_k_norm,
           lambda_q1, lambda_k1, lambda_q2, lambda_k2, diff_subln, w_out):
    B, S, D = x.shape
    depth = w_in.shape[0]
    width = D // 2
    moba_heads = width // HEAD_DIM
    diff_heads = width // (2 * HEAD_DIM)
    assert S % MOBA_BLOCK == 0 and width % MXU_TILE == 0 and S // MOBA_BLOCK > 1
    nt = S // ATT_TILE
    scale = HEAD_DIM ** -0.5 * LOG2E

    m_bias, m_offs = _bias_tables(moba_heads, nt)
    d_bias, d_offs = _bias_tables(diff_heads, nt)
    r = lax.broadcasted_iota(jnp.int32, (MXU_TILE, MXU_TILE), 0) // HEAD_DIM
    c = lax.broadcasted_iota(jnp.int32, (MXU_TILE, MXU_TILE), 1) // HEAD_DIM
    gsum = (r == c).astype(BF16)
    reps = width // HEAD_DIM
    tab = jnp.asarray(_pair_table(nt))

    for layer in range(depth):
        w = w_in[layer]
        cols = [w[:, k * width:(k + 1) * width] for k in range(8)]
        wa = jnp.concatenate([cols[0], cols[1], cols[4], cols[5]], axis=1).astype(BF16)
        wvt = jnp.concatenate([cols[2], cols[6]], axis=1).T.astype(BF16)
        wgt = jnp.concatenate([cols[3], cols[7]], axis=1).T.astype(BF16)
        tile_w = lambda v, s: (jnp.tile(v, reps) * s)[None, :].astype(F32)
        qm, km, dq, dk, gt, vt, kmean = _inproj(
            x, norm_g[layer][None, :], wa, wvt, wgt, gsum,
            tile_w(moba_q_norm[layer], scale), tile_w(moba_k_norm[layer], 1.0),
            tile_w(diff_q_norm[layer], scale), tile_w(diff_k_norm[layer], 1.0))
        m_att = _moba(qm, km, vt, gt, kmean, m_bias, m_offs, tab)
        lambda_init = 0.8 - 0.6 * math.exp(-0.3 * layer)
        lamv = jnp.stack([lambda_q1[layer], lambda_k1[layer], lambda_q2[layer], lambda_k2[layer]])
        subw = jnp.broadcast_to(diff_subln[layer][:, None], (2 * HEAD_DIM, ATT_TILE)).astype(F32)
        d_att = _diff(dq, dk, vt, gt, d_bias, d_offs, lamv, subw, lambda_init, width // LANES, tab)
        x = _outproj(x, m_att, d_att, w_out[layer].astype(BF16))
    return x
```

```python
import functools
import math

import numpy as np
import jax
import jax.numpy as jnp
from jax import lax
from jax.experimental import pallas as pl
from jax.experimental.pallas import tpu as pltpu

HEAD_DIM = 64
MOBA_BLOCK = 256
MOBA_TOPK = 3
NORM_EPS = 1e-6
NEG_INF = -1e30

LANES = 128
MXU_TILE = 256
ATT_TILE = MOBA_BLOCK
PROJ_ROWS = 512
LOG2E = math.log2(math.e)
BF16_ROWS = 16
SWEEP_UNROLLS = (8, 4, 2)
MAX_RING_SLOTS = 4

F32 = jnp.float32
BF16 = jnp.bfloat16


def _dot(a, b):
    return jnp.dot(a, b, preferred_element_type=F32)


def _dot_nt(a, b):
    return lax.dot_general(a, b, (((1,), (1,)), ((), ())), preferred_element_type=F32)


def _alibi_slopes(n_heads):
    return np.asarray(2.0 ** (-8.0 * np.arange(1, n_heads + 1) / n_heads), dtype=np.float32)


def _bias_tables(n_heads, n_tiles):
    slopes = _alibi_slopes(n_heads) * np.float32(LOG2E)
    k_loc = np.arange(ATT_TILE, dtype=np.int32)[:, None]
    q_loc = np.arange(ATT_TILE, dtype=np.int32)[None, :]
    rel = (k_loc - q_loc).astype(np.float32)
    off = slopes[:, None, None] * rel[None]
    diag = np.where((k_loc <= q_loc)[None], off, np.float32(NEG_INF))
    tiles = np.stack([off, diag], axis=1).astype(np.float32)
    d = np.arange(n_tiles, dtype=np.float32)
    offs = -slopes[:, None] * (ATT_TILE * d)[None, :]
    offs = np.broadcast_to(offs[:, :, None], (n_heads, n_tiles, ATT_TILE)).astype(np.float32)
    return jnp.asarray(tiles), jnp.asarray(offs)


def _inproj_kernel(x_ref, g_ref, wa_ref, wvt_ref, wgt_ref, gsum_ref, wqm_ref, wkm_ref, wqd_ref, wkd_ref,
                   qm_ref, km_ref, dq_ref, dk_ref, gt_ref, vt_ref, kmean_ref, *, width):
    t = pl.program_id(1)
    rows = x_ref.shape[1]
    blocks = rows // MOBA_BLOCK
    x = x_ref[0]
    ms = jnp.mean(x * x, axis=-1, keepdims=True)
    h = (x * lax.rsqrt(ms + NORM_EPS) * g_ref[...]).astype(BF16)

    gsum = gsum_ref[...]
    low_half = lax.broadcasted_iota(jnp.int32, (rows, width), 1) % LANES < HEAD_DIM

    def head_norm(y, w):
        sq = (y * y).astype(BF16)
        parts = [_dot(sq[:, c * MXU_TILE:(c + 1) * MXU_TILE], gsum) for c in range(width // MXU_TILE)]
        ssq = jnp.concatenate(parts, axis=1)
        return y * lax.rsqrt(ssq * (1.0 / HEAD_DIM) + NORM_EPS) * w

    def store_split(ref, q):
        ref[0, :, 0:width] = jnp.where(low_half, q, 0.0).astype(BF16)
        ref[0, :, width:2 * width] = jnp.where(low_half, 0.0, q).astype(BF16)

    def silu(y):
        return y * (1.0 / (1.0 + jnp.exp(-y)))

    y = _dot(h, wa_ref[:, 0 * width:1 * width])
    store_split(qm_ref, head_norm(y, wqm_ref[...]))

    y = _dot(h, wa_ref[:, 1 * width:2 * width])
    kn = head_norm(y, wkm_ref[...])
    km_ref[0] = kn.astype(BF16)
    for s in range(blocks):
        kmean_ref[0, pl.ds(t * blocks + s, 1), :] = jnp.mean(
            kn[s * MOBA_BLOCK:(s + 1) * MOBA_BLOCK], axis=0, keepdims=True)

    y = _dot(h, wa_ref[:, 2 * width:3 * width])
    store_split(dq_ref, head_norm(y, wqd_ref[...]))

    y = _dot(h, wa_ref[:, 3 * width:4 * width])
    dk_ref[0] = head_norm(y, wkd_ref[...]).astype(BF16)

    vt = _dot_nt(wvt_ref[...], h)
    gt = silu(_dot_nt(wgt_ref[...], h))
    for s in range(blocks):
        vt_ref[0, s] = vt[:, s * MOBA_BLOCK:(s + 1) * MOBA_BLOCK].astype(BF16)
        gt_ref[0, s] = gt[:, s * MOBA_BLOCK:(s + 1) * MOBA_BLOCK].astype(BF16)


def _inproj(x, g, wa, wvt, wgt, gsum, wqm, wkm, wqd, wkd):
    B, S, D = x.shape
    width = wa.shape[1] // 4
    rows = min(PROJ_ROWS, S)
    nblk = S // MOBA_BLOCK
    blocks = rows // MOBA_BLOCK
    full = lambda shape: pl.BlockSpec(shape, lambda b, t: (0,) * len(shape))
    row_spec = lambda w: pl.BlockSpec((1, rows, w), lambda b, t: (b, t, 0))
    act = lambda w: jax.ShapeDtypeStruct((B, S, w), BF16)
    t_spec = pl.BlockSpec((1, blocks, 2 * width, MOBA_BLOCK), lambda b, t: (b, t, 0, 0))
    t_shape = jax.ShapeDtypeStruct((B, nblk, 2 * width, MOBA_BLOCK), BF16)
    return pl.pallas_call(
        functools.partial(_inproj_kernel, width=width),
        grid=(B, S // rows),
        in_specs=[row_spec(D), full((1, D)), full(wa.shape), full(wvt.shape), full(wgt.shape),
                  full(gsum.shape),
                  full((1, width)), full((1, width)), full((1, width)), full((1, width))],
        out_specs=[row_spec(2 * width), row_spec(width), row_spec(2 * width), row_spec(width),
                   t_spec, t_spec,
                   pl.BlockSpec((1, nblk, width), lambda b, t: (b, 0, 0))],
        out_shape=[act(2 * width), act(width), act(2 * width), act(width), t_shape, t_shape,
                   jax.ShapeDtypeStruct((B, nblk, width), F32)],
        compiler_params=pltpu.CompilerParams(dimension_semantics=("arbitrary", "arbitrary")),
        name="inproj",
    )(x, g, wa, wvt, wgt, gsum, wqm, wkm, wqd, wkd)


def _pair_table(n_tiles):
    rows = [(i, p, int(p == i // 2)) for i in range(n_tiles) for p in range(i // 2 + 1)]
    rows += [(rows[-1][0], rows[-1][1], 0)] * 2
    return np.ascontiguousarray(np.asarray(rows, dtype=np.int32).T)


def _item(i, t):
    t = jnp.asarray(t, jnp.int32)
    is_diag = t == 0
    j = jnp.where(is_diag, i, t - 1)
    is_pad = jnp.logical_and(t > 0, t - 1 >= i)
    return j, is_diag, is_pad


def _rows(i):
    return pl.ds(pl.multiple_of(i * ATT_TILE, ATT_TILE), ATT_TILE)


def _pad_off(is_pad):
    return jnp.where(is_pad, jnp.float32(NEG_INF), jnp.float32(0.0))


def _sweep_unroll(n_tiles):
    n_pairs = _pair_table(n_tiles).shape[1] - 2
    return next(u for u in SWEEP_UNROLLS if n_pairs % u == 0)


def _ring_slots(unroll):
    return min(unroll, MAX_RING_SLOTS)


def _sweep(tab_ref, n_maps, unroll, score_map, soft, value_map, finish):
    n_pairs = tab_ref.shape[1] - 2
    ring = _ring_slots(unroll)
    assert n_pairs % unroll == 0 and unroll % ring == 0
    pair = lambda g: (tab_ref[0, g], tab_ref[1, g])

    def score_all(g, slot):
        return tuple(x for c in range(n_maps) for x in score_map(*pair(g), slot, c))

    def half_step(g, k, carry):
        colmax, m, alpha = carry
        colmax1 = []
        for c in range(n_maps):
            colmax1 += score_map(*pair(g), k, c)
            value_map(*pair(g - 2), (k - 2) % ring, c, alpha[c])
        m1, alpha1 = soft(*pair(g - 1), (k - 1) % ring, colmax, m)
        return tuple(colmax1), m1, alpha1

    colmax = score_all(0, 0)
    m_init = tuple(jnp.full((1, ATT_TILE), NEG_INF, F32) for _ in range(n_maps))
    m, alpha = soft(*pair(0), 0, colmax, m_init)
    carry = (score_all(1, 1), m, alpha)

    def step(h, carry):
        g = 2 + unroll * h
        for k in range(unroll):
            carry = half_step(g + k, (2 + k) % ring, carry)
        for k in range(unroll):
            @pl.when(tab_ref[2, g + k - 2] == 1)
            def _(k=k):
                finish(tab_ref[0, g + k - 2])
        return carry

    lax.fori_loop(0, n_pairs // unroll, step, carry)


def _score_map(i, p, slot, c, q_ref, k_ref, bias_of, s_ref):
    q = q_ref[0, _rows(i), :]
    items = [_item(i, 2 * p + u) for u in range(2)]
    kt = jnp.concatenate([k_ref[0, _rows(j), :] for j, _, _ in items], axis=0)
    s2 = _dot_nt(kt, q)
    colmax = []
    for u, (_, is_diag, _) in enumerate(items):
        s = s2[u * ATT_TILE:(u + 1) * ATT_TILE] + bias_of(is_diag.astype(jnp.int32))
        s_ref[slot, u, c] = s
        colmax.append(jnp.max(s, axis=0, keepdims=True))
    return colmax


def _soft(p, slot, colmax, offs, m, s_ref, p_ref):
    m_out, alpha_out = [], []
    for c in range(len(m)):
        m_old = jnp.where(p == 0, jnp.float32(NEG_INF), m[c])
        m_new = jnp.maximum(m_old, jnp.maximum(colmax[2 * c] + offs[0][c], colmax[2 * c + 1] + offs[1][c]))
        for u in range(2):
            pu = jnp.exp2(s_ref[slot, u, c] - (m_new - offs[u][c]))
            p_ref[slot, c, u * ATT_TILE:(u + 1) * ATT_TILE, :] = pu.astype(BF16)
        m_out.append(m_new)
        alpha_out.append(jnp.exp2(m_old - m_new))
    return tuple(m_out), tuple(alpha_out)


def _value_map(i, slot, c, alpha_c, vts, p_ref, acc_ref):
    vt = jnp.concatenate(vts, axis=1)
    vt1 = jnp.concatenate([vt, jnp.ones((BF16_ROWS, vt.shape[1]), BF16)], axis=0)
    par = i % acc_ref.shape[0]
    acc_ref[par, c] = alpha_c * acc_ref[par, c] + _dot(vt1, p_ref[slot, c])


def _moba_kernel(tab_ref, qa_ref, qb_ref, k_ref, vt_ref, gt_ref, kmean_ref, bias_ref, offs_ref, o_ref,
                 sel_ref, s_ref, p_ref, acc_ref):
    nblk = kmean_ref.shape[1]
    n_tiles = k_ref.shape[1] // ATT_TILE
    q_refs = (qa_ref, qb_ref)
    acc_ref[...] = jnp.zeros(acc_ref.shape, F32)

    km = kmean_ref[0]
    km_hi = km.astype(BF16)
    km_lo = (km - km_hi.astype(F32)).astype(BF16)
    blk = lax.broadcasted_iota(jnp.int32, (nblk, ATT_TILE), 0)
    for i in range(n_tiles):
        valid = blk < i
        for hh in range(2):
            q = q_refs[hh][0, i * ATT_TILE:(i + 1) * ATT_TILE, :]
            g = _dot_nt(km_hi, q) + _dot_nt(km_lo, q)
            g = jnp.where(valid, g, -jnp.inf)
            sel = blk == i
            for _ in range(min(MOBA_TOPK, nblk - 1)):
                mx = jnp.max(g, axis=0, keepdims=True)
                first = jnp.min(jnp.where(g == mx, blk, nblk), axis=0, keepdims=True)
                pick = blk == first
                sel = jnp.logical_or(sel, jnp.logical_and(pick, valid))
                g = jnp.where(pick, -jnp.inf, g)
            sel_ref[i, hh] = jnp.where(sel, 0.0, NEG_INF).astype(F32)

    def score_map(i, p, slot, hh):
        return _score_map(i, p, slot, hh, q_refs[hh], k_ref, lambda d: bias_ref[hh, d], s_ref)

    def soft(i, p, slot, colmax, m):
        offs = []
        for u in range(2):
            j, _, is_pad = _item(i, 2 * p + u)
            pad = _pad_off(is_pad)
            offs.append([offs_ref[hh, pl.ds(i - j, 1), :] + sel_ref[i, hh, pl.ds(j, 1), :] + pad
                         for hh in range(2)])
        return _soft(p, slot, colmax, offs, m, s_ref, p_ref)

    def value_map(i, p, slot, hh, alpha_c):
        vts = []
        for u in range(2):
            j, _, _ = _item(i, 2 * p + u)
            vts.append(vt_ref[0, j, hh * HEAD_DIM:(hh + 1) * HEAD_DIM, :])
        _value_map(i, slot, hh, alpha_c, vts, p_ref, acc_ref)

    def finish(i):
        par = i % acc_ref.shape[0]
        o_t = jnp.concatenate([acc_ref[par, hh, 0:HEAD_DIM] * (1.0 / acc_ref[par, hh, HEAD_DIM:HEAD_DIM + 1])
                               for hh in range(2)], axis=0)
        o_ref[0, i] = (o_t * gt_ref[0, i].astype(F32)).astype(BF16)

    _sweep(tab_ref, 2, acc_ref.shape[0], score_map, soft, value_map, finish)


def _moba(q_split, km, vt, gt, kmean, bias, offs, tab):
    B, S, width = km.shape
    pairs = width // LANES
    nt = S // ATT_TILE
    nblk = kmean.shape[1]
    unroll = _sweep_unroll(nt)
    ring = _ring_slots(unroll)
    seq_spec = lambda lane_block0: pl.BlockSpec((1, S, LANES), lambda b, p: (b, 0, lane_block0 + p))
    t_spec = pl.BlockSpec((1, nblk, LANES, ATT_TILE), lambda b, p: (b, 0, p, 0))
    return pl.pallas_call(
        _moba_kernel,
        grid=(B, pairs),
        in_specs=[pl.BlockSpec(memory_space=pltpu.SMEM),
                  seq_spec(0), seq_spec(pairs), seq_spec(0), t_spec, t_spec,
                  pl.BlockSpec((1, nblk, LANES), lambda b, p: (b, 0, p)),
                  pl.BlockSpec((2, 2, ATT_TILE, ATT_TILE), lambda b, p: (p, 0, 0, 0)),
                  pl.BlockSpec((2, nt, ATT_TILE), lambda b, p: (p, 0, 0))],
        out_specs=t_spec,
        out_shape=jax.ShapeDtypeStruct((B, nt, width, ATT_TILE), BF16),
        scratch_shapes=[pltpu.VMEM((nt, 2, nblk, ATT_TILE), F32),
                        pltpu.VMEM((ring, 2, 2, ATT_TILE, ATT_TILE), F32),
                        pltpu.VMEM((ring, 2, 2 * ATT_TILE, ATT_TILE), BF16),
                        pltpu.VMEM((unroll, 2, HEAD_DIM + BF16_ROWS, ATT_TILE), F32)],
        compiler_params=pltpu.CompilerParams(dimension_semantics=("arbitrary", "arbitrary")),
        name="moba_attn",
    )(tab, q_split, q_split, km, vt, gt, kmean, bias, offs)


def _diff_kernel(tab_ref, qa_ref, qb_ref, k_ref, vt_ref, gt_ref, bias_ref, offs_ref, lamv_ref, subw_ref,
                 o_ref, s_ref, p_ref, acc_ref, *, lambda_init):
    q_refs = (qa_ref, qb_ref)
    acc_ref[...] = jnp.zeros(acc_ref.shape, F32)
    lv = lamv_ref[...]
    lam = (jnp.exp(jnp.sum(lv[0:1] * lv[1:2], axis=-1, keepdims=True))
           - jnp.exp(jnp.sum(lv[2:3] * lv[3:4], axis=-1, keepdims=True)) + lambda_init)

    def score_map(i, p, slot, c):
        return _score_map(i, p, slot, c, q_refs[c], k_ref, lambda d: bias_ref[0, d], s_ref)

    def soft(i, p, slot, colmax, m):
        offs = []
        for u in range(2):
            j, _, is_pad = _item(i, 2 * p + u)
            off = offs_ref[0, pl.ds(i - j, 1), :] + _pad_off(is_pad)
            offs.append([off, off])
        return _soft(p, slot, colmax, offs, m, s_ref, p_ref)

    def value_map(i, p, slot, c, alpha_c):
        vts = [vt_ref[0, _item(i, 2 * p + u)[0]] for u in range(2)]
        _value_map(i, slot, c, alpha_c, vts, p_ref, acc_ref)

    def finish(i):
        par = i % acc_ref.shape[0]
        o = [acc_ref[par, c, 0:LANES] * (1.0 / acc_ref[par, c, LANES:LANES + 1]) for c in range(2)]
        o_t = o[0] - lam * o[1]
        ms = jnp.mean(o_t * o_t, axis=0, keepdims=True)
        o_n = o_t * lax.rsqrt(ms + NORM_EPS) * subw_ref[...] * (1.0 - lambda_init)
        o_ref[0, i] = (o_n * gt_ref[0, i].astype(F32)).astype(BF16)

    _sweep(tab_ref, 2, acc_ref.shape[0], score_map, soft, value_map, finish)


def _diff(q_split, dk, vt, gt, bias, offs, lamv, subw, lambda_init, v_row_block0, tab):
    B, S, width = dk.shape
    heads = width // LANES
    nt = S // ATT_TILE
    nblk = vt.shape[1]
    unroll = _sweep_unroll(nt)
    ring = _ring_slots(unroll)
    seq_spec = lambda lane_block0: pl.BlockSpec((1, S, LANES), lambda b, h: (b, 0, lane_block0 + h))
    t_spec = lambda row_block0: pl.BlockSpec((1, nblk, LANES, ATT_TILE), lambda b, h: (b, 0, row_block0 + h, 0))
    return pl.pallas_call(
        functools.partial(_diff_kernel, lambda_init=lambda_init),
        grid=(B, heads),
        in_specs=[pl.BlockSpec(memory_space=pltpu.SMEM),
                  seq_spec(0), seq_spec(heads), seq_spec(0), t_spec(v_row_block0), t_spec(v_row_block0),
                  pl.BlockSpec((1, 2, ATT_TILE, ATT_TILE), lambda b, h: (h, 0, 0, 0)),
                  pl.BlockSpec((1, nt, ATT_TILE), lambda b, h: (h, 0, 0)),
                  pl.BlockSpec(lamv.shape, lambda b, h: (0, 0)),
                  pl.BlockSpec((LANES, ATT_TILE), lambda b, h: (0, 0))],
        out_specs=t_spec(0),
        out_shape=jax.ShapeDtypeStruct((B, nt, width, ATT_TILE), BF16),
        scratch_shapes=[pltpu.VMEM((ring, 2, 2, ATT_TILE, ATT_TILE), F32),
                        pltpu.VMEM((ring, 2, 2 * ATT_TILE, ATT_TILE), BF16),
                        pltpu.VMEM((unroll, 2, LANES + BF16_ROWS, ATT_TILE), F32)],
        compiler_params=pltpu.CompilerParams(dimension_semantics=("arbitrary", "arbitrary")),
        name="diff_attn",
    )(tab, q_split, q_split, dk, vt, gt, bias, offs, lamv, subw)


def _outproj_kernel(x_ref, m_ref, d_ref, w_ref, o_ref):
    for t in range(m_ref.shape[1]):
        mixed_t = jnp.concatenate([m_ref[0, t], d_ref[0, t]], axis=0)
        y = lax.dot_general(mixed_t, w_ref[...], (((0,), (0,)), ((), ())), preferred_element_type=F32)
        rows = pl.ds(t * ATT_TILE, ATT_TILE)
        o_ref[0, rows, :] = x_ref[0, rows, :] + y


def _outproj(x, m_att, d_att, w_out):
    B, S, D = x.shape
    width = m_att.shape[2]
    rows = min(PROJ_ROWS, S)
    tiles = rows // ATT_TILE
    row_spec = pl.BlockSpec((1, rows, D), lambda b, t: (b, t, 0))
    t_spec = pl.BlockSpec((1, tiles, width, ATT_TILE), lambda b, t: (b, t, 0, 0))
    return pl.pallas_call(
        _outproj_kernel,
        grid=(B, S // rows),
        in_specs=[row_spec, t_spec, t_spec, pl.BlockSpec(w_out.shape, lambda b, t: (0, 0))],
        out_specs=row_spec,
        out_shape=jax.ShapeDtypeStruct((B, S, D), F32),
        compiler_params=pltpu.CompilerParams(dimension_semantics=("arbitrary", "arbitrary")),
        name="outproj",
    )(x, m_att, d_att, w_out)


def kernel(x, norm_g, w_in, moba_q_norm, moba_k_norm, diff_q_norm, diff_k_norm,
           lambda_q1, lambda_k1, lambda_q2, lambda_k2, diff_subln, w_out):
    B, S, D = x.shape
    depth = w_in.shape[0]
    width = D // 2
    moba_heads = width // HEAD_DIM
    diff_heads = width // (2 * HEAD_DIM)
    assert S % MOBA_BLOCK == 0 and width % MXU_TILE == 0 and S // MOBA_BLOCK > 1
    nt = S // ATT_TILE
    scale = HEAD_DIM ** -0.5 * LOG2E

    m_bias, m_offs = _bias_tables(moba_heads, nt)
    d_bias, d_offs = _bias_tables(diff_heads, nt)
    r = lax.broadcasted_iota(jnp.int32, (MXU_TILE, MXU_TILE), 0) // HEAD_DIM
    c = lax.broadcasted_iota(jnp.int32, (MXU_TILE, MXU_TILE), 1) // HEAD_DIM
    gsum = (r == c).astype(BF16)
    reps = width // HEAD_DIM
    tab = jnp.asarray(_pair_table(nt))

    for layer in range(depth):
        w = w_in[layer]
        cols = [w[:, k * width:(k + 1) * width] for k in range(8)]
        wa = jnp.concatenate([cols[0], cols[1], cols[4], cols[5]], axis=1).astype(BF16)
        wvt = jnp.concatenate([cols[2], cols[6]], axis=1).T.astype(BF16)
        wgt = jnp.concatenate([cols[3], cols[7]], axis=1).T.astype(BF16)
        tile_w = lambda v, s: (jnp.tile(v, reps) * s)[None, :].astype(F32)
        qm, km, dq, dk, gt, vt, kmean = _inproj(
            x, norm_g[layer][None, :], wa, wvt, wgt, gsum,
            tile_w(moba_q_norm[layer], scale), tile_w(moba_k_norm[layer], 1.0),
            tile_w(diff_q_norm[layer], scale), tile_w(diff_k_norm[layer], 1.0))
        m_att = _moba(qm, km, vt, gt, kmean, m_bias, m_offs, tab)
        lambda_init = 0.8 - 0.6 * math.exp(-0.3 * layer)
        lamv = jnp.stack([lambda_q1[layer], lambda_k1[layer], lambda_q2[layer], lambda_k2[layer]])
        subw = jnp.broadcast_to(diff_subln[layer][:, None], (2 * HEAD_DIM, ATT_TILE)).astype(F32)
        d_att = _diff(dq, dk, vt, gt, d_bias, d_offs, lamv, subw, lambda_init, width // LANES, tab)
        x = _outproj(x, m_att, d_att, w_out[layer].astype(BF16))
    return x
```

```python
import functools
import math

import numpy as np
import jax
import jax.numpy as jnp
from jax import lax
from jax.experimental import pallas as pl
from jax.experimental.pallas import tpu as pltpu

HEAD_DIM = 64
MOBA_BLOCK = 256
MOBA_TOPK = 3
NORM_EPS = 1e-6
NEG_INF = -1e30

LANES = 128
MXU_TILE = 256
ATT_TILE = MOBA_BLOCK
PROJ_ROWS = 1024
LOG2E = math.log2(math.e)
BF16_ROWS = 16
SWEEP_UNROLLS = (8, 4, 2)
MAX_RING_SLOTS = 4

F32 = jnp.float32
BF16 = jnp.bfloat16


def _dot(a, b):
    return jnp.dot(a, b, preferred_element_type=F32)


def _dot_nt(a, b):
    return lax.dot_general(a, b, (((1,), (1,)), ((), ())), preferred_element_type=F32)


def _alibi_slopes(n_heads):
    return np.asarray(2.0 ** (-8.0 * np.arange(1, n_heads + 1) / n_heads), dtype=np.float32)


def _bias_tables(n_heads, n_tiles):
    slopes = _alibi_slopes(n_heads) * np.float32(LOG2E)
    k_loc = np.arange(ATT_TILE, dtype=np.int32)[:, None]
    q_loc = np.arange(ATT_TILE, dtype=np.int32)[None, :]
    rel = (k_loc - q_loc).astype(np.float32)
    off = slopes[:, None, None] * rel[None]
    diag = np.where((k_loc <= q_loc)[None], off, np.float32(NEG_INF))
    tiles = np.stack([off, diag], axis=1).astype(np.float32)
    d = np.arange(n_tiles, dtype=np.float32)
    offs = -slopes[:, None] * (ATT_TILE * d)[None, :]
    offs = np.broadcast_to(offs[:, :, None], (n_heads, n_tiles, ATT_TILE)).astype(np.float32)
    return jnp.asarray(tiles), jnp.asarray(offs)


def _project_rows(x, t, g_ref, wa_ref, wvt_ref, wgt_ref, gsum_ref, wqm_ref, wkm_ref, wqd_ref, wkd_ref,
                  qm_ref, km_ref, dq_ref, dk_ref, gt_ref, vt_ref, kmean_ref, width):
    rows = x.shape[0]
    blocks = rows // MOBA_BLOCK
    ms = jnp.mean(x * x, axis=-1, keepdims=True)
    h = (x * lax.rsqrt(ms + NORM_EPS) * g_ref[...]).astype(BF16)

    gsum = gsum_ref[...]
    low_half = lax.broadcasted_iota(jnp.int32, (rows, width), 1) % LANES < HEAD_DIM

    def head_norm(y, w):
        sq = (y * y).astype(BF16)
        parts = [_dot(sq[:, c * MXU_TILE:(c + 1) * MXU_TILE], gsum) for c in range(width // MXU_TILE)]
        ssq = jnp.concatenate(parts, axis=1)
        return y * lax.rsqrt(ssq * (1.0 / HEAD_DIM) + NORM_EPS) * w

    def store_split(ref, q):
        ref[0, :, 0:width] = jnp.where(low_half, q, 0.0).astype(BF16)
        ref[0, :, width:2 * width] = jnp.where(low_half, 0.0, q).astype(BF16)

    def silu(y):
        return y * (1.0 / (1.0 + jnp.exp(-y)))

    y = _dot(h, wa_ref[:, 0 * width:1 * width])
    store_split(qm_ref, head_norm(y, wqm_ref[...]))

    y = _dot(h, wa_ref[:, 1 * width:2 * width])
    kn = head_norm(y, wkm_ref[...])
    km_ref[0] = kn.astype(BF16)
    for s in range(blocks):
        kmean_ref[0, pl.ds(t * blocks + s, 1), :] = jnp.mean(
            kn[s * MOBA_BLOCK:(s + 1) * MOBA_BLOCK], axis=0, keepdims=True)

    y = _dot(h, wa_ref[:, 2 * width:3 * width])
    store_split(dq_ref, head_norm(y, wqd_ref[...]))

    y = _dot(h, wa_ref[:, 3 * width:4 * width])
    dk_ref[0] = head_norm(y, wkd_ref[...]).astype(BF16)

    vt = _dot_nt(wvt_ref[...], h)
    gt = silu(_dot_nt(wgt_ref[...], h))
    for s in range(blocks):
        vt_ref[0, s] = vt[:, s * MOBA_BLOCK:(s + 1) * MOBA_BLOCK].astype(BF16)
        gt_ref[0, s] = gt[:, s * MOBA_BLOCK:(s + 1) * MOBA_BLOCK].astype(BF16)


def _merge_rows(x_ref, m_ref, d_ref, w_ref):
    parts = []
    for t in range(m_ref.shape[1]):
        mixed_t = jnp.concatenate([m_ref[0, t], d_ref[0, t]], axis=0)
        y = lax.dot_general(mixed_t, w_ref[...], (((0,), (0,)), ((), ())), preferred_element_type=F32)
        parts.append(x_ref[0, t * ATT_TILE:(t + 1) * ATT_TILE, :] + y)
    return jnp.concatenate(parts, axis=0)


def _inproj_kernel(x_ref, *refs, width):
    _project_rows(x_ref[0], pl.program_id(1), *refs, width)


def _mid_kernel(x_ref, m_ref, d_ref, wo_ref, *refs, width):
    xo_ref, out_refs = refs[9], refs[10:]
    x_new = _merge_rows(x_ref, m_ref, d_ref, wo_ref)
    xo_ref[0] = x_new
    _project_rows(x_new, pl.program_id(1), *refs[:9], *out_refs, width)


def _inproj(x, g, wa, wvt, wgt, gsum, wqm, wkm, wqd, wkd, merge=None):
    B, S, D = x.shape
    width = wa.shape[1] // 4
    rows = min(PROJ_ROWS, S)
    nblk = S // MOBA_BLOCK
    blocks = rows // MOBA_BLOCK
    full = lambda shape: pl.BlockSpec(shape, lambda b, t: (0,) * len(shape))
    row_spec = lambda w: pl.BlockSpec((1, rows, w), lambda b, t: (b, t, 0))
    act = lambda w: jax.ShapeDtypeStruct((B, S, w), BF16)
    t_spec = pl.BlockSpec((1, blocks, 2 * width, MOBA_BLOCK), lambda b, t: (b, t, 0, 0))
    t_shape = jax.ShapeDtypeStruct((B, nblk, 2 * width, MOBA_BLOCK), BF16)
    weights = (g, wa, wvt, wgt, gsum, wqm, wkm, wqd, wkd)
    in_specs = [row_spec(D)] + [full(w.shape) for w in weights]
    out_specs = [row_spec(2 * width), row_spec(width), row_spec(2 * width), row_spec(width),
                 t_spec, t_spec, pl.BlockSpec((1, nblk, width), lambda b, t: (b, 0, 0))]
    out_shape = [act(2 * width), act(width), act(2 * width), act(width), t_shape, t_shape,
                 jax.ShapeDtypeStruct((B, nblk, width), F32)]
    params = pltpu.CompilerParams(dimension_semantics=("arbitrary", "arbitrary"))
    if merge is None:
        return pl.pallas_call(
            functools.partial(_inproj_kernel, width=width), grid=(B, S // rows),
            in_specs=in_specs, out_specs=out_specs, out_shape=out_shape,
            compiler_params=params, name="inproj")(x, *weights)
    m_att, d_att, w_out = merge
    in_specs = [row_spec(D), t_spec_of(m_att, blocks), t_spec_of(d_att, blocks), full(w_out.shape)] + in_specs[1:]
    outs = pl.pallas_call(
        functools.partial(_mid_kernel, width=width), grid=(B, S // rows),
        in_specs=in_specs, out_specs=[row_spec(D)] + out_specs,
        out_shape=[jax.ShapeDtypeStruct((B, S, D), F32)] + out_shape,
        compiler_params=params, name="outproj_inproj")(x, m_att, d_att, w_out, *weights)
    return outs[0], outs[1:]


def t_spec_of(a, tiles):
    return pl.BlockSpec((1, tiles, a.shape[2], ATT_TILE), lambda b, t: (b, t, 0, 0))


def _pair_table(n_tiles):
    rows = [(i, p, int(p == i // 2)) for i in range(n_tiles) for p in range(i // 2 + 1)]
    rows += [(rows[-1][0], rows[-1][1], 0)] * 2
    return np.ascontiguousarray(np.asarray(rows, dtype=np.int32).T)


def _item(i, t):
    t = jnp.asarray(t, jnp.int32)
    is_diag = t == 0
    j = jnp.where(is_diag, i, t - 1)
    is_pad = jnp.logical_and(t > 0, t - 1 >= i)
    return j, is_diag, is_pad


def _rows(i):
    return pl.ds(pl.multiple_of(i * ATT_TILE, ATT_TILE), ATT_TILE)


def _pad_off(is_pad):
    return jnp.where(is_pad, jnp.float32(NEG_INF), jnp.float32(0.0))


def _sweep_unroll(n_tiles):
    n_pairs = _pair_table(n_tiles).shape[1] - 2
    return next(u for u in SWEEP_UNROLLS if n_pairs % u == 0)


def _ring_slots(unroll):
    return min(unroll, MAX_RING_SLOTS)


def _sweep(tab_ref, n_maps, unroll, score_map, soft, value_map, finish):
    n_pairs = tab_ref.shape[1] - 2
    ring = _ring_slots(unroll)
    assert n_pairs % unroll == 0 and unroll % ring == 0
    pair = lambda g: (tab_ref[0, g], tab_ref[1, g])

    def score_all(g, slot):
        return tuple(x for c in range(n_maps) for x in score_map(*pair(g), slot, c))

    def half_step(g, k, carry):
        colmax, m, alpha = carry
        colmax1 = []
        for c in range(n_maps):
            colmax1 += score_map(*pair(g), k, c)
            value_map(*pair(g - 2), (k - 2) % ring, c, alpha[c])
        m1, alpha1 = soft(*pair(g - 1), (k - 1) % ring, colmax, m)
        return tuple(colmax1), m1, alpha1

    colmax = score_all(0, 0)
    m_init = tuple(jnp.full((1, ATT_TILE), NEG_INF, F32) for _ in range(n_maps))
    m, alpha = soft(*pair(0), 0, colmax, m_init)
    carry = (score_all(1, 1), m, alpha)

    def step(h, carry):
        g = 2 + unroll * h
        for k in range(unroll):
            carry = half_step(g + k, (2 + k) % ring, carry)
        for k in range(unroll):
            @pl.when(tab_ref[2, g + k - 2] == 1)
            def _(k=k):
                finish(tab_ref[0, g + k - 2])
        return carry

    lax.fori_loop(0, n_pairs // unroll, step, carry)


def _score_map(i, p, slot, c, q_ref, k_ref, bias_of, s_ref):
    q = q_ref[0, _rows(i), :]
    items = [_item(i, 2 * p + u) for u in range(2)]
    kt = jnp.concatenate([k_ref[0, _rows(j), :] for j, _, _ in items], axis=0)
    s2 = _dot_nt(kt, q)
    colmax = []
    for u, (_, is_diag, _) in enumerate(items):
        s = s2[u * ATT_TILE:(u + 1) * ATT_TILE] + bias_of(is_diag.astype(jnp.int32))
        s_ref[slot, u, c] = s
        colmax.append(jnp.max(s, axis=0, keepdims=True))
    return colmax


def _soft(p, slot, colmax, offs, m, s_ref, p_ref):
    m_out, alpha_out = [], []
    for c in range(len(m)):
        m_old = jnp.where(p == 0, jnp.float32(NEG_INF), m[c])
        m_new = jnp.maximum(m_old, jnp.maximum(colmax[2 * c] + offs[0][c], colmax[2 * c + 1] + offs[1][c]))
        for u in range(2):
            pu = jnp.exp2(s_ref[slot, u, c] - (m_new - offs[u][c]))
            p_ref[slot, c, u * ATT_TILE:(u + 1) * ATT_TILE, :] = pu.astype(BF16)
        m_out.append(m_new)
        alpha_out.append(jnp.exp2(m_old - m_new))
    return tuple(m_out), tuple(alpha_out)


def _value_map(i, slot, c, alpha_c, vts, p_ref, acc_ref):
    vt = jnp.concatenate(vts, axis=1)
    vt1 = jnp.concatenate([vt, jnp.ones((BF16_ROWS, vt.shape[1]), BF16)], axis=0)
    par = i % acc_ref.shape[0]
    acc_ref[par, c] = alpha_c * acc_ref[par, c] + _dot(vt1, p_ref[slot, c])


def _moba_kernel(tab_ref, qa_ref, qb_ref, k_ref, vt_ref, gt_ref, kmean_ref, bias_ref, offs_ref, o_ref,
                 sel_ref, s_ref, p_ref, acc_ref):
    nblk = kmean_ref.shape[1]
    n_tiles = k_ref.shape[1] // ATT_TILE
    q_refs = (qa_ref, qb_ref)
    acc_ref[...] = jnp.zeros(acc_ref.shape, F32)

    km = kmean_ref[0]
    km_hi = km.astype(BF16)
    km_lo = (km - km_hi.astype(F32)).astype(BF16)
    blk = lax.broadcasted_iota(jnp.int32, (nblk, ATT_TILE), 0)
    for i in range(n_tiles):
        valid = blk < i
        for hh in range(2):
            q = q_refs[hh][0, i * ATT_TILE:(i + 1) * ATT_TILE, :]
            g = _dot_nt(km_hi, q) + _dot_nt(km_lo, q)
            g = jnp.where(valid, g, -jnp.inf)
            sel = blk == i
            for _ in range(min(MOBA_TOPK, nblk - 1)):
                mx = jnp.max(g, axis=0, keepdims=True)
                first = jnp.min(jnp.where(g == mx, blk, nblk), axis=0, keepdims=True)
                pick = blk == first
                sel = jnp.logical_or(sel, jnp.logical_and(pick, valid))
                g = jnp.where(pick, -jnp.inf, g)
            sel_ref[i, hh] = jnp.where(sel, 0.0, NEG_INF).astype(F32)

    def score_map(i, p, slot, hh):
        return _score_map(i, p, slot, hh, q_refs[hh], k_ref, lambda d: bias_ref[hh, d], s_ref)

    def soft(i, p, slot, colmax, m):
        offs = []
        for u in range(2):
            j, _, is_pad = _item(i, 2 * p + u)
            pad = _pad_off(is_pad)
            offs.append([offs_ref[hh, pl.ds(i - j, 1), :] + sel_ref[i, hh, pl.ds(j, 1), :] + pad
                         for hh in range(2)])
        return _soft(p, slot, colmax, offs, m, s_ref, p_ref)

    def value_map(i, p, slot, hh, alpha_c):
        vts = []
        for u in range(2):
            j, _, _ = _item(i, 2 * p + u)
            vts.append(vt_ref[0, j, hh * HEAD_DIM:(hh + 1) * HEAD_DIM, :])
        _value_map(i, slot, hh, alpha_c, vts, p_ref, acc_ref)

    def finish(i):
        par = i % acc_ref.shape[0]
        o_t = jnp.concatenate([acc_ref[par, hh, 0:HEAD_DIM] * (1.0 / acc_ref[par, hh, HEAD_DIM:HEAD_DIM + 1])
                               for hh in range(2)], axis=0)
        o_ref[0, i] = (o_t * gt_ref[0, i].astype(F32)).astype(BF16)

    _sweep(tab_ref, 2, acc_ref.shape[0], score_map, soft, value_map, finish)


def _moba(q_split, km, vt, gt, kmean, bias, offs, tab):
    B, S, width = km.shape
    pairs = width // LANES
    nt = S // ATT_TILE
    nblk = kmean.shape[1]
    unroll = _sweep_unroll(nt)
    ring = _ring_slots(unroll)
    seq_spec = lambda lane_block0: pl.BlockSpec((1, S, LANES), lambda b, p: (b, 0, lane_block0 + p))
    t_spec = pl.BlockSpec((1, nblk, LANES, ATT_TILE), lambda b, p: (b, 0, p, 0))
    return pl.pallas_call(
        _moba_kernel,
        grid=(B, pairs),
        in_specs=[pl.BlockSpec(memory_space=pltpu.SMEM),
                  seq_spec(0), seq_spec(pairs), seq_spec(0), t_spec, t_spec,
                  pl.BlockSpec((1, nblk, LANES), lambda b, p: (b, 0, p)),
                  pl.BlockSpec((2, 2, ATT_TILE, ATT_TILE), lambda b, p: (p, 0, 0, 0)),
                  pl.BlockSpec((2, nt, ATT_TILE), lambda b, p: (p, 0, 0))],
        out_specs=t_spec,
        out_shape=jax.ShapeDtypeStruct((B, nt, width, ATT_TILE), BF16),
        scratch_shapes=[pltpu.VMEM((nt, 2, nblk, ATT_TILE), F32),
                        pltpu.VMEM((ring, 2, 2, ATT_TILE, ATT_TILE), F32),
                        pltpu.VMEM((ring, 2, 2 * ATT_TILE, ATT_TILE), BF16),
                        pltpu.VMEM((unroll, 2, HEAD_DIM + BF16_ROWS, ATT_TILE), F32)],
        compiler_params=pltpu.CompilerParams(dimension_semantics=("arbitrary", "arbitrary")),
        name="moba_attn",
    )(tab, q_split, q_split, km, vt, gt, kmean, bias, offs)


def _diff_kernel(tab_ref, qa_ref, qb_ref, k_ref, vt_ref, gt_ref, bias_ref, offs_ref, lamv_ref, subw_ref,
                 o_ref, s_ref, p_ref, acc_ref, *, lambda_init):
    q_refs = (qa_ref, qb_ref)
    acc_ref[...] = jnp.zeros(acc_ref.shape, F32)
    lv = lamv_ref[...]
    lam = (jnp.exp(jnp.sum(lv[0:1] * lv[1:2], axis=-1, keepdims=True))
           - jnp.exp(jnp.sum(lv[2:3] * lv[3:4], axis=-1, keepdims=True)) + lambda_init)

    def score_map(i, p, slot, c):
        return _score_map(i, p, slot, c, q_refs[c], k_ref, lambda d: bias_ref[0, d], s_ref)

    def soft(i, p, slot, colmax, m):
        offs = []
        for u in range(2):
            j, _, is_pad = _item(i, 2 * p + u)
            off = offs_ref[0, pl.ds(i - j, 1), :] + _pad_off(is_pad)
            offs.append([off, off])
        return _soft(p, slot, colmax, offs, m, s_ref, p_ref)

    def value_map(i, p, slot, c, alpha_c):
        vts = [vt_ref[0, _item(i, 2 * p + u)[0]] for u in range(2)]
        _value_map(i, slot, c, alpha_c, vts, p_ref, acc_ref)

    def finish(i):
        par = i % acc_ref.shape[0]
        o = [acc_ref[par, c, 0:LANES] * (1.0 / acc_ref[par, c, LANES:LANES + 1]) for c in range(2)]
        o_t = o[0] - lam * o[1]
        ms = jnp.mean(o_t * o_t, axis=0, keepdims=True)
        o_n = o_t * lax.rsqrt(ms + NORM_EPS) * subw_ref[...] * (1.0 - lambda_init)
        o_ref[0, i] = (o_n * gt_ref[0, i].astype(F32)).astype(BF16)

    _sweep(tab_ref, 2, acc_ref.shape[0], score_map, soft, value_map, finish)


def _diff(q_split, dk, vt, gt, bias, offs, lamv, subw, lambda_init, v_row_block0, tab):
    B, S, width = dk.shape
    heads = width // LANES
    nt = S // ATT_TILE
    nblk = vt.shape[1]
    unroll = _sweep_unroll(nt)
    ring = _ring_slots(unroll)
    seq_spec = lambda lane_block0: pl.BlockSpec((1, S, LANES), lambda b, h: (b, 0, lane_block0 + h))
    t_spec = lambda row_block0: pl.BlockSpec((1, nblk, LANES, ATT_TILE), lambda b, h: (b, 0, row_block0 + h, 0))
    return pl.pallas_call(
        functools.partial(_diff_kernel, lambda_init=lambda_init),
        grid=(B, heads),
        in_specs=[pl.BlockSpec(memory_space=pltpu.SMEM),
                  seq_spec(0), seq_spec(heads), seq_spec(0), t_spec(v_row_block0), t_spec(v_row_block0),
                  pl.BlockSpec((1, 2, ATT_TILE, ATT_TILE), lambda b, h: (h, 0, 0, 0)),
                  pl.BlockSpec((1, nt, ATT_TILE), lambda b, h: (h, 0, 0)),
                  pl.BlockSpec(lamv.shape, lambda b, h: (0, 0)),
                  pl.BlockSpec((LANES, ATT_TILE), lambda b, h: (0, 0))],
        out_specs=t_spec(0),
        out_shape=jax.ShapeDtypeStruct((B, nt, width, ATT_TILE), BF16),
        scratch_shapes=[pltpu.VMEM((ring, 2, 2, ATT_TILE, ATT_TILE), F32),
                        pltpu.VMEM((ring, 2, 2 * ATT_TILE, ATT_TILE), BF16),
                        pltpu.VMEM((unroll, 2, LANES + BF16_ROWS, ATT_TILE), F32)],
        compiler_params=pltpu.CompilerParams(dimension_semantics=("arbitrary", "arbitrary")),
        name="diff_attn",
    )(tab, q_split, q_split, dk, vt, gt, bias, offs, lamv, subw)


def _outproj_kernel(x_ref, m_ref, d_ref, w_ref, o_ref):
    o_ref[0] = _merge_rows(x_ref, m_ref, d_ref, w_ref)


def _outproj(x, m_att, d_att, w_out):
    B, S, D = x.shape
    rows = min(PROJ_ROWS, S)
    tiles = rows // ATT_TILE
    row_spec = pl.BlockSpec((1, rows, D), lambda b, t: (b, t, 0))
    return pl.pallas_call(
        _outproj_kernel,
        grid=(B, S // rows),
        in_specs=[row_spec, t_spec_of(m_att, tiles), t_spec_of(d_att, tiles),
                  pl.BlockSpec(w_out.shape, lambda b, t: (0, 0))],
        out_specs=row_spec,
        out_shape=jax.ShapeDtypeStruct((B, S, D), F32),
        compiler_params=pltpu.CompilerParams(dimension_semantics=("arbitrary", "arbitrary")),
        name="outproj",
    )(x, m_att, d_att, w_out)


def kernel(x, norm_g, w_in, moba_q_norm, moba_k_norm, diff_q_norm, diff_k_norm,
           lambda_q1, lambda_k1, lambda_q2, lambda_k2, diff_subln, w_out):
    B, S, D = x.shape
    depth = w_in.shape[0]
    width = D // 2
    moba_heads = width // HEAD_DIM
    diff_heads = width // (2 * HEAD_DIM)
    assert S % MOBA_BLOCK == 0 and width % MXU_TILE == 0 and S // MOBA_BLOCK > 1
    nt = S // ATT_TILE
    scale = HEAD_DIM ** -0.5 * LOG2E

    m_bias, m_offs = _bias_tables(moba_heads, nt)
    d_bias, d_offs = _bias_tables(diff_heads, nt)
    r = lax.broadcasted_iota(jnp.int32, (MXU_TILE, MXU_TILE), 0) // HEAD_DIM
    c = lax.broadcasted_iota(jnp.int32, (MXU_TILE, MXU_TILE), 1) // HEAD_DIM
    gsum = (r == c).astype(BF16)
    reps = width // HEAD_DIM
    tab = jnp.asarray(_pair_table(nt))

    def layer_weights(layer):
        w = w_in[layer]
        cols = [w[:, k * width:(k + 1) * width] for k in range(8)]
        wa = jnp.concatenate([cols[0], cols[1], cols[4], cols[5]], axis=1).astype(BF16)
        wvt = jnp.concatenate([cols[2], cols[6]], axis=1).T.astype(BF16)
        wgt = jnp.concatenate([cols[3], cols[7]], axis=1).T.astype(BF16)
        tile_w = lambda v, s: (jnp.tile(v, reps) * s)[None, :].astype(F32)
        return (norm_g[layer][None, :], wa, wvt, wgt, gsum,
                tile_w(moba_q_norm[layer], scale), tile_w(moba_k_norm[layer], 1.0),
                tile_w(diff_q_norm[layer], scale), tile_w(diff_k_norm[layer], 1.0))

    proj = _inproj(x, *layer_weights(0))
    for layer in range(depth):
        qm, km, dq, dk, gt, vt, kmean = proj
        m_att = _moba(qm, km, vt, gt, kmean, m_bias, m_offs, tab)
        lambda_init = 0.8 - 0.6 * math.exp(-0.3 * layer)
        lamv = jnp.stack([lambda_q1[layer], lambda_k1[layer], lambda_q2[layer], lambda_k2[layer]])
        subw = jnp.broadcast_to(diff_subln[layer][:, None], (2 * HEAD_DIM, ATT_TILE)).astype(F32)
        d_att = _diff(dq, dk, vt, gt, d_bias, d_offs, lamv, subw, lambda_init, width // LANES, tab)
        wo = w_out[layer].astype(BF16)
        if layer + 1 < depth:
            x, proj = _inproj(x, *layer_weights(layer + 1), merge=(m_att, d_att, wo))
        else:
            x = _outproj(x, m_att, d_att, wo)
    return x
```

```python
import functools
import math

import numpy as np
import jax
import jax.numpy as jnp
from jax import lax
from jax.experimental import pallas as pl
from jax.experimental.pallas import tpu as pltpu

HEAD_DIM = 64
MOBA_BLOCK = 256
MOBA_TOPK = 3
NORM_EPS = 1e-6
NEG_INF = -1e30

LANES = 128
MXU_TILE = 256
ATT_TILE = MOBA_BLOCK
PROJ_ROWS = 1024
LOG2E = math.log2(math.e)
BF16_ROWS = 16
SWEEP_UNROLLS = (8, 4, 2)
MAX_RING_SLOTS = 4

F32 = jnp.float32
BF16 = jnp.bfloat16


def _dot(a, b):
    return jnp.dot(a, b, preferred_element_type=F32)


def _dot_nt(a, b):
    return lax.dot_general(a, b, (((1,), (1,)), ((), ())), preferred_element_type=F32)


def _alibi_slopes(n_heads):
    return np.asarray(2.0 ** (-8.0 * np.arange(1, n_heads + 1) / n_heads), dtype=np.float32)


def _bias_tables(n_heads, n_tiles):
    slopes = _alibi_slopes(n_heads) * np.float32(LOG2E)
    k_loc = np.arange(ATT_TILE, dtype=np.int32)[:, None]
    q_loc = np.arange(ATT_TILE, dtype=np.int32)[None, :]
    rel = (k_loc - q_loc).astype(np.float32)
    off = slopes[:, None, None] * rel[None]
    diag = np.where((k_loc <= q_loc)[None], off, np.float32(NEG_INF))
    tiles = np.stack([off, diag], axis=1).astype(np.float32)
    d = np.arange(n_tiles, dtype=np.float32)
    offs = -slopes[:, None] * (ATT_TILE * d)[None, :]
    offs = np.broadcast_to(offs[:, :, None], (n_heads, n_tiles, ATT_TILE)).astype(np.float32)
    return jnp.asarray(tiles), jnp.asarray(offs)


def _project_rows(x, t, g_ref, wa_ref, wvt_ref, wgt_ref, gsum_ref, wqm_ref, wkm_ref, wqd_ref, wkd_ref,
                  qm_ref, km_ref, dq_ref, dk_ref, gt_ref, vt_ref, kmean_ref, width):
    rows = x.shape[0]
    blocks = rows // MOBA_BLOCK
    ms = jnp.mean(x * x, axis=-1, keepdims=True)
    h = (x * lax.rsqrt(ms + NORM_EPS) * g_ref[...]).astype(BF16)

    gsum = gsum_ref[...]
    low_half = lax.broadcasted_iota(jnp.int32, (rows, width), 1) % LANES < HEAD_DIM

    def head_norm(y, w):
        sq = (y * y).astype(BF16)
        parts = [_dot(sq[:, c * MXU_TILE:(c + 1) * MXU_TILE], gsum) for c in range(width // MXU_TILE)]
        ssq = jnp.concatenate(parts, axis=1)
        return y * lax.rsqrt(ssq * (1.0 / HEAD_DIM) + NORM_EPS) * w

    def store_split(ref, q):
        ref[0, :, 0:width] = jnp.where(low_half, q, 0.0).astype(BF16)
        ref[0, :, width:2 * width] = jnp.where(low_half, 0.0, q).astype(BF16)

    def silu(y):
        return y * (1.0 / (1.0 + jnp.exp(-y)))

    y = _dot(h, wa_ref[:, 0 * width:1 * width])
    store_split(qm_ref, head_norm(y, wqm_ref[...]))

    y = _dot(h, wa_ref[:, 1 * width:2 * width])
    kn = head_norm(y, wkm_ref[...])
    km_ref[0] = kn.astype(BF16)
    for s in range(blocks):
        kmean_ref[0, pl.ds(t * blocks + s, 1), :] = jnp.mean(
            kn[s * MOBA_BLOCK:(s + 1) * MOBA_BLOCK], axis=0, keepdims=True)

    y = _dot(h, wa_ref[:, 2 * width:3 * width])
    store_split(dq_ref, head_norm(y, wqd_ref[...]))

    y = _dot(h, wa_ref[:, 3 * width:4 * width])
    dk_ref[0] = head_norm(y, wkd_ref[...]).astype(BF16)

    vt = _dot_nt(wvt_ref[...], h)
    gt = silu(_dot_nt(wgt_ref[...], h))
    for s in range(blocks):
        vt_ref[0, s] = vt[:, s * MOBA_BLOCK:(s + 1) * MOBA_BLOCK].astype(BF16)
        gt_ref[0, s] = gt[:, s * MOBA_BLOCK:(s + 1) * MOBA_BLOCK].astype(BF16)


def _merge_rows(x_ref, m_ref, d_ref, w_ref):
    parts = []
    for t in range(m_ref.shape[1]):
        mixed_t = jnp.concatenate([m_ref[0, t], d_ref[0, t]], axis=0)
        y = lax.dot_general(mixed_t, w_ref[...], (((0,), (0,)), ((), ())), preferred_element_type=F32)
        parts.append(x_ref[0, t * ATT_TILE:(t + 1) * ATT_TILE, :] + y)
    return jnp.concatenate(parts, axis=0)


def _inproj_kernel(x_ref, *refs, width):
    _project_rows(x_ref[0], pl.program_id(1), *refs, width)


def _mid_kernel(x_ref, m_ref, d_ref, wo_ref, *refs, width):
    xo_ref, out_refs = refs[9], refs[10:]
    x_new = _merge_rows(x_ref, m_ref, d_ref, wo_ref)
    xo_ref[0] = x_new
    _project_rows(x_new, pl.program_id(1), *refs[:9], *out_refs, width)


def _inproj(x, g, wa, wvt, wgt, gsum, wqm, wkm, wqd, wkd, merge=None):
    B, S, D = x.shape
    width = wa.shape[1] // 4
    rows = min(PROJ_ROWS, S)
    nblk = S // MOBA_BLOCK
    blocks = rows // MOBA_BLOCK
    full = lambda shape: pl.BlockSpec(shape, lambda b, t: (0,) * len(shape))
    row_spec = lambda w: pl.BlockSpec((1, rows, w), lambda b, t: (b, t, 0))
    act = lambda w: jax.ShapeDtypeStruct((B, S, w), BF16)
    t_spec = pl.BlockSpec((1, blocks, 2 * width, MOBA_BLOCK), lambda b, t: (b, t, 0, 0))
    t_shape = jax.ShapeDtypeStruct((B, nblk, 2 * width, MOBA_BLOCK), BF16)
    weights = (g, wa, wvt, wgt, gsum, wqm, wkm, wqd, wkd)
    in_specs = [row_spec(D)] + [full(w.shape) for w in weights]
    out_specs = [row_spec(2 * width), row_spec(width), row_spec(2 * width), row_spec(width),
                 t_spec, t_spec, pl.BlockSpec((1, nblk, width), lambda b, t: (b, 0, 0))]
    out_shape = [act(2 * width), act(width), act(2 * width), act(width), t_shape, t_shape,
                 jax.ShapeDtypeStruct((B, nblk, width), F32)]
    params = pltpu.CompilerParams(dimension_semantics=("arbitrary", "arbitrary"))
    if merge is None:
        return pl.pallas_call(
            functools.partial(_inproj_kernel, width=width), grid=(B, S // rows),
            in_specs=in_specs, out_specs=out_specs, out_shape=out_shape,
            compiler_params=params, name="inproj")(x, *weights)
    m_att, d_att, w_out = merge
    in_specs = [row_spec(D), t_spec_of(m_att, blocks), t_spec_of(d_att, blocks), full(w_out.shape)] + in_specs[1:]
    outs = pl.pallas_call(
        functools.partial(_mid_kernel, width=width), grid=(B, S // rows),
        in_specs=in_specs, out_specs=[row_spec(D)] + out_specs,
        out_shape=[jax.ShapeDtypeStruct((B, S, D), F32)] + out_shape,
        compiler_params=params, name="outproj_inproj")(x, m_att, d_att, w_out, *weights)
    return outs[0], outs[1:]


def t_spec_of(a, tiles):
    return pl.BlockSpec((1, tiles, a.shape[2], ATT_TILE), lambda b, t: (b, t, 0, 0))


def _pair_table(n_tiles):
    rows = [(i, p, int(p == i // 2)) for i in range(n_tiles) for p in range(i // 2 + 1)]
    rows += [(rows[-1][0], rows[-1][1], 0)] * 2
    return np.ascontiguousarray(np.asarray(rows, dtype=np.int32).T)


def _item(i, t):
    t = jnp.asarray(t, jnp.int32)
    is_diag = t == 0
    j = jnp.where(is_diag, i, t - 1)
    is_pad = jnp.logical_and(t > 0, t - 1 >= i)
    return j, is_diag, is_pad


def _rows(i):
    return pl.ds(pl.multiple_of(i * ATT_TILE, ATT_TILE), ATT_TILE)


def _pad_off(is_pad):
    return jnp.where(is_pad, jnp.float32(NEG_INF), jnp.float32(0.0))


def _sweep_unroll(n_tiles):
    n_pairs = _pair_table(n_tiles).shape[1] - 2
    return next(u for u in SWEEP_UNROLLS if n_pairs % u == 0)


def _ring_slots(unroll):
    return min(unroll, MAX_RING_SLOTS)


def _sweep(tab_ref, n_maps, unroll, score_map, soft, value_map, finish):
    n_pairs = tab_ref.shape[1] - 2
    ring = _ring_slots(unroll)
    assert n_pairs % unroll == 0 and unroll % ring == 0
    pair = lambda g: (tab_ref[0, g], tab_ref[1, g])

    def score_all(g, slot):
        return tuple(x for c in range(n_maps) for x in score_map(*pair(g), slot, c))

    def half_step(g, k, carry):
        colmax, m, alpha = carry
        colmax1 = []
        for c in range(n_maps):
            colmax1 += score_map(*pair(g), k, c)
            value_map(*pair(g - 2), (k - 2) % ring, c, alpha[c])
        m1, alpha1 = soft(*pair(g - 1), (k - 1) % ring, colmax, m)
        return tuple(colmax1), m1, alpha1

    colmax = score_all(0, 0)
    m_init = tuple(jnp.full((1, ATT_TILE), NEG_INF, F32) for _ in range(n_maps))
    m, alpha = soft(*pair(0), 0, colmax, m_init)
    carry = (score_all(1, 1), m, alpha)

    def step(h, carry):
        g = 2 + unroll * h
        for k in range(unroll):
            carry = half_step(g + k, (2 + k) % ring, carry)
        for k in range(unroll):
            @pl.when(tab_ref[2, g + k - 2] == 1)
            def _(k=k):
                finish(tab_ref[0, g + k - 2])
        return carry

    lax.fori_loop(0, n_pairs // unroll, step, carry)


def _score_map(i, p, slot, c, q_ref, k_ref, bias_of, s_ref):
    q = q_ref[0, _rows(i), :]
    items = [_item(i, 2 * p + u) for u in range(2)]
    kt = jnp.concatenate([k_ref[0, _rows(j), :] for j, _, _ in items], axis=0)
    s2 = _dot_nt(kt, q)
    colmax = []
    for u, (_, is_diag, _) in enumerate(items):
        s = s2[u * ATT_TILE:(u + 1) * ATT_TILE] + bias_of(is_diag.astype(jnp.int32))
        s_ref[slot, u, c] = s.astype(BF16)
        colmax.append(jnp.max(s, axis=0, keepdims=True))
    return colmax


def _soft(p, slot, colmax, offs, m, s_ref, p_ref):
    m_out, alpha_out = [], []
    for c in range(len(m)):
        m_old = jnp.where(p == 0, jnp.float32(NEG_INF), m[c])
        m_new = jnp.maximum(m_old, jnp.maximum(colmax[2 * c] + offs[0][c], colmax[2 * c + 1] + offs[1][c]))
        for u in range(2):
            shift = (m_new - offs[u][c]).astype(BF16)
            p_ref[slot, c, u * ATT_TILE:(u + 1) * ATT_TILE, :] = jnp.exp2(s_ref[slot, u, c] - shift)
        m_out.append(m_new)
        alpha_out.append(jnp.exp2(m_old - m_new))
    return tuple(m_out), tuple(alpha_out)


def _value_map(i, slot, c, alpha_c, vts, p_ref, acc_ref):
    vt = jnp.concatenate(vts, axis=1)
    vt1 = jnp.concatenate([vt, jnp.ones((BF16_ROWS, vt.shape[1]), BF16)], axis=0)
    par = i % acc_ref.shape[0]
    acc_ref[par, c] = alpha_c * acc_ref[par, c] + _dot(vt1, p_ref[slot, c])


def _moba_kernel(tab_ref, qa_ref, qb_ref, k_ref, vt_ref, gt_ref, kmean_ref, bias_ref, offs_ref, o_ref,
                 sel_ref, s_ref, p_ref, acc_ref):
    nblk = kmean_ref.shape[1]
    n_tiles = k_ref.shape[1] // ATT_TILE
    q_refs = (qa_ref, qb_ref)
    acc_ref[...] = jnp.zeros(acc_ref.shape, F32)

    km = kmean_ref[0]
    km_hi = km.astype(BF16)
    km_lo = (km - km_hi.astype(F32)).astype(BF16)
    blk = lax.broadcasted_iota(jnp.int32, (nblk, ATT_TILE), 0)
    for i in range(n_tiles):
        valid = blk < i
        for hh in range(2):
            q = q_refs[hh][0, i * ATT_TILE:(i + 1) * ATT_TILE, :]
            g = _dot_nt(km_hi, q) + _dot_nt(km_lo, q)
            g = jnp.where(valid, g, -jnp.inf)
            sel = blk == i
            for _ in range(min(MOBA_TOPK, nblk - 1)):
                mx = jnp.max(g, axis=0, keepdims=True)
                first = jnp.min(jnp.where(g == mx, blk, nblk), axis=0, keepdims=True)
                pick = blk == first
                sel = jnp.logical_or(sel, jnp.logical_and(pick, valid))
                g = jnp.where(pick, -jnp.inf, g)
            sel_ref[i, hh] = jnp.where(sel, 0.0, NEG_INF).astype(F32)

    def score_map(i, p, slot, hh):
        return _score_map(i, p, slot, hh, q_refs[hh], k_ref, lambda d: bias_ref[hh, d], s_ref)

    def soft(i, p, slot, colmax, m):
        offs = []
        for u in range(2):
            j, _, is_pad = _item(i, 2 * p + u)
            pad = _pad_off(is_pad)
            offs.append([offs_ref[hh, pl.ds(i - j, 1), :] + sel_ref[i, hh, pl.ds(j, 1), :] + pad
                         for hh in range(2)])
        return _soft(p, slot, colmax, offs, m, s_ref, p_ref)

    def value_map(i, p, slot, hh, alpha_c):
        vts = []
        for u in range(2):
            j, _, _ = _item(i, 2 * p + u)
            vts.append(vt_ref[0, j, hh * HEAD_DIM:(hh + 1) * HEAD_DIM, :])
        _value_map(i, slot, hh, alpha_c, vts, p_ref, acc_ref)

    def finish(i):
        par = i % acc_ref.shape[0]
        o_t = jnp.concatenate([acc_ref[par, hh, 0:HEAD_DIM] * (1.0 / acc_ref[par, hh, HEAD_DIM:HEAD_DIM + 1])
                               for hh in range(2)], axis=0)
        o_ref[0, i] = (o_t * gt_ref[0, i].astype(F32)).astype(BF16)

    _sweep(tab_ref, 2, acc_ref.shape[0], score_map, soft, value_map, finish)


def _moba(q_split, km, vt, gt, kmean, bias, offs, tab):
    B, S, width = km.shape
    pairs = width // LANES
    nt = S // ATT_TILE
    nblk = kmean.shape[1]
    unroll = _sweep_unroll(nt)
    ring = _ring_slots(unroll)
    seq_spec = lambda lane_block0: pl.BlockSpec((1, S, LANES), lambda b, p: (b, 0, lane_block0 + p))
    t_spec = pl.BlockSpec((1, nblk, LANES, ATT_TILE), lambda b, p: (b, 0, p, 0))
    return pl.pallas_call(
        _moba_kernel,
        grid=(B, pairs),
        in_specs=[pl.BlockSpec(memory_space=pltpu.SMEM),
                  seq_spec(0), seq_spec(pairs), seq_spec(0), t_spec, t_spec,
                  pl.BlockSpec((1, nblk, LANES), lambda b, p: (b, 0, p)),
                  pl.BlockSpec((2, 2, ATT_TILE, ATT_TILE), lambda b, p: (p, 0, 0, 0)),
                  pl.BlockSpec((2, nt, ATT_TILE), lambda b, p: (p, 0, 0))],
        out_specs=t_spec,
        out_shape=jax.ShapeDtypeStruct((B, nt, width, ATT_TILE), BF16),
        scratch_shapes=[pltpu.VMEM((nt, 2, nblk, ATT_TILE), F32),
                        pltpu.VMEM((ring, 2, 2, ATT_TILE, ATT_TILE), BF16),
                        pltpu.VMEM((ring, 2, 2 * ATT_TILE, ATT_TILE), BF16),
                        pltpu.VMEM((unroll, 2, HEAD_DIM + BF16_ROWS, ATT_TILE), F32)],
        compiler_params=pltpu.CompilerParams(dimension_semantics=("arbitrary", "arbitrary")),
        name="moba_attn",
    )(tab, q_split, q_split, km, vt, gt, kmean, bias, offs)


def _diff_kernel(tab_ref, qa_ref, qb_ref, k_ref, vt_ref, gt_ref, bias_ref, offs_ref, lamv_ref, subw_ref,
                 o_ref, s_ref, p_ref, acc_ref, *, lambda_init):
    q_refs = (qa_ref, qb_ref)
    acc_ref[...] = jnp.zeros(acc_ref.shape, F32)
    lv = lamv_ref[...]
    lam = (jnp.exp(jnp.sum(lv[0:1] * lv[1:2], axis=-1, keepdims=True))
           - jnp.exp(jnp.sum(lv[2:3] * lv[3:4], axis=-1, keepdims=True)) + lambda_init)

    def score_map(i, p, slot, c):
        return _score_map(i, p, slot, c, q_refs[c], k_ref, lambda d: bias_ref[0, d], s_ref)

    def soft(i, p, slot, colmax, m):
        offs = []
        for u in range(2):
            j, _, is_pad = _item(i, 2 * p + u)
            off = offs_ref[0, pl.ds(i - j, 1), :] + _pad_off(is_pad)
            offs.append([off, off])
        return _soft(p, slot, colmax, offs, m, s_ref, p_ref)

    def value_map(i, p, slot, c, alpha_c):
        vts = [vt_ref[0, _item(i, 2 * p + u)[0]] for u in range(2)]
        _value_map(i, slot, c, alpha_c, vts, p_ref, acc_ref)

    def finish(i):
        par = i % acc_ref.shape[0]
        o = [acc_ref[par, c, 0:LANES] * (1.0 / acc_ref[par, c, LANES:LANES + 1]) for c in range(2)]
        o_t = o[0] - lam * o[1]
        ms = jnp.mean(o_t * o_t, axis=0, keepdims=True)
        o_n = o_t * lax.rsqrt(ms + NORM_EPS) * subw_ref[...] * (1.0 - lambda_init)
        o_ref[0, i] = (o_n * gt_ref[0, i].astype(F32)).astype(BF16)

    _sweep(tab_ref, 2, acc_ref.shape[0], score_map, soft, value_map, finish)


def _diff(q_split, dk, vt, gt, bias, offs, lamv, subw, lambda_init, v_row_block0, tab):
    B, S, width = dk.shape
    heads = width // LANES
    nt = S // ATT_TILE
    nblk = vt.shape[1]
    unroll = _sweep_unroll(nt)
    ring = _ring_slots(unroll)
    seq_spec = lambda lane_block0: pl.BlockSpec((1, S, LANES), lambda b, h: (b, 0, lane_block0 + h))
    t_spec = lambda row_block0: pl.BlockSpec((1, nblk, LANES, ATT_TILE), lambda b, h: (b, 0, row_block0 + h, 0))
    return pl.pallas_call(
        functools.partial(_diff_kernel, lambda_init=lambda_init),
        grid=(B, heads),
        in_specs=[pl.BlockSpec(memory_space=pltpu.SMEM),
                  seq_spec(0), seq_spec(heads), seq_spec(0), t_spec(v_row_block0), t_spec(v_row_block0),
                  pl.BlockSpec((1, 2, ATT_TILE, ATT_TILE), lambda b, h: (h, 0, 0, 0)),
                  pl.BlockSpec((1, nt, ATT_TILE), lambda b, h: (h, 0, 0)),
                  pl.BlockSpec(lamv.shape, lambda b, h: (0, 0)),
                  pl.BlockSpec((LANES, ATT_TILE), lambda b, h: (0, 0))],
        out_specs=t_spec(0),
        out_shape=jax.ShapeDtypeStruct((B, nt, width, ATT_TILE), BF16),
        scratch_shapes=[pltpu.VMEM((ring, 2, 2, ATT_TILE, ATT_TILE), BF16),
                        pltpu.VMEM((ring, 2, 2 * ATT_TILE, ATT_TILE), BF16),
                        pltpu.VMEM((unroll, 2, LANES + BF16_ROWS, ATT_TILE), F32)],
        compiler_params=pltpu.CompilerParams(dimension_semantics=("arbitrary", "arbitrary")),
        name="diff_attn",
    )(tab, q_split, q_split, dk, vt, gt, bias, offs, lamv, subw)


def _outproj_kernel(x_ref, m_ref, d_ref, w_ref, o_ref):
    o_ref[0] = _merge_rows(x_ref, m_ref, d_ref, w_ref)


def _outproj(x, m_att, d_att, w_out):
    B, S, D = x.shape
    rows = min(PROJ_ROWS, S)
    tiles = rows // ATT_TILE
    row_spec = pl.BlockSpec((1, rows, D), lambda b, t: (b, t, 0))
    return pl.pallas_call(
        _outproj_kernel,
        grid=(B, S // rows),
        in_specs=[row_spec, t_spec_of(m_att, tiles), t_spec_of(d_att, tiles),
                  pl.BlockSpec(w_out.shape, lambda b, t: (0, 0))],
        out_specs=row_spec,
        out_shape=jax.ShapeDtypeStruct((B, S, D), F32),
        compiler_params=pltpu.CompilerParams(dimension_semantics=("arbitrary", "arbitrary")),
        name="outproj",
    )(x, m_att, d_att, w_out)


def kernel(x, norm_g, w_in, moba_q_norm, moba_k_norm, diff_q_norm, diff_k_norm,
           lambda_q1, lambda_k1, lambda_q2, lambda_k2, diff_subln, w_out):
    B, S, D = x.shape
    depth = w_in.shape[0]
    width = D // 2
    moba_heads = width // HEAD_DIM
    diff_heads = width // (2 * HEAD_DIM)
    assert S % MOBA_BLOCK == 0 and width % MXU_TILE == 0 and S // MOBA_BLOCK > 1
    nt = S // ATT_TILE
    scale = HEAD_DIM ** -0.5 * LOG2E

    m_bias, m_offs = _bias_tables(moba_heads, nt)
    d_bias, d_offs = _bias_tables(diff_heads, nt)
    r = lax.broadcasted_iota(jnp.int32, (MXU_TILE, MXU_TILE), 0) // HEAD_DIM
    c = lax.broadcasted_iota(jnp.int32, (MXU_TILE, MXU_TILE), 1) // HEAD_DIM
    gsum = (r == c).astype(BF16)
    reps = width // HEAD_DIM
    tab = jnp.asarray(_pair_table(nt))

    def layer_weights(layer):
        w = w_in[layer]
        cols = [w[:, k * width:(k + 1) * width] for k in range(8)]
        wa = jnp.concatenate([cols[0], cols[1], cols[4], cols[5]], axis=1).astype(BF16)
        wvt = jnp.concatenate([cols[2], cols[6]], axis=1).T.astype(BF16)
        wgt = jnp.concatenate([cols[3], cols[7]], axis=1).T.astype(BF16)
        tile_w = lambda v, s: (jnp.tile(v, reps) * s)[None, :].astype(F32)
        return (norm_g[layer][None, :], wa, wvt, wgt, gsum,
                tile_w(moba_q_norm[layer], scale), tile_w(moba_k_norm[layer], 1.0),
                tile_w(diff_q_norm[layer], scale), tile_w(diff_k_norm[layer], 1.0))

    proj = _inproj(x, *layer_weights(0))
    for layer in range(depth):
        qm, km, dq, dk, gt, vt, kmean = proj
        m_att = _moba(qm, km, vt, gt, kmean, m_bias, m_offs, tab)
        lambda_init = 0.8 - 0.6 * math.exp(-0.3 * layer)
        lamv = jnp.stack([lambda_q1[layer], lambda_k1[layer], lambda_q2[layer], lambda_k2[layer]])
        subw = jnp.broadcast_to(diff_subln[layer][:, None], (2 * HEAD_DIM, ATT_TILE)).astype(F32)
        d_att = _diff(dq, dk, vt, gt, d_bias, d_offs, lamv, subw, lambda_init, width // LANES, tab)
        wo = w_out[layer].astype(BF16)
        if layer + 1 < depth:
            x, proj = _inproj(x, *layer_weights(layer + 1), merge=(m_att, d_att, wo))
        else:
            x = _outproj(x, m_att, d_att, wo)
    return x
```

```python
import functools
import math

import numpy as np
import jax
import jax.numpy as jnp
from jax import lax
from jax.experimental import pallas as pl
from jax.experimental.pallas import tpu as pltpu

HEAD_DIM = 64
MOBA_BLOCK = 256
MOBA_TOPK = 3
NORM_EPS = 1e-6
NEG_INF = -1e30

LANES = 128
MXU_TILE = 256
ATT_TILE = MOBA_BLOCK
PROJ_ROWS = 1024
LOG2E = math.log2(math.e)
BF16_ROWS = 16
SWEEP_UNROLLS = (12, 8, 4, 2)
MAX_RING_SLOTS = 4

F32 = jnp.float32
BF16 = jnp.bfloat16


def _dot(a, b):
    return jnp.dot(a, b, preferred_element_type=F32)


def _dot_nt(a, b):
    return lax.dot_general(a, b, (((1,), (1,)), ((), ())), preferred_element_type=F32)


def _proj_rows(seq):
    return max(r for r in range(MOBA_BLOCK, PROJ_ROWS + 1, MOBA_BLOCK) if seq % r == 0)


def _alibi_slopes(n_heads):
    return np.asarray(2.0 ** (-8.0 * np.arange(1, n_heads + 1) / n_heads), dtype=np.float32)


def _bias_tables(n_heads, n_tiles):
    slopes = _alibi_slopes(n_heads) * np.float32(LOG2E)
    k_loc = np.arange(ATT_TILE, dtype=np.int32)[:, None]
    q_loc = np.arange(ATT_TILE, dtype=np.int32)[None, :]
    rel = (k_loc - q_loc).astype(np.float32)
    off = slopes[:, None, None] * rel[None]
    diag = np.where((k_loc <= q_loc)[None], off, np.float32(NEG_INF))
    tiles = np.stack([off, diag], axis=1).astype(np.float32)
    d = np.arange(n_tiles, dtype=np.float32)
    offs = -slopes[:, None] * (ATT_TILE * d)[None, :]
    offs = np.broadcast_to(offs[:, :, None], (n_heads, n_tiles, ATT_TILE)).astype(np.float32)
    return jnp.asarray(tiles), jnp.asarray(offs)


def _project_rows(x, t, g_ref, wa_ref, wvt_ref, wgt_ref, gsum_ref, wqm_ref, wkm_ref, wqd_ref, wkd_ref,
                  qm_ref, km_ref, dq_ref, dk_ref, gt_ref, vt_ref, kmean_ref, width):
    rows = x.shape[0]
    blocks = rows // MOBA_BLOCK
    ms = jnp.mean(x * x, axis=-1, keepdims=True)
    h = (x * lax.rsqrt(ms + NORM_EPS) * g_ref[...]).astype(BF16)

    gsum = gsum_ref[...]
    low_half = lax.broadcasted_iota(jnp.int32, (rows, width), 1) % LANES < HEAD_DIM

    def head_norm(y, w):
        sq = (y * y).astype(BF16)
        parts = [_dot(sq[:, c * MXU_TILE:(c + 1) * MXU_TILE], gsum) for c in range(width // MXU_TILE)]
        ssq = jnp.concatenate(parts, axis=1)
        return y * lax.rsqrt(ssq * (1.0 / HEAD_DIM) + NORM_EPS) * w

    def store_split(ref, q):
        ref[0, :, 0:width] = jnp.where(low_half, q, 0.0).astype(BF16)
        ref[0, :, width:2 * width] = jnp.where(low_half, 0.0, q).astype(BF16)

    def silu(y):
        return y * (1.0 / (1.0 + jnp.exp(-y)))

    y = _dot(h, wa_ref[:, 0 * width:1 * width])
    store_split(qm_ref, head_norm(y, wqm_ref[...]))

    y = _dot(h, wa_ref[:, 1 * width:2 * width])
    kn = head_norm(y, wkm_ref[...])
    km_ref[0] = kn.astype(BF16)
    for s in range(blocks):
        kmean_ref[0, pl.ds(t * blocks + s, 1), :] = jnp.mean(
            kn[s * MOBA_BLOCK:(s + 1) * MOBA_BLOCK], axis=0, keepdims=True)

    y = _dot(h, wa_ref[:, 2 * width:3 * width])
    store_split(dq_ref, head_norm(y, wqd_ref[...]))

    y = _dot(h, wa_ref[:, 3 * width:4 * width])
    dk_ref[0] = head_norm(y, wkd_ref[...]).astype(BF16)

    vt = _dot_nt(wvt_ref[...], h)
    gt = silu(_dot_nt(wgt_ref[...], h))
    for s in range(blocks):
        vt_ref[0, s] = vt[:, s * MOBA_BLOCK:(s + 1) * MOBA_BLOCK].astype(BF16)
        gt_ref[0, s] = gt[:, s * MOBA_BLOCK:(s + 1) * MOBA_BLOCK].astype(BF16)


def _merge_rows(x_ref, m_ref, d_ref, w_ref):
    parts = []
    for t in range(m_ref.shape[1]):
        mixed_t = jnp.concatenate([m_ref[0, t], d_ref[0, t]], axis=0)
        y = lax.dot_general(mixed_t, w_ref[...], (((0,), (0,)), ((), ())), preferred_element_type=F32)
        parts.append(x_ref[0, t * ATT_TILE:(t + 1) * ATT_TILE, :] + y)
    return jnp.concatenate(parts, axis=0)


def _inproj_kernel(x_ref, *refs, width):
    _project_rows(x_ref[0], pl.program_id(1), *refs, width)


def _mid_kernel(x_ref, m_ref, d_ref, wo_ref, *refs, width):
    xo_ref, out_refs = refs[9], refs[10:]
    x_new = _merge_rows(x_ref, m_ref, d_ref, wo_ref)
    xo_ref[0] = x_new
    _project_rows(x_new, pl.program_id(1), *refs[:9], *out_refs, width)


def _inproj(x, g, wa, wvt, wgt, gsum, wqm, wkm, wqd, wkd, merge=None):
    B, S, D = x.shape
    width = wa.shape[1] // 4
    rows = _proj_rows(S)
    nblk = S // MOBA_BLOCK
    blocks = rows // MOBA_BLOCK
    full = lambda shape: pl.BlockSpec(shape, lambda b, t: (0,) * len(shape))
    row_spec = lambda w: pl.BlockSpec((1, rows, w), lambda b, t: (b, t, 0))
    act = lambda w: jax.ShapeDtypeStruct((B, S, w), BF16)
    t_spec = pl.BlockSpec((1, blocks, 2 * width, MOBA_BLOCK), lambda b, t: (b, t, 0, 0))
    t_shape = jax.ShapeDtypeStruct((B, nblk, 2 * width, MOBA_BLOCK), BF16)
    weights = (g, wa, wvt, wgt, gsum, wqm, wkm, wqd, wkd)
    in_specs = [row_spec(D)] + [full(w.shape) for w in weights]
    out_specs = [row_spec(2 * width), row_spec(width), row_spec(2 * width), row_spec(width),
                 t_spec, t_spec, pl.BlockSpec((1, nblk, width), lambda b, t: (b, 0, 0))]
    out_shape = [act(2 * width), act(width), act(2 * width), act(width), t_shape, t_shape,
                 jax.ShapeDtypeStruct((B, nblk, width), F32)]
    params = pltpu.CompilerParams(dimension_semantics=("arbitrary", "arbitrary"))
    if merge is None:
        return pl.pallas_call(
            functools.partial(_inproj_kernel, width=width), grid=(B, S // rows),
            in_specs=in_specs, out_specs=out_specs, out_shape=out_shape,
            compiler_params=params, name="inproj")(x, *weights)
    m_att, d_att, w_out = merge
    in_specs = [row_spec(D), t_spec_of(m_att, blocks), t_spec_of(d_att, blocks), full(w_out.shape)] + in_specs[1:]
    outs = pl.pallas_call(
        functools.partial(_mid_kernel, width=width), grid=(B, S // rows),
        in_specs=in_specs, out_specs=[row_spec(D)] + out_specs,
        out_shape=[jax.ShapeDtypeStruct((B, S, D), F32)] + out_shape,
        compiler_params=params, name="outproj_inproj")(x, m_att, d_att, w_out, *weights)
    return outs[0], outs[1:]


def t_spec_of(a, tiles):
    return pl.BlockSpec((1, tiles, a.shape[2], ATT_TILE), lambda b, t: (b, t, 0, 0))


def _pair_table(n_tiles):
    rows = [(i, p, int(p == i // 2)) for i in range(n_tiles) for p in range(i // 2 + 1)]
    rows += [(rows[-1][0], rows[-1][1], 0)] * 2
    return np.ascontiguousarray(np.asarray(rows, dtype=np.int32).T)


def _item(i, t):
    t = jnp.asarray(t, jnp.int32)
    is_diag = t == 0
    j = jnp.where(is_diag, i, t - 1)
    is_pad = jnp.logical_and(t > 0, t - 1 >= i)
    return j, is_diag, is_pad


def _rows(i):
    return pl.ds(pl.multiple_of(i * ATT_TILE, ATT_TILE), ATT_TILE)


def _pad_off(is_pad):
    return jnp.where(is_pad, jnp.float32(NEG_INF), jnp.float32(0.0))


def _sweep_unroll(n_tiles):
    n_pairs = _pair_table(n_tiles).shape[1] - 2
    return next(u for u in SWEEP_UNROLLS if n_pairs % u == 0)


def _ring_slots(unroll):
    return min(unroll, MAX_RING_SLOTS)


def _sweep(tab_ref, n_maps, unroll, score_map, soft, value_map, finish):
    n_pairs = tab_ref.shape[1] - 2
    ring = _ring_slots(unroll)
    assert n_pairs % unroll == 0 and unroll % ring == 0
    pair = lambda g: (tab_ref[0, g], tab_ref[1, g])

    def score_all(g, slot):
        return tuple(x for c in range(n_maps) for x in score_map(*pair(g), slot, c))

    def half_step(g, k, carry):
        colmax, m, alpha = carry
        colmax1 = []
        for c in range(n_maps):
            colmax1 += score_map(*pair(g), k, c)
            value_map(*pair(g - 2), (k - 2) % ring, c, alpha[c])
        m1, alpha1 = soft(*pair(g - 1), (k - 1) % ring, colmax, m)
        return tuple(colmax1), m1, alpha1

    colmax = score_all(0, 0)
    m_init = tuple(jnp.full((1, ATT_TILE), NEG_INF, F32) for _ in range(n_maps))
    m, alpha = soft(*pair(0), 0, colmax, m_init)
    carry = (score_all(1, 1), m, alpha)

    def step(h, carry):
        g = 2 + unroll * h
        for k in range(unroll):
            carry = half_step(g + k, (2 + k) % ring, carry)
        for k in range(unroll):
            @pl.when(tab_ref[2, g + k - 2] == 1)
            def _(k=k):
                finish(tab_ref[0, g + k - 2])
        return carry

    lax.fori_loop(0, n_pairs // unroll, step, carry)


def _score_map(i, p, slot, c, q_ref, k_ref, bias_of, s_ref):
    q = q_ref[0, _rows(i), :]
    items = [_item(i, 2 * p + u) for u in range(2)]
    kt = jnp.concatenate([k_ref[0, _rows(j), :] for j, _, _ in items], axis=0)
    s2 = _dot_nt(kt, q)
    colmax = []
    for u, (_, is_diag, _) in enumerate(items):
        s = s2[u * ATT_TILE:(u + 1) * ATT_TILE] + bias_of(is_diag.astype(jnp.int32))
        s = s.astype(BF16)
        s_ref[slot, u, c] = s
        colmax.append(jnp.max(s, axis=0, keepdims=True).astype(F32))
    return colmax


def _soft(p, slot, colmax, offs, m, s_ref, p_ref):
    m_out, alpha_out = [], []
    for c in range(len(m)):
        m_old = jnp.where(p == 0, jnp.float32(NEG_INF), m[c])
        m_new = jnp.maximum(m_old, jnp.maximum(colmax[2 * c] + offs[0][c], colmax[2 * c + 1] + offs[1][c]))
        for u in range(2):
            shift = (m_new - offs[u][c]).astype(BF16)
            p_ref[slot, c, u * ATT_TILE:(u + 1) * ATT_TILE, :] = jnp.exp2(s_ref[slot, u, c] - shift)
        m_out.append(m_new)
        alpha_out.append(jnp.exp2(m_old - m_new))
    return tuple(m_out), tuple(alpha_out)


def _value_map(i, slot, c, alpha_c, vts, p_ref, acc_ref):
    vt = jnp.concatenate(vts, axis=1)
    vt1 = jnp.concatenate([vt, jnp.ones((BF16_ROWS, vt.shape[1]), BF16)], axis=0)
    par = i % acc_ref.shape[0]
    acc_ref[par, c] = alpha_c * acc_ref[par, c] + _dot(vt1, p_ref[slot, c])


def _moba_kernel(tab_ref, qa_ref, qb_ref, k_ref, vt_ref, gt_ref, kmean_ref, bias_ref, offs_ref, o_ref,
                 sel_ref, s_ref, p_ref, acc_ref):
    nblk = kmean_ref.shape[1]
    n_tiles = k_ref.shape[1] // ATT_TILE
    q_refs = (qa_ref, qb_ref)
    acc_ref[...] = jnp.zeros(acc_ref.shape, F32)

    km = kmean_ref[0]
    km_hi = km.astype(BF16)
    km_lo = (km - km_hi.astype(F32)).astype(BF16)
    blk = lax.broadcasted_iota(jnp.int32, (nblk, ATT_TILE), 0)
    for i in range(n_tiles):
        valid = blk < i
        for hh in range(2):
            q = q_refs[hh][0, i * ATT_TILE:(i + 1) * ATT_TILE, :]
            g = _dot_nt(km_hi, q) + _dot_nt(km_lo, q)
            g = jnp.where(valid, g, -jnp.inf)
            sel = blk == i
            for _ in range(min(MOBA_TOPK, nblk - 1)):
                mx = jnp.max(g, axis=0, keepdims=True)
                first = jnp.min(jnp.where(g == mx, blk, nblk), axis=0, keepdims=True)
                pick = blk == first
                sel = jnp.logical_or(sel, jnp.logical_and(pick, valid))
                g = jnp.where(pick, -jnp.inf, g)
            sel_ref[i, hh] = jnp.where(sel, 0.0, NEG_INF).astype(F32)

    def score_map(i, p, slot, hh):
        return _score_map(i, p, slot, hh, q_refs[hh], k_ref, lambda d: bias_ref[hh, d], s_ref)

    def soft(i, p, slot, colmax, m):
        offs = []
        for u in range(2):
            j, _, is_pad = _item(i, 2 * p + u)
            pad = _pad_off(is_pad)
            offs.append([offs_ref[hh, pl.ds(i - j, 1), :] + sel_ref[i, hh, pl.ds(j, 1), :] + pad
                         for hh in range(2)])
        return _soft(p, slot, colmax, offs, m, s_ref, p_ref)

    def value_map(i, p, slot, hh, alpha_c):
        vts = []
        for u in range(2):
            j, _, _ = _item(i, 2 * p + u)
            vts.append(vt_ref[0, j, hh * HEAD_DIM:(hh + 1) * HEAD_DIM, :])
        _value_map(i, slot, hh, alpha_c, vts, p_ref, acc_ref)

    def finish(i):
        par = i % acc_ref.shape[0]
        o_t = jnp.concatenate([acc_ref[par, hh, 0:HEAD_DIM] * (1.0 / acc_ref[par, hh, HEAD_DIM:HEAD_DIM + 1])
                               for hh in range(2)], axis=0)
        o_ref[0, i] = (o_t * gt_ref[0, i].astype(F32)).astype(BF16)

    _sweep(tab_ref, 2, acc_ref.shape[0], score_map, soft, value_map, finish)


def _moba(q_split, km, vt, gt, kmean, bias, offs, tab):
    B, S, width = km.shape
    pairs = width // LANES
    nt = S // ATT_TILE
    nblk = kmean.shape[1]
    unroll = _sweep_unroll(nt)
    ring = _ring_slots(unroll)
    seq_spec = lambda lane_block0: pl.BlockSpec((1, S, LANES), lambda b, p: (b, 0, lane_block0 + p))
    t_spec = pl.BlockSpec((1, nblk, LANES, ATT_TILE), lambda b, p: (b, 0, p, 0))
    return pl.pallas_call(
        _moba_kernel,
        grid=(B, pairs),
        in_specs=[pl.BlockSpec(memory_space=pltpu.SMEM),
                  seq_spec(0), seq_spec(pairs), seq_spec(0), t_spec, t_spec,
                  pl.BlockSpec((1, nblk, LANES), lambda b, p: (b, 0, p)),
                  pl.BlockSpec((2, 2, ATT_TILE, ATT_TILE), lambda b, p: (p, 0, 0, 0)),
                  pl.BlockSpec((2, nt, ATT_TILE), lambda b, p: (p, 0, 0))],
        out_specs=t_spec,
        out_shape=jax.ShapeDtypeStruct((B, nt, width, ATT_TILE), BF16),
        scratch_shapes=[pltpu.VMEM((nt, 2, nblk, ATT_TILE), F32),
                        pltpu.VMEM((ring, 2, 2, ATT_TILE, ATT_TILE), BF16),
                        pltpu.VMEM((ring, 2, 2 * ATT_TILE, ATT_TILE), BF16),
                        pltpu.VMEM((unroll, 2, HEAD_DIM + BF16_ROWS, ATT_TILE), F32)],
        compiler_params=pltpu.CompilerParams(dimension_semantics=("arbitrary", "arbitrary")),
        name="moba_attn",
    )(tab, q_split, q_split, km, vt, gt, kmean, bias, offs)


def _diff_kernel(tab_ref, qa_ref, qb_ref, k_ref, vt_ref, gt_ref, bias_ref, offs_ref, lamv_ref, subw_ref,
                 o_ref, s_ref, p_ref, acc_ref, *, lambda_init):
    q_refs = (qa_ref, qb_ref)
    acc_ref[...] = jnp.zeros(acc_ref.shape, F32)
    lv = lamv_ref[...]
    lam = (jnp.exp(jnp.sum(lv[0:1] * lv[1:2], axis=-1, keepdims=True))
           - jnp.exp(jnp.sum(lv[2:3] * lv[3:4], axis=-1, keepdims=True)) + lambda_init)

    def score_map(i, p, slot, c):
        return _score_map(i, p, slot, c, q_refs[c], k_ref, lambda d: bias_ref[0, d], s_ref)

    def soft(i, p, slot, colmax, m):
        offs = []
        for u in range(2):
            j, _, is_pad = _item(i, 2 * p + u)
            off = offs_ref[0, pl.ds(i - j, 1), :] + _pad_off(is_pad)
            offs.append([off, off])
        return _soft(p, slot, colmax, offs, m, s_ref, p_ref)

    def value_map(i, p, slot, c, alpha_c):
        vts = [vt_ref[0, _item(i, 2 * p + u)[0]] for u in range(2)]
        _value_map(i, slot, c, alpha_c, vts, p_ref, acc_ref)

    def finish(i):
        par = i % acc_ref.shape[0]
        o = [acc_ref[par, c, 0:LANES] * (1.0 / acc_ref[par, c, LANES:LANES + 1]) for c in range(2)]
        o_t = o[0] - lam * o[1]
        ms = jnp.mean(o_t * o_t, axis=0, keepdims=True)
        o_n = o_t * lax.rsqrt(ms + NORM_EPS) * subw_ref[...]
        o_ref[0, i] = (o_n * gt_ref[0, i].astype(F32)).astype(BF16)

    _sweep(tab_ref, 2, acc_ref.shape[0], score_map, soft, value_map, finish)


def _diff(q_split, dk, vt, gt, bias, offs, lamv, subw, lambda_init, v_row_block0, tab):
    B, S, width = dk.shape
    heads = width // LANES
    nt = S // ATT_TILE
    nblk = vt.shape[1]
    unroll = _sweep_unroll(nt)
    ring = _ring_slots(unroll)
    seq_spec = lambda lane_block0: pl.BlockSpec((1, S, LANES), lambda b, h: (b, 0, lane_block0 + h))
    t_spec = lambda row_block0: pl.BlockSpec((1, nblk, LANES, ATT_TILE), lambda b, h: (b, 0, row_block0 + h, 0))
    return pl.pallas_call(
        functools.partial(_diff_kernel, lambda_init=lambda_init),
        grid=(B, heads),
        in_specs=[pl.BlockSpec(memory_space=pltpu.SMEM),
                  seq_spec(0), seq_spec(heads), seq_spec(0), t_spec(v_row_block0), t_spec(v_row_block0),
                  pl.BlockSpec((1, 2, ATT_TILE, ATT_TILE), lambda b, h: (h, 0, 0, 0)),
                  pl.BlockSpec((1, nt, ATT_TILE), lambda b, h: (h, 0, 0)),
                  pl.BlockSpec(lamv.shape, lambda b, h: (0, 0)),
                  pl.BlockSpec((LANES, ATT_TILE), lambda b, h: (0, 0))],
        out_specs=t_spec(0),
        out_shape=jax.ShapeDtypeStruct((B, nt, width, ATT_TILE), BF16),
        scratch_shapes=[pltpu.VMEM((ring, 2, 2, ATT_TILE, ATT_TILE), BF16),
                        pltpu.VMEM((ring, 2, 2 * ATT_TILE, ATT_TILE), BF16),
                        pltpu.VMEM((unroll, 2, LANES + BF16_ROWS, ATT_TILE), F32)],
        compiler_params=pltpu.CompilerParams(dimension_semantics=("arbitrary", "arbitrary")),
        name="diff_attn",
    )(tab, q_split, q_split, dk, vt, gt, bias, offs, lamv, subw)


def _outproj_kernel(x_ref, m_ref, d_ref, w_ref, o_ref):
    o_ref[0] = _merge_rows(x_ref, m_ref, d_ref, w_ref)


def _outproj(x, m_att, d_att, w_out):
    B, S, D = x.shape
    rows = _proj_rows(S)
    tiles = rows // ATT_TILE
    row_spec = pl.BlockSpec((1, rows, D), lambda b, t: (b, t, 0))
    return pl.pallas_call(
        _outproj_kernel,
        grid=(B, S // rows),
        in_specs=[row_spec, t_spec_of(m_att, tiles), t_spec_of(d_att, tiles),
                  pl.BlockSpec(w_out.shape, lambda b, t: (0, 0))],
        out_specs=row_spec,
        out_shape=jax.ShapeDtypeStruct((B, S, D), F32),
        compiler_params=pltpu.CompilerParams(dimension_semantics=("arbitrary", "arbitrary")),
        name="outproj",
    )(x, m_att, d_att, w_out)


def kernel(x, norm_g, w_in, moba_q_norm, moba_k_norm, diff_q_norm, diff_k_norm,
           lambda_q1, lambda_k1, lambda_q2, lambda_k2, diff_subln, w_out):
    B, S, D = x.shape
    depth = w_in.shape[0]
    width = D // 2
    moba_heads = width // HEAD_DIM
    diff_heads = width // (2 * HEAD_DIM)
    assert S % MOBA_BLOCK == 0 and width % MXU_TILE == 0 and S // MOBA_BLOCK > 1
    nt = S // ATT_TILE
    scale = HEAD_DIM ** -0.5 * LOG2E

    m_bias, m_offs = _bias_tables(moba_heads, nt)
    d_bias, d_offs = _bias_tables(diff_heads, nt)
    r = lax.broadcasted_iota(jnp.int32, (MXU_TILE, MXU_TILE), 0) // HEAD_DIM
    c = lax.broadcasted_iota(jnp.int32, (MXU_TILE, MXU_TILE), 1) // HEAD_DIM
    gsum = (r == c).astype(BF16)
    reps = width // HEAD_DIM
    tab = jnp.asarray(_pair_table(nt))

    def layer_weights(layer):
        w = w_in[layer]
        cols = [w[:, k * width:(k + 1) * width] for k in range(8)]
        wa = jnp.concatenate([cols[0], cols[1], cols[4], cols[5]], axis=1).astype(BF16)
        wvt = jnp.concatenate([cols[2], cols[6]], axis=1).astype(BF16).T
        wgt = jnp.concatenate([cols[3], cols[7]], axis=1).astype(BF16).T
        tile_w = lambda v, s: (jnp.tile(v, reps) * s)[None, :].astype(F32)
        return (norm_g[layer][None, :], wa, wvt, wgt, gsum,
                tile_w(moba_q_norm[layer], scale), tile_w(moba_k_norm[layer], 1.0),
                tile_w(diff_q_norm[layer], scale), tile_w(diff_k_norm[layer], 1.0))

    proj = _inproj(x, *layer_weights(0))
    for layer in range(depth):
        qm, km, dq, dk, gt, vt, kmean = proj
        m_att = _moba(qm, km, vt, gt, kmean, m_bias, m_offs, tab)
        lambda_init = 0.8 - 0.6 * math.exp(-0.3 * layer)
        lamv = jnp.stack([lambda_q1[layer], lambda_k1[layer], lambda_q2[layer], lambda_k2[layer]])
        subw = jnp.broadcast_to((diff_subln[layer] * (1.0 - lambda_init))[:, None],
                                (2 * HEAD_DIM, ATT_TILE)).astype(F32)
        d_att = _diff(dq, dk, vt, gt, d_bias, d_offs, lamv, subw, lambda_init, width // LANES, tab)
        wo = w_out[layer].astype(BF16)
        if layer + 1 < depth:
            x, proj = _inproj(x, *layer_weights(layer + 1), merge=(m_att, d_att, wo))
        else:
            x = _outproj(x, m_att, d_att, wo)
    return x
```

```python
import functools
import math

import numpy as np
import jax
import jax.numpy as jnp
from jax import lax
from jax.experimental import pallas as pl
from jax.experimental.pallas import tpu as pltpu

HEAD_DIM = 64
MOBA_BLOCK = 256
MOBA_TOPK = 3
NORM_EPS = 1e-6
NEG_INF = -1e30

LANES = 128
MXU_TILE = 256
ATT_TILE = MOBA_BLOCK
PROJ_ROWS = 1024
LOG2E = math.log2(math.e)
BF16_ROWS = 16
SWEEP_UNROLLS = (8, 4, 2)
MAX_RING_SLOTS = 4

F32 = jnp.float32
BF16 = jnp.bfloat16


def _dot(a, b):
    return jnp.dot(a, b, preferred_element_type=F32)


def _dot_nt(a, b):
    return lax.dot_general(a, b, (((1,), (1,)), ((), ())), preferred_element_type=F32)


def _proj_rows(seq):
    return max(r for r in range(MOBA_BLOCK, PROJ_ROWS + 1, MOBA_BLOCK) if seq % r == 0)


def _alibi_slopes(n_heads):
    return np.asarray(2.0 ** (-8.0 * np.arange(1, n_heads + 1) / n_heads), dtype=np.float32)


def _bias_tables(n_heads, n_tiles):
    slopes = _alibi_slopes(n_heads) * np.float32(LOG2E)
    k_loc = np.arange(ATT_TILE, dtype=np.int32)[:, None]
    q_loc = np.arange(ATT_TILE, dtype=np.int32)[None, :]
    rel = (k_loc - q_loc).astype(np.float32)
    off = slopes[:, None, None] * rel[None]
    diag = np.where((k_loc <= q_loc)[None], off, np.float32(NEG_INF))
    tiles = np.stack([off, diag], axis=1).astype(np.float32)
    d = np.arange(n_tiles, dtype=np.float32)
    offs = -slopes[:, None] * (ATT_TILE * d)[None, :]
    offs = np.broadcast_to(offs[:, :, None], (n_heads, n_tiles, ATT_TILE)).astype(np.float32)
    return jnp.asarray(tiles), jnp.asarray(offs)


def _project_rows(x, t, g_ref, wa_ref, wvt_ref, wgt_ref, gsum_ref, wqm_ref, wkm_ref, wqd_ref, wkd_ref,
                  qm_ref, km_ref, dq_ref, dk_ref, gt_ref, vt_ref, kmean_ref, width):
    rows = x.shape[0]
    blocks = rows // MOBA_BLOCK
    ms = jnp.mean(x * x, axis=-1, keepdims=True)
    h = (x * lax.rsqrt(ms + NORM_EPS) * g_ref[...]).astype(BF16)

    gsum = gsum_ref[...]
    low_half = lax.broadcasted_iota(jnp.int32, (rows, width), 1) % LANES < HEAD_DIM

    def head_norm(y, w):
        sq = (y * y).astype(BF16)
        parts = [_dot(sq[:, c * MXU_TILE:(c + 1) * MXU_TILE], gsum) for c in range(width // MXU_TILE)]
        ssq = jnp.concatenate(parts, axis=1)
        return y * lax.rsqrt(ssq * (1.0 / HEAD_DIM) + NORM_EPS) * w

    def store_split(ref, q):
        ref[0, :, 0:width] = jnp.where(low_half, q, 0.0).astype(BF16)
        ref[0, :, width:2 * width] = jnp.where(low_half, 0.0, q).astype(BF16)

    def silu(y):
        return y * (1.0 / (1.0 + jnp.exp(-y)))

    y = _dot(h, wa_ref[:, 0 * width:1 * width])
    store_split(qm_ref, head_norm(y, wqm_ref[...]))

    y = _dot(h, wa_ref[:, 1 * width:2 * width])
    kn = head_norm(y, wkm_ref[...])
    km_ref[0] = kn.astype(BF16)
    for s in range(blocks):
        kmean_ref[0, pl.ds(t * blocks + s, 1), :] = jnp.mean(
            kn[s * MOBA_BLOCK:(s + 1) * MOBA_BLOCK], axis=0, keepdims=True)

    y = _dot(h, wa_ref[:, 2 * width:3 * width])
    store_split(dq_ref, head_norm(y, wqd_ref[...]))

    y = _dot(h, wa_ref[:, 3 * width:4 * width])
    dk_ref[0] = head_norm(y, wkd_ref[...]).astype(BF16)

    vt = _dot_nt(wvt_ref[...], h)
    gt = silu(_dot_nt(wgt_ref[...], h))
    for s in range(blocks):
        vt_ref[0, s] = vt[:, s * MOBA_BLOCK:(s + 1) * MOBA_BLOCK].astype(BF16)
        gt_ref[0, s] = gt[:, s * MOBA_BLOCK:(s + 1) * MOBA_BLOCK].astype(BF16)


def _merge_rows(x_ref, m_ref, d_ref, w_ref):
    parts = []
    for t in range(m_ref.shape[1]):
        mixed_t = jnp.concatenate([m_ref[0, t], d_ref[0, t]], axis=0)
        y = lax.dot_general(mixed_t, w_ref[...], (((0,), (0,)), ((), ())), preferred_element_type=F32)
        parts.append(x_ref[0, t * ATT_TILE:(t + 1) * ATT_TILE, :] + y)
    return jnp.concatenate(parts, axis=0)


def _inproj_kernel(x_ref, *refs, width):
    _project_rows(x_ref[0], pl.program_id(1), *refs, width)


def _mid_kernel(x_ref, m_ref, d_ref, wo_ref, *refs, width):
    xo_ref, out_refs = refs[9], refs[10:]
    x_new = _merge_rows(x_ref, m_ref, d_ref, wo_ref)
    xo_ref[0] = x_new
    _project_rows(x_new, pl.program_id(1), *refs[:9], *out_refs, width)


def _inproj(x, g, wa, wvt, wgt, gsum, wqm, wkm, wqd, wkd, merge=None):
    B, S, D = x.shape
    width = wa.shape[1] // 4
    rows = _proj_rows(S)
    nblk = S // MOBA_BLOCK
    blocks = rows // MOBA_BLOCK
    full = lambda shape: pl.BlockSpec(shape, lambda b, t: (0,) * len(shape))
    row_spec = lambda w: pl.BlockSpec((1, rows, w), lambda b, t: (b, t, 0))
    act = lambda w: jax.ShapeDtypeStruct((B, S, w), BF16)
    t_spec = pl.BlockSpec((1, blocks, 2 * width, MOBA_BLOCK), lambda b, t: (b, t, 0, 0))
    t_shape = jax.ShapeDtypeStruct((B, nblk, 2 * width, MOBA_BLOCK), BF16)
    weights = (g, wa, wvt, wgt, gsum, wqm, wkm, wqd, wkd)
    in_specs = [row_spec(D)] + [full(w.shape) for w in weights]
    out_specs = [row_spec(2 * width), row_spec(width), row_spec(2 * width), row_spec(width),
                 t_spec, t_spec, pl.BlockSpec((1, nblk, width), lambda b, t: (b, 0, 0))]
    out_shape = [act(2 * width), act(width), act(2 * width), act(width), t_shape, t_shape,
                 jax.ShapeDtypeStruct((B, nblk, width), F32)]
    params = pltpu.CompilerParams(dimension_semantics=("arbitrary", "arbitrary"))
    if merge is None:
        return pl.pallas_call(
            functools.partial(_inproj_kernel, width=width), grid=(B, S // rows),
            in_specs=in_specs, out_specs=out_specs, out_shape=out_shape,
            compiler_params=params, name="inproj")(x, *weights)
    m_att, d_att, w_out = merge
    in_specs = [row_spec(D), t_spec_of(m_att, blocks), t_spec_of(d_att, blocks), full(w_out.shape)] + in_specs[1:]
    outs = pl.pallas_call(
        functools.partial(_mid_kernel, width=width), grid=(B, S // rows),
        in_specs=in_specs, out_specs=[row_spec(D)] + out_specs,
        out_shape=[jax.ShapeDtypeStruct((B, S, D), F32)] + out_shape,
        compiler_params=params, name="outproj_inproj")(x, m_att, d_att, w_out, *weights)
    return outs[0], outs[1:]


def t_spec_of(a, tiles):
    return pl.BlockSpec((1, tiles, a.shape[2], ATT_TILE), lambda b, t: (b, t, 0, 0))


def _pair_table(n_tiles):
    rows = [(i, p, int(p == i // 2)) for i in range(n_tiles) for p in range(i // 2 + 1)]
    rows += [(rows[-1][0], rows[-1][1], 0)] * 2
    return np.ascontiguousarray(np.asarray(rows, dtype=np.int32).T)


def _item(i, t):
    t = jnp.asarray(t, jnp.int32)
    is_diag = t == 0
    j = jnp.where(is_diag, i, t - 1)
    is_pad = jnp.logical_and(t > 0, t - 1 >= i)
    return j, is_diag, is_pad


def _rows(i):
    return pl.ds(pl.multiple_of(i * ATT_TILE, ATT_TILE), ATT_TILE)


def _pad_off(is_pad):
    return jnp.where(is_pad, jnp.float32(NEG_INF), jnp.float32(0.0))


def _sweep_unroll(n_tiles):
    n_pairs = _pair_table(n_tiles).shape[1] - 2
    return next(u for u in SWEEP_UNROLLS if n_pairs % u == 0)


def _ring_slots(unroll):
    return min(unroll, MAX_RING_SLOTS)


def _sweep(tab_ref, n_maps, unroll, score_map, soft, value_map, finish):
    n_pairs = tab_ref.shape[1] - 2
    ring = _ring_slots(unroll)
    assert n_pairs % unroll == 0 and unroll % ring == 0
    pair = lambda g: (tab_ref[0, g], tab_ref[1, g])

    def score_all(g, slot):
        return tuple(x for c in range(n_maps) for x in score_map(*pair(g), slot, c))

    def half_step(g, k, carry):
        colmax, m, alpha = carry
        colmax1 = []
        for c in range(n_maps):
            colmax1 += score_map(*pair(g), k, c)
            value_map(*pair(g - 2), (k - 2) % ring, c, alpha[c])
        m1, alpha1 = soft(*pair(g - 1), (k - 1) % ring, colmax, m)
        return tuple(colmax1), m1, alpha1

    colmax = score_all(0, 0)
    m_init = tuple(jnp.full((1, ATT_TILE), NEG_INF, F32) for _ in range(n_maps))
    m, alpha = soft(*pair(0), 0, colmax, m_init)
    carry = (score_all(1, 1), m, alpha)

    def step(h, carry):
        g = 2 + unroll * h
        for k in range(unroll):
            carry = half_step(g + k, (2 + k) % ring, carry)
        for k in range(unroll):
            @pl.when(tab_ref[2, g + k - 2] == 1)
            def _(k=k):
                finish(tab_ref[0, g + k - 2])
        return carry

    lax.fori_loop(0, n_pairs // unroll, step, carry)


def _score_map(i, p, slot, c, q_ref, k_ref, bias_of, s_ref):
    q = q_ref[0, _rows(i), :]
    items = [_item(i, 2 * p + u) for u in range(2)]
    kt = jnp.concatenate([k_ref[0, _rows(j), :] for j, _, _ in items], axis=0)
    s2 = _dot_nt(kt, q)
    colmax = []
    for u, (_, is_diag, _) in enumerate(items):
        s = s2[u * ATT_TILE:(u + 1) * ATT_TILE] + bias_of(is_diag.astype(jnp.int32))
        s = s.astype(BF16)
        s_ref[slot, u, c] = s
        colmax.append(jnp.max(s, axis=0, keepdims=True).astype(F32))
    return colmax


def _soft(p, slot, colmax, offs, m, s_ref, p_ref):
    m_out, alpha_out = [], []
    for c in range(len(m)):
        m_old = jnp.where(p == 0, jnp.float32(NEG_INF), m[c])
        m_new = jnp.maximum(m_old, jnp.maximum(colmax[2 * c] + offs[0][c], colmax[2 * c + 1] + offs[1][c]))
        for u in range(2):
            shift = (m_new - offs[u][c]).astype(BF16)
            p_ref[slot, c, u * ATT_TILE:(u + 1) * ATT_TILE, :] = jnp.exp2(s_ref[slot, u, c] - shift)
        m_out.append(m_new)
        alpha_out.append(jnp.exp2(m_old - m_new))
    return tuple(m_out), tuple(alpha_out)


def _value_map(i, slot, c, alpha_c, vts, p_ref, acc_ref):
    vt = jnp.concatenate(vts, axis=1)
    vt1 = jnp.concatenate([vt, jnp.ones((BF16_ROWS, vt.shape[1]), BF16)], axis=0)
    par = i % acc_ref.shape[0]
    acc_ref[par, c] = alpha_c * acc_ref[par, c] + _dot(vt1, p_ref[slot, c])


def _moba_kernel(tab_ref, qa_ref, qb_ref, k_ref, vt_ref, gt_ref, kmean_ref, bias_ref, offs_ref, o_ref,
                 sel_ref, s_ref, p_ref, acc_ref):
    nblk = kmean_ref.shape[1]
    n_tiles = k_ref.shape[1] // ATT_TILE
    q_refs = (qa_ref, qb_ref)
    acc_ref[...] = jnp.zeros(acc_ref.shape, F32)

    km = kmean_ref[0]
    km_hi = km.astype(BF16)
    km_lo = (km - km_hi.astype(F32)).astype(BF16)
    blk = lax.broadcasted_iota(jnp.int32, (nblk, ATT_TILE), 0)
    for i in range(n_tiles):
        valid = blk < i
        for hh in range(2):
            q = q_refs[hh][0, i * ATT_TILE:(i + 1) * ATT_TILE, :]
            g = _dot_nt(km_hi, q) + _dot_nt(km_lo, q)
            g = jnp.where(valid, g, -jnp.inf)
            sel = blk == i
            for _ in range(min(MOBA_TOPK, nblk - 1)):
                mx = jnp.max(g, axis=0, keepdims=True)
                first = jnp.min(jnp.where(g == mx, blk, nblk), axis=0, keepdims=True)
                pick = blk == first
                sel = jnp.logical_or(sel, jnp.logical_and(pick, valid))
                g = jnp.where(pick, -jnp.inf, g)
            sel_ref[i, hh] = jnp.where(sel, 0.0, NEG_INF).astype(F32)

    def score_map(i, p, slot, hh):
        return _score_map(i, p, slot, hh, q_refs[hh], k_ref, lambda d: bias_ref[hh, d], s_ref)

    def soft(i, p, slot, colmax, m):
        offs = []
        for u in range(2):
            j, _, is_pad = _item(i, 2 * p + u)
            pad = _pad_off(is_pad)
            offs.append([offs_ref[hh, pl.ds(i - j, 1), :] + sel_ref[i, hh, pl.ds(j, 1), :] + pad
                         for hh in range(2)])
        return _soft(p, slot, colmax, offs, m, s_ref, p_ref)

    def value_map(i, p, slot, hh, alpha_c):
        vts = []
        for u in range(2):
            j, _, _ = _item(i, 2 * p + u)
            vts.append(vt_ref[0, j, hh * HEAD_DIM:(hh + 1) * HEAD_DIM, :])
        _value_map(i, slot, hh, alpha_c, vts, p_ref, acc_ref)

    def finish(i):
        par = i % acc_ref.shape[0]
        o_t = jnp.concatenate([acc_ref[par, hh, 0:HEAD_DIM] * (1.0 / acc_ref[par, hh, HEAD_DIM:HEAD_DIM + 1])
                               for hh in range(2)], axis=0)
        o_ref[0, i] = (o_t * gt_ref[0, i].astype(F32)).astype(BF16)

    _sweep(tab_ref, 2, acc_ref.shape[0], score_map, soft, value_map, finish)


def _moba(q_split, km, vt, gt, kmean, bias, offs, tab):
    B, S, width = km.shape
    pairs = width // LANES
    nt = S // ATT_TILE
    nblk = kmean.shape[1]
    unroll = _sweep_unroll(nt)
    ring = _ring_slots(unroll)
    seq_spec = lambda lane_block0: pl.BlockSpec((1, S, LANES), lambda b, p: (b, 0, lane_block0 + p))
    t_spec = pl.BlockSpec((1, nblk, LANES, ATT_TILE), lambda b, p: (b, 0, p, 0))
    return pl.pallas_call(
        _moba_kernel,
        grid=(B, pairs),
        in_specs=[pl.BlockSpec(memory_space=pltpu.SMEM),
                  seq_spec(0), seq_spec(pairs), seq_spec(0), t_spec, t_spec,
                  pl.BlockSpec((1, nblk, LANES), lambda b, p: (b, 0, p)),
                  pl.BlockSpec((2, 2, ATT_TILE, ATT_TILE), lambda b, p: (p, 0, 0, 0)),
                  pl.BlockSpec((2, nt, ATT_TILE), lambda b, p: (p, 0, 0))],
        out_specs=t_spec,
        out_shape=jax.ShapeDtypeStruct((B, nt, width, ATT_TILE), BF16),
        scratch_shapes=[pltpu.VMEM((nt, 2, nblk, ATT_TILE), F32),
                        pltpu.VMEM((ring, 2, 2, ATT_TILE, ATT_TILE), BF16),
                        pltpu.VMEM((ring, 2, 2 * ATT_TILE, ATT_TILE), BF16),
                        pltpu.VMEM((unroll, 2, HEAD_DIM + BF16_ROWS, ATT_TILE), F32)],
        compiler_params=pltpu.CompilerParams(dimension_semantics=("arbitrary", "arbitrary")),
        name="moba_attn",
    )(tab, q_split, q_split, km, vt, gt, kmean, bias, offs)


def _diff_kernel(tab_ref, qa_ref, qb_ref, k_ref, vt_ref, gt_ref, bias_ref, offs_ref, lamv_ref, subw_ref,
                 o_ref, s_ref, p_ref, acc_ref, *, lambda_init):
    q_refs = (qa_ref, qb_ref)
    acc_ref[...] = jnp.zeros(acc_ref.shape, F32)
    lv = lamv_ref[...]
    lam = (jnp.exp(jnp.sum(lv[0:1] * lv[1:2], axis=-1, keepdims=True))
           - jnp.exp(jnp.sum(lv[2:3] * lv[3:4], axis=-1, keepdims=True)) + lambda_init)

    def score_map(i, p, slot, c):
        return _score_map(i, p, slot, c, q_refs[c], k_ref, lambda d: bias_ref[0, d], s_ref)

    def soft(i, p, slot, colmax, m):
        offs = []
        for u in range(2):
            j, _, is_pad = _item(i, 2 * p + u)
            off = offs_ref[0, pl.ds(i - j, 1), :] + _pad_off(is_pad)
            offs.append([off, off])
        return _soft(p, slot, colmax, offs, m, s_ref, p_ref)

    def value_map(i, p, slot, c, alpha_c):
        vts = [vt_ref[0, _item(i, 2 * p + u)[0]] for u in range(2)]
        _value_map(i, slot, c, alpha_c, vts, p_ref, acc_ref)

    def finish(i):
        par = i % acc_ref.shape[0]
        o = [acc_ref[par, c, 0:LANES] * (1.0 / acc_ref[par, c, LANES:LANES + 1]) for c in range(2)]
        o_t = o[0] - lam * o[1]
        ms = jnp.mean(o_t * o_t, axis=0, keepdims=True)
        o_n = o_t * lax.rsqrt(ms + NORM_EPS) * subw_ref[...]
        o_ref[0, i] = (o_n * gt_ref[0, i].astype(F32)).astype(BF16)

    _sweep(tab_ref, 2, acc_ref.shape[0], score_map, soft, value_map, finish)


def _diff(q_split, dk, vt, gt, bias, offs, lamv, subw, lambda_init, v_row_block0, tab):
    B, S, width = dk.shape
    heads = width // LANES
    nt = S // ATT_TILE
    nblk = vt.shape[1]
    unroll = _sweep_unroll(nt)
    ring = _ring_slots(unroll)
    seq_spec = lambda lane_block0: pl.BlockSpec((1, S, LANES), lambda b, h: (b, 0, lane_block0 + h))
    t_spec = lambda row_block0: pl.BlockSpec((1, nblk, LANES, ATT_TILE), lambda b, h: (b, 0, row_block0 + h, 0))
    return pl.pallas_call(
        functools.partial(_diff_kernel, lambda_init=lambda_init),
        grid=(B, heads),
        in_specs=[pl.BlockSpec(memory_space=pltpu.SMEM),
                  seq_spec(0), seq_spec(heads), seq_spec(0), t_spec(v_row_block0), t_spec(v_row_block0),
                  pl.BlockSpec((1, 2, ATT_TILE, ATT_TILE), lambda b, h: (h, 0, 0, 0)),
                  pl.BlockSpec((1, nt, ATT_TILE), lambda b, h: (h, 0, 0)),
                  pl.BlockSpec(lamv.shape, lambda b, h: (0, 0)),
                  pl.BlockSpec((LANES, ATT_TILE), lambda b, h: (0, 0))],
        out_specs=t_spec(0),
        out_shape=jax.ShapeDtypeStruct((B, nt, width, ATT_TILE), BF16),
        scratch_shapes=[pltpu.VMEM((ring, 2, 2, ATT_TILE, ATT_TILE), BF16),
                        pltpu.VMEM((ring, 2, 2 * ATT_TILE, ATT_TILE), BF16),
                        pltpu.VMEM((unroll, 2, LANES + BF16_ROWS, ATT_TILE), F32)],
        compiler_params=pltpu.CompilerParams(dimension_semantics=("arbitrary", "arbitrary")),
        name="diff_attn",
    )(tab, q_split, q_split, dk, vt, gt, bias, offs, lamv, subw)


def _outproj_kernel(x_ref, m_ref, d_ref, w_ref, o_ref):
    o_ref[0] = _merge_rows(x_ref, m_ref, d_ref, w_ref)


def _outproj(x, m_att, d_att, w_out):
    B, S, D = x.shape
    rows = _proj_rows(S)
    tiles = rows // ATT_TILE
    row_spec = pl.BlockSpec((1, rows, D), lambda b, t: (b, t, 0))
    return pl.pallas_call(
        _outproj_kernel,
        grid=(B, S // rows),
        in_specs=[row_spec, t_spec_of(m_att, tiles), t_spec_of(d_att, tiles),
                  pl.BlockSpec(w_out.shape, lambda b, t: (0, 0))],
        out_specs=row_spec,
        out_shape=jax.ShapeDtypeStruct((B, S, D), F32),
        compiler_params=pltpu.CompilerParams(dimension_semantics=("arbitrary", "arbitrary")),
        name="outproj",
    )(x, m_att, d_att, w_out)


def kernel(x, norm_g, w_in, moba_q_norm, moba_k_norm, diff_q_norm, diff_k_norm,
           lambda_q1, lambda_k1, lambda_q2, lambda_k2, diff_subln, w_out):
    B, S, D = x.shape
    depth = w_in.shape[0]
    width = D // 2
    moba_heads = width // HEAD_DIM
    diff_heads = width // (2 * HEAD_DIM)
    assert S % MOBA_BLOCK == 0 and width % MXU_TILE == 0 and S // MOBA_BLOCK > 1
    nt = S // ATT_TILE
    scale = HEAD_DIM ** -0.5 * LOG2E

    m_bias, m_offs = _bias_tables(moba_heads, nt)
    d_bias, d_offs = _bias_tables(diff_heads, nt)
    r = lax.broadcasted_iota(jnp.int32, (MXU_TILE, MXU_TILE), 0) // HEAD_DIM
    c = lax.broadcasted_iota(jnp.int32, (MXU_TILE, MXU_TILE), 1) // HEAD_DIM
    gsum = (r == c).astype(BF16)
    reps = width // HEAD_DIM
    tab = jnp.asarray(_pair_table(nt))

    def layer_weights(layer):
        w = w_in[layer]
        cols = [w[:, k * width:(k + 1) * width] for k in range(8)]
        wa = jnp.concatenate([cols[0], cols[1], cols[4], cols[5]], axis=1).astype(BF16)
        wvt = jnp.concatenate([cols[2], cols[6]], axis=1).astype(BF16).T
        wgt = jnp.concatenate([cols[3], cols[7]], axis=1).astype(BF16).T
        tile_w = lambda v, s: (jnp.tile(v, reps) * s)[None, :].astype(F32)
        return (norm_g[layer][None, :], wa, wvt, wgt, gsum,
                tile_w(moba_q_norm[layer], scale), tile_w(moba_k_norm[layer], 1.0),
                tile_w(diff_q_norm[layer], scale), tile_w(diff_k_norm[layer], 1.0))

    proj = _inproj(x, *layer_weights(0))
    for layer in range(depth):
        qm, km, dq, dk, gt, vt, kmean = proj
        m_att = _moba(qm, km, vt, gt, kmean, m_bias, m_offs, tab)
        lambda_init = 0.8 - 0.6 * math.exp(-0.3 * layer)
        lamv = jnp.stack([lambda_q1[layer], lambda_k1[layer], lambda_q2[layer], lambda_k2[layer]])
        subw = jnp.broadcast_to((diff_subln[layer] * (1.0 - lambda_init))[:, None],
                                (2 * HEAD_DIM, ATT_TILE)).astype(F32)
        d_att = _diff(dq, dk, vt, gt, d_bias, d_offs, lamv, subw, lambda_init, width // LANES, tab)
        wo = w_out[layer].astype(BF16)
        if layer + 1 < depth:
            x, proj = _inproj(x, *layer_weights(layer + 1), merge=(m_att, d_att, wo))
        else:
            x = _outproj(x, m_att, d_att, wo)
    return x
```

```python
import functools
import math

import numpy as np
import jax
import jax.numpy as jnp
from jax import lax
from jax.experimental import pallas as pl
from jax.experimental.pallas import tpu as pltpu

HEAD_DIM = 64
MOBA_BLOCK = 256
MOBA_TOPK = 3
NORM_EPS = 1e-6
NEG_INF = -1e30

LANES = 128
MXU_TILE = 256
ATT_TILE = MOBA_BLOCK
PROJ_ROWS = 1024
LOG2E = math.log2(math.e)
BF16_ROWS = 16
SWEEP_UNROLLS = (8, 4, 2)
MAX_RING_SLOTS = 4

F32 = jnp.float32
BF16 = jnp.bfloat16


def _dot(a, b):
    return jnp.dot(a, b, preferred_element_type=F32)


def _dot_nt(a, b):
    return lax.dot_general(a, b, (((1,), (1,)), ((), ())), preferred_element_type=F32)


def _proj_rows(seq):
    return max(r for r in range(MOBA_BLOCK, PROJ_ROWS + 1, MOBA_BLOCK) if seq % r == 0)


def _alibi_slopes(n_heads):
    return np.asarray(2.0 ** (-8.0 * np.arange(1, n_heads + 1) / n_heads), dtype=np.float32)


def _bias_tables(n_heads, n_tiles):
    slopes = _alibi_slopes(n_heads) * np.float32(LOG2E)
    k_loc = np.arange(ATT_TILE, dtype=np.int32)[:, None]
    q_loc = np.arange(ATT_TILE, dtype=np.int32)[None, :]
    rel = (k_loc - q_loc).astype(np.float32)
    off = slopes[:, None, None] * rel[None]
    diag = np.where((k_loc <= q_loc)[None], off, np.float32(NEG_INF))
    tiles = np.stack([off, diag], axis=1).astype(np.float32)
    d = np.arange(n_tiles, dtype=np.float32)
    offs = -slopes[:, None] * (ATT_TILE * d)[None, :]
    offs = np.broadcast_to(offs[:, :, None], (n_heads, n_tiles, ATT_TILE)).astype(np.float32)
    return jnp.asarray(tiles), jnp.asarray(offs)


def _project_rows(x, t, g_ref, wa_ref, wvt_ref, wgt_ref, gsum_ref, wqm_ref, wkm_ref, wqd_ref, wkd_ref,
                  qm_ref, km_ref, dq_ref, dk_ref, gt_ref, vt_ref, kmean_ref, width):
    rows = x.shape[0]
    blocks = rows // MOBA_BLOCK
    ms = jnp.mean(x * x, axis=-1, keepdims=True)
    h = (x * lax.rsqrt(ms + NORM_EPS) * g_ref[...]).astype(BF16)

    gsum = gsum_ref[...]
    low_half = lax.broadcasted_iota(jnp.int32, (rows, width), 1) % LANES < HEAD_DIM

    def head_norm(y, w):
        sq = (y * y).astype(BF16)
        parts = [_dot(sq[:, c * MXU_TILE:(c + 1) * MXU_TILE], gsum) for c in range(width // MXU_TILE)]
        ssq = jnp.concatenate(parts, axis=1)
        return y * lax.rsqrt(ssq * (1.0 / HEAD_DIM) + NORM_EPS) * w

    def store_split(ref, q):
        ref[0, :, 0:width] = jnp.where(low_half, q, 0.0).astype(BF16)
        ref[0, :, width:2 * width] = jnp.where(low_half, 0.0, q).astype(BF16)

    def silu(y):
        return y * (1.0 / (1.0 + jnp.exp(-y)))

    y = _dot(h, wa_ref[:, 0 * width:1 * width])
    store_split(qm_ref, head_norm(y, wqm_ref[...]))

    y = _dot(h, wa_ref[:, 1 * width:2 * width])
    kn = head_norm(y, wkm_ref[...])
    km_ref[0] = kn.astype(BF16)
    for s in range(blocks):
        kmean_ref[0, pl.ds(t * blocks + s, 1), :] = jnp.mean(
            kn[s * MOBA_BLOCK:(s + 1) * MOBA_BLOCK], axis=0, keepdims=True)

    y = _dot(h, wa_ref[:, 2 * width:3 * width])
    store_split(dq_ref, head_norm(y, wqd_ref[...]))

    y = _dot(h, wa_ref[:, 3 * width:4 * width])
    dk_ref[0] = head_norm(y, wkd_ref[...]).astype(BF16)

    vt = _dot_nt(wvt_ref[...], h)
    gt = silu(_dot_nt(wgt_ref[...], h))
    for s in range(blocks):
        vt_ref[0, s] = vt[:, s * MOBA_BLOCK:(s + 1) * MOBA_BLOCK].astype(BF16)
        gt_ref[0, s] = gt[:, s * MOBA_BLOCK:(s + 1) * MOBA_BLOCK].astype(BF16)


def _merge_rows(x_ref, m_ref, d_ref, w_ref):
    parts = []
    for t in range(m_ref.shape[1]):
        mixed_t = jnp.concatenate([m_ref[0, t], d_ref[0, t]], axis=0)
        y = lax.dot_general(mixed_t, w_ref[...], (((0,), (0,)), ((), ())), preferred_element_type=F32)
        parts.append(x_ref[0, t * ATT_TILE:(t + 1) * ATT_TILE, :] + y)
    return jnp.concatenate(parts, axis=0)


def _inproj_kernel(x_ref, *refs, width):
    _project_rows(x_ref[0], pl.program_id(1), *refs, width)


def _mid_kernel(x_ref, m_ref, d_ref, wo_ref, *refs, width):
    xo_ref, out_refs = refs[9], refs[10:]
    x_new = _merge_rows(x_ref, m_ref, d_ref, wo_ref)
    xo_ref[0] = x_new
    _project_rows(x_new, pl.program_id(1), *refs[:9], *out_refs, width)


def _inproj(x, g, wa, wvt, wgt, gsum, wqm, wkm, wqd, wkd, merge=None):
    B, S, D = x.shape
    width = wa.shape[1] // 4
    rows = _proj_rows(S)
    nblk = S // MOBA_BLOCK
    blocks = rows // MOBA_BLOCK
    full = lambda shape: pl.BlockSpec(shape, lambda b, t: (0,) * len(shape))
    row_spec = lambda w: pl.BlockSpec((1, rows, w), lambda b, t: (b, t, 0))
    act = lambda w: jax.ShapeDtypeStruct((B, S, w), BF16)
    t_spec = pl.BlockSpec((1, blocks, 2 * width, MOBA_BLOCK), lambda b, t: (b, t, 0, 0))
    t_shape = jax.ShapeDtypeStruct((B, nblk, 2 * width, MOBA_BLOCK), BF16)
    weights = (g, wa, wvt, wgt, gsum, wqm, wkm, wqd, wkd)
    in_specs = [row_spec(D)] + [full(w.shape) for w in weights]
    out_specs = [row_spec(2 * width), row_spec(width), row_spec(2 * width), row_spec(width),
                 t_spec, t_spec, pl.BlockSpec((1, nblk, width), lambda b, t: (b, 0, 0))]
    out_shape = [act(2 * width), act(width), act(2 * width), act(width), t_shape, t_shape,
                 jax.ShapeDtypeStruct((B, nblk, width), F32)]
    params = pltpu.CompilerParams(dimension_semantics=("arbitrary", "arbitrary"))
    if merge is None:
        return pl.pallas_call(
            functools.partial(_inproj_kernel, width=width), grid=(B, S // rows),
            in_specs=in_specs, out_specs=out_specs, out_shape=out_shape,
            compiler_params=params, name="inproj")(x, *weights)
    m_att, d_att, w_out = merge
    in_specs = [row_spec(D), t_spec_of(m_att, blocks), t_spec_of(d_att, blocks), full(w_out.shape)] + in_specs[1:]
    outs = pl.pallas_call(
        functools.partial(_mid_kernel, width=width), grid=(B, S // rows),
        in_specs=in_specs, out_specs=[row_spec(D)] + out_specs,
        out_shape=[jax.ShapeDtypeStruct((B, S, D), F32)] + out_shape,
        compiler_params=params, name="outproj_inproj")(x, m_att, d_att, w_out, *weights)
    return outs[0], outs[1:]


def t_spec_of(a, tiles):
    return pl.BlockSpec((1, tiles, a.shape[2], ATT_TILE), lambda b, t: (b, t, 0, 0))


def _pair_table(n_tiles):
    rows = [(i, p, int(p == i // 2)) for i in range(n_tiles) for p in range(i // 2 + 1)]
    rows += [(rows[-1][0], rows[-1][1], 0)] * 2
    return np.ascontiguousarray(np.asarray(rows, dtype=np.int32).T)


def _item(i, t):
    t = jnp.asarray(t, jnp.int32)
    is_diag = t == 0
    j = jnp.where(is_diag, i, t - 1)
    is_pad = jnp.logical_and(t > 0, t - 1 >= i)
    return j, is_diag, is_pad


def _rows(i):
    return pl.ds(pl.multiple_of(i * ATT_TILE, ATT_TILE), ATT_TILE)


def _pad_off(is_pad):
    return jnp.where(is_pad, jnp.float32(NEG_INF), jnp.float32(0.0))


def _sweep_unroll(n_tiles):
    n_pairs = _pair_table(n_tiles).shape[1] - 2
    return next(u for u in SWEEP_UNROLLS if n_pairs % u == 0)


def _ring_slots(unroll):
    return min(unroll, MAX_RING_SLOTS)


def _sweep(tab_ref, n_maps, unroll, score_map, soft, value_map, finish):
    n_pairs = tab_ref.shape[1] - 2
    ring = _ring_slots(unroll)
    assert n_pairs % unroll == 0 and unroll % ring == 0
    pair = lambda g: (tab_ref[0, g], tab_ref[1, g])

    def score_all(g, slot):
        return tuple(x for c in range(n_maps) for x in score_map(*pair(g), slot, c))

    def half_step(g, k, carry):
        colmax, m, alpha = carry
        colmax1 = []
        for c in range(n_maps):
            colmax1 += score_map(*pair(g), k, c)
            value_map(*pair(g - 2), (k - 2) % ring, c, alpha[c])
        m1, alpha1 = soft(*pair(g - 1), (k - 1) % ring, colmax, m)
        return tuple(colmax1), m1, alpha1

    colmax = score_all(0, 0)
    m_init = tuple(jnp.full((1, ATT_TILE), NEG_INF, F32) for _ in range(n_maps))
    m, alpha = soft(*pair(0), 0, colmax, m_init)
    carry = (score_all(1, 1), m, alpha)

    def step(h, carry):
        g = 2 + unroll * h
        for k in range(unroll):
            carry = half_step(g + k, (2 + k) % ring, carry)
        for k in range(unroll):
            @pl.when(tab_ref[2, g + k - 2] == 1)
            def _(k=k):
                finish(tab_ref[0, g + k - 2])
        return carry

    lax.fori_loop(0, n_pairs // unroll, step, carry)


def _score_map(i, p, slot, c, q_ref, k_ref, bias_of, s_ref):
    q = q_ref[0, _rows(i), :]
    items = [_item(i, 2 * p + u) for u in range(2)]
    kt = jnp.concatenate([k_ref[0, _rows(j), :] for j, _, _ in items], axis=0)
    s2 = _dot_nt(kt, q)
    colmax = []
    for u, (_, is_diag, _) in enumerate(items):
        s = s2[u * ATT_TILE:(u + 1) * ATT_TILE] + bias_of(is_diag.astype(jnp.int32))
        s_ref[slot, u, c] = s.astype(BF16)
        colmax.append(jnp.max(s, axis=0, keepdims=True))
    return colmax


def _soft(p, slot, colmax, offs, m, s_ref, p_ref):
    m_out, alpha_out = [], []
    for c in range(len(m)):
        m_old = jnp.where(p == 0, jnp.float32(NEG_INF), m[c])
        m_new = jnp.maximum(m_old, jnp.maximum(colmax[2 * c] + offs[0][c], colmax[2 * c + 1] + offs[1][c]))
        for u in range(2):
            shift = (m_new - offs[u][c]).astype(BF16)
            p_ref[slot, c, u * ATT_TILE:(u + 1) * ATT_TILE, :] = jnp.exp2(s_ref[slot, u, c] - shift)
        m_out.append(m_new)
        alpha_out.append(jnp.exp2(m_old - m_new))
    return tuple(m_out), tuple(alpha_out)


def _value_map(i, slot, c, alpha_c, vts, p_ref, acc_ref):
    vt = jnp.concatenate(vts, axis=1)
    vt1 = jnp.concatenate([vt, jnp.ones((BF16_ROWS, vt.shape[1]), BF16)], axis=0)
    par = i % acc_ref.shape[0]
    acc_ref[par, c] = alpha_c * acc_ref[par, c] + _dot(vt1, p_ref[slot, c])


def _moba_kernel(tab_ref, qa_ref, qb_ref, k_ref, vt_ref, gt_ref, kmean_ref, bias_ref, offs_ref, o_ref,
                 sel_ref, s_ref, p_ref, acc_ref):
    nblk = kmean_ref.shape[1]
    n_tiles = k_ref.shape[1] // ATT_TILE
    q_refs = (qa_ref, qb_ref)
    acc_ref[...] = jnp.zeros(acc_ref.shape, F32)

    km = kmean_ref[0]
    km_hi = km.astype(BF16)
    km_lo = (km - km_hi.astype(F32)).astype(BF16)
    blk = lax.broadcasted_iota(jnp.int32, (nblk, ATT_TILE), 0)
    for i in range(n_tiles):
        valid = blk < i
        for hh in range(2):
            q = q_refs[hh][0, i * ATT_TILE:(i + 1) * ATT_TILE, :]
            g = _dot_nt(km_hi, q) + _dot_nt(km_lo, q)
            g = jnp.where(valid, g, -jnp.inf)
            sel = blk == i
            for _ in range(min(MOBA_TOPK, nblk - 1)):
                mx = jnp.max(g, axis=0, keepdims=True)
                first = jnp.min(jnp.where(g == mx, blk, nblk), axis=0, keepdims=True)
                pick = blk == first
                sel = jnp.logical_or(sel, jnp.logical_and(pick, valid))
                g = jnp.where(pick, -jnp.inf, g)
            sel_ref[i, hh] = jnp.where(sel, 0.0, NEG_INF).astype(F32)

    def score_map(i, p, slot, hh):
        return _score_map(i, p, slot, hh, q_refs[hh], k_ref, lambda d: bias_ref[hh, d], s_ref)

    def soft(i, p, slot, colmax, m):
        offs = []
        for u in range(2):
            j, _, is_pad = _item(i, 2 * p + u)
            pad = _pad_off(is_pad)
            offs.append([offs_ref[hh, pl.ds(i - j, 1), :] + sel_ref[i, hh, pl.ds(j, 1), :] + pad
                         for hh in range(2)])
        return _soft(p, slot, colmax, offs, m, s_ref, p_ref)

    def value_map(i, p, slot, hh, alpha_c):
        vts = []
        for u in range(2):
            j, _, _ = _item(i, 2 * p + u)
            vts.append(vt_ref[0, j, hh * HEAD_DIM:(hh + 1) * HEAD_DIM, :])
        _value_map(i, slot, hh, alpha_c, vts, p_ref, acc_ref)

    def finish(i):
        par = i % acc_ref.shape[0]
        o_t = jnp.concatenate([acc_ref[par, hh, 0:HEAD_DIM] * (1.0 / acc_ref[par, hh, HEAD_DIM:HEAD_DIM + 1])
                               for hh in range(2)], axis=0)
        o_ref[0, i] = (o_t * gt_ref[0, i].astype(F32)).astype(BF16)

    _sweep(tab_ref, 2, acc_ref.shape[0], score_map, soft, value_map, finish)


def _moba(q_split, km, vt, gt, kmean, bias, offs, tab):
    B, S, width = km.shape
    pairs = width // LANES
    nt = S // ATT_TILE
    nblk = kmean.shape[1]
    unroll = _sweep_unroll(nt)
    ring = _ring_slots(unroll)
    seq_spec = lambda lane_block0: pl.BlockSpec((1, S, LANES), lambda b, p: (b, 0, lane_block0 + p))
    t_spec = pl.BlockSpec((1, nblk, LANES, ATT_TILE), lambda b, p: (b, 0, p, 0))
    return pl.pallas_call(
        _moba_kernel,
        grid=(B, pairs),
        in_specs=[pl.BlockSpec(memory_space=pltpu.SMEM),
                  seq_spec(0), seq_spec(pairs), seq_spec(0), t_spec, t_spec,
                  pl.BlockSpec((1, nblk, LANES), lambda b, p: (b, 0, p)),
                  pl.BlockSpec((2, 2, ATT_TILE, ATT_TILE), lambda b, p: (p, 0, 0, 0)),
                  pl.BlockSpec((2, nt, ATT_TILE), lambda b, p: (p, 0, 0))],
        out_specs=t_spec,
        out_shape=jax.ShapeDtypeStruct((B, nt, width, ATT_TILE), BF16),
        scratch_shapes=[pltpu.VMEM((nt, 2, nblk, ATT_TILE), F32),
                        pltpu.VMEM((ring, 2, 2, ATT_TILE, ATT_TILE), BF16),
                        pltpu.VMEM((ring, 2, 2 * ATT_TILE, ATT_TILE), BF16),
                        pltpu.VMEM((unroll, 2, HEAD_DIM + BF16_ROWS, ATT_TILE), F32)],
        compiler_params=pltpu.CompilerParams(dimension_semantics=("arbitrary", "arbitrary")),
        name="moba_attn",
    )(tab, q_split, q_split, km, vt, gt, kmean, bias, offs)


def _diff_kernel(tab_ref, qa_ref, qb_ref, k_ref, vt_ref, gt_ref, bias_ref, offs_ref, lamv_ref, subw_ref,
                 o_ref, s_ref, p_ref, acc_ref, *, lambda_init):
    q_refs = (qa_ref, qb_ref)
    acc_ref[...] = jnp.zeros(acc_ref.shape, F32)
    lv = lamv_ref[...]
    lam = (jnp.exp(jnp.sum(lv[0:1] * lv[1:2], axis=-1, keepdims=True))
           - jnp.exp(jnp.sum(lv[2:3] * lv[3:4], axis=-1, keepdims=True)) + lambda_init)

    def score_map(i, p, slot, c):
        return _score_map(i, p, slot, c, q_refs[c], k_ref, lambda d: bias_ref[0, d], s_ref)

    def soft(i, p, slot, colmax, m):
        offs = []
        for u in range(2):
            j, _, is_pad = _item(i, 2 * p + u)
            off = offs_ref[0, pl.ds(i - j, 1), :] + _pad_off(is_pad)
            offs.append([off, off])
        return _soft(p, slot, colmax, offs, m, s_ref, p_ref)

    def value_map(i, p, slot, c, alpha_c):
        vts = [vt_ref[0, _item(i, 2 * p + u)[0]] for u in range(2)]
        _value_map(i, slot, c, alpha_c, vts, p_ref, acc_ref)

    def finish(i):
        par = i % acc_ref.shape[0]
        o = [acc_ref[par, c, 0:LANES] * (1.0 / acc_ref[par, c, LANES:LANES + 1]) for c in range(2)]
        o_t = o[0] - lam * o[1]
        ms = jnp.mean(o_t * o_t, axis=0, keepdims=True)
        o_n = o_t * lax.rsqrt(ms + NORM_EPS) * subw_ref[...]
        o_ref[0, i] = (o_n * gt_ref[0, i].astype(F32)).astype(BF16)

    _sweep(tab_ref, 2, acc_ref.shape[0], score_map, soft, value_map, finish)


def _diff(q_split, dk, vt, gt, bias, offs, lamv, subw, lambda_init, v_row_block0, tab):
    B, S, width = dk.shape
    heads = width // LANES
    nt = S // ATT_TILE
    nblk = vt.shape[1]
    unroll = _sweep_unroll(nt)
    ring = _ring_slots(unroll)
    seq_spec = lambda lane_block0: pl.BlockSpec((1, S, LANES), lambda b, h: (b, 0, lane_block0 + h))
    t_spec = lambda row_block0: pl.BlockSpec((1, nblk, LANES, ATT_TILE), lambda b, h: (b, 0, row_block0 + h, 0))
    return pl.pallas_call(
        functools.partial(_diff_kernel, lambda_init=lambda_init),
        grid=(B, heads),
        in_specs=[pl.BlockSpec(memory_space=pltpu.SMEM),
                  seq_spec(0), seq_spec(heads), seq_spec(0), t_spec(v_row_block0), t_spec(v_row_block0),
                  pl.BlockSpec((1, 2, ATT_TILE, ATT_TILE), lambda b, h: (h, 0, 0, 0)),
                  pl.BlockSpec((1, nt, ATT_TILE), lambda b, h: (h, 0, 0)),
                  pl.BlockSpec(lamv.shape, lambda b, h: (0, 0)),
                  pl.BlockSpec((LANES, ATT_TILE), lambda b, h: (0, 0))],
        out_specs=t_spec(0),
        out_shape=jax.ShapeDtypeStruct((B, nt, width, ATT_TILE), BF16),
        scratch_shapes=[pltpu.VMEM((ring, 2, 2, ATT_TILE, ATT_TILE), BF16),
                        pltpu.VMEM((ring, 2, 2 * ATT_TILE, ATT_TILE), BF16),
                        pltpu.VMEM((unroll, 2, LANES + BF16_ROWS, ATT_TILE), F32)],
        compiler_params=pltpu.CompilerParams(dimension_semantics=("arbitrary", "arbitrary")),
        name="diff_attn",
    )(tab, q_split, q_split, dk, vt, gt, bias, offs, lamv, subw)


def _outproj_kernel(x_ref, m_ref, d_ref, w_ref, o_ref):
    o_ref[0] = _merge_rows(x_ref, m_ref, d_ref, w_ref)


def _outproj(x, m_att, d_att, w_out):
    B, S, D = x.shape
    rows = _proj_rows(S)
    tiles = rows // ATT_TILE
    row_spec = pl.BlockSpec((1, rows, D), lambda b, t: (b, t, 0))
    return pl.pallas_call(
        _outproj_kernel,
        grid=(B, S // rows),
        in_specs=[row_spec, t_spec_of(m_att, tiles), t_spec_of(d_att, tiles),
                  pl.BlockSpec(w_out.shape, lambda b, t: (0, 0))],
        out_specs=row_spec,
        out_shape=jax.ShapeDtypeStruct((B, S, D), F32),
        compiler_params=pltpu.CompilerParams(dimension_semantics=("arbitrary", "arbitrary")),
        name="outproj",
    )(x, m_att, d_att, w_out)


def kernel(x, norm_g, w_in, moba_q_norm, moba_k_norm, diff_q_norm, diff_k_norm,
           lambda_q1, lambda_k1, lambda_q2, lambda_k2, diff_subln, w_out):
    B, S, D = x.shape
    depth = w_in.shape[0]
    width = D // 2
    moba_heads = width // HEAD_DIM
    diff_heads = width // (2 * HEAD_DIM)
    assert S % MOBA_BLOCK == 0 and width % MXU_TILE == 0 and S // MOBA_BLOCK > 1
    nt = S // ATT_TILE
    scale = HEAD_DIM ** -0.5 * LOG2E

    m_bias, m_offs = _bias_tables(moba_heads, nt)
    d_bias, d_offs = _bias_tables(diff_heads, nt)
    r = lax.broadcasted_iota(jnp.int32, (MXU_TILE, MXU_TILE), 0) // HEAD_DIM
    c = lax.broadcasted_iota(jnp.int32, (MXU_TILE, MXU_TILE), 1) // HEAD_DIM
    gsum = (r == c).astype(BF16)
    reps = width // HEAD_DIM
    tab = jnp.asarray(_pair_table(nt))

    def layer_weights(layer):
        w = w_in[layer]
        cols = [w[:, k * width:(k + 1) * width] for k in range(8)]
        wa = jnp.concatenate([cols[0], cols[1], cols[4], cols[5]], axis=1).astype(BF16)
        wvt = jnp.concatenate([cols[2], cols[6]], axis=1).astype(BF16).T
        wgt = jnp.concatenate([cols[3], cols[7]], axis=1).astype(BF16).T
        tile_w = lambda v, s: (jnp.tile(v, reps) * s)[None, :].astype(F32)
        return (norm_g[layer][None, :], wa, wvt, wgt, gsum,
                tile_w(moba_q_norm[layer], scale), tile_w(moba_k_norm[layer], 1.0),
                tile_w(diff_q_norm[layer], scale), tile_w(diff_k_norm[layer], 1.0))

    proj = _inproj(x, *layer_weights(0))
    for layer in range(depth):
        qm, km, dq, dk, gt, vt, kmean = proj
        m_att = _moba(qm, km, vt, gt, kmean, m_bias, m_offs, tab)
        lambda_init = 0.8 - 0.6 * math.exp(-0.3 * layer)
        lamv = jnp.stack([lambda_q1[layer], lambda_k1[layer], lambda_q2[layer], lambda_k2[layer]])
        subw = jnp.broadcast_to((diff_subln[layer] * (1.0 - lambda_init))[:, None],
                                (2 * HEAD_DIM, ATT_TILE)).astype(F32)
        d_att = _diff(dq, dk, vt, gt, d_bias, d_offs, lamv, subw, lambda_init, width // LANES, tab)
        wo = w_out[layer].astype(BF16)
        if layer + 1 < depth:
            x, proj = _inproj(x, *layer_weights(layer + 1), merge=(m_att, d_att, wo))
        else:
            x = _outproj(x, m_att, d_att, wo)
    return x
```

```python
import functools
import math

import numpy as np
import jax
import jax.numpy as jnp
from jax import lax
from jax.experimental import pallas as pl
from jax.experimental.pallas import tpu as pltpu

HEAD_DIM = 64
MOBA_BLOCK = 256
MOBA_TOPK = 3
NORM_EPS = 1e-6
NEG_INF = -1e30

LANES = 128
MXU_TILE = 256
ATT_TILE = MOBA_BLOCK
PROJ_ROWS = 1024
LOG2E = math.log2(math.e)
BF16_ROWS = 16
SWEEP_UNROLLS = (8, 4, 2)
MAX_RING_SLOTS = 4

F32 = jnp.float32
BF16 = jnp.bfloat16


def _dot(a, b):
    return jnp.dot(a, b, preferred_element_type=F32)


def _dot_nt(a, b):
    return lax.dot_general(a, b, (((1,), (1,)), ((), ())), preferred_element_type=F32)


def _proj_rows(seq):
    return max(r for r in range(MOBA_BLOCK, PROJ_ROWS + 1, MOBA_BLOCK) if seq % r == 0)


def _alibi_slopes(n_heads):
    return np.asarray(2.0 ** (-8.0 * np.arange(1, n_heads + 1) / n_heads), dtype=np.float32)


def _bias_tables(n_heads, n_tiles):
    slopes = _alibi_slopes(n_heads) * np.float32(LOG2E)
    k_loc = np.arange(ATT_TILE, dtype=np.int32)[:, None]
    q_loc = np.arange(ATT_TILE, dtype=np.int32)[None, :]
    rel = (k_loc - q_loc).astype(np.float32)
    off = slopes[:, None, None] * rel[None]
    diag = np.where((k_loc <= q_loc)[None], off, np.float32(NEG_INF))
    tiles = np.stack([off, diag], axis=1).astype(np.float32)
    d = np.arange(n_tiles, dtype=np.float32)
    offs = -slopes[:, None] * (ATT_TILE * d)[None, :]
    offs = np.broadcast_to(offs[:, :, None], (n_heads, n_tiles, ATT_TILE)).astype(np.float32)
    return jnp.asarray(tiles), jnp.asarray(offs)


def _project_rows(x, t, g_ref, wa_ref, wvt_ref, wgt_ref, gsum_ref, wqm_ref, wkm_ref, wqd_ref, wkd_ref,
                  qm_ref, km_ref, dq_ref, dk_ref, gt_ref, vt_ref, kmean_ref, width):
    rows = x.shape[0]
    blocks = rows // MOBA_BLOCK
    ms = jnp.mean(x * x, axis=-1, keepdims=True)
    h = (x * lax.rsqrt(ms + NORM_EPS) * g_ref[...]).astype(BF16)

    gsum = gsum_ref[...]
    low_half = lax.broadcasted_iota(jnp.int32, (rows, width), 1) % LANES < HEAD_DIM

    def head_norm(y, w):
        sq = (y * y).astype(BF16)
        parts = [_dot(sq[:, c * MXU_TILE:(c + 1) * MXU_TILE], gsum) for c in range(width // MXU_TILE)]
        ssq = jnp.concatenate(parts, axis=1)
        return y * lax.rsqrt(ssq * (1.0 / HEAD_DIM) + NORM_EPS) * w

    def store_split(ref, q):
        ref[0, :, 0:width] = jnp.where(low_half, q, 0.0).astype(BF16)
        ref[0, :, width:2 * width] = jnp.where(low_half, 0.0, q).astype(BF16)

    def silu(y):
        return y * (1.0 / (1.0 + jnp.exp(-y)))

    y = _dot(h, wa_ref[:, 0 * width:1 * width])
    store_split(qm_ref, head_norm(y, wqm_ref[...]))

    y = _dot(h, wa_ref[:, 1 * width:2 * width])
    kn = head_norm(y, wkm_ref[...])
    km_ref[0] = kn.astype(BF16)
    for s in range(blocks):
        kmean_ref[0, pl.ds(t * blocks + s, 1), :] = jnp.mean(
            kn[s * MOBA_BLOCK:(s + 1) * MOBA_BLOCK], axis=0, keepdims=True)

    y = _dot(h, wa_ref[:, 2 * width:3 * width])
    store_split(dq_ref, head_norm(y, wqd_ref[...]))

    y = _dot(h, wa_ref[:, 3 * width:4 * width])
    dk_ref[0] = head_norm(y, wkd_ref[...]).astype(BF16)

    vt = _dot_nt(wvt_ref[...], h)
    gt = silu(_dot_nt(wgt_ref[...], h))
    for s in range(blocks):
        vt_ref[0, s] = vt[:, s * MOBA_BLOCK:(s + 1) * MOBA_BLOCK].astype(BF16)
        gt_ref[0, s] = gt[:, s * MOBA_BLOCK:(s + 1) * MOBA_BLOCK].astype(BF16)


def _merge_rows(x_ref, m_ref, d_ref, w_ref):
    parts = []
    for t in range(m_ref.shape[1]):
        mixed_t = jnp.concatenate([m_ref[0, t], d_ref[0, t]], axis=0)
        y = lax.dot_general(mixed_t, w_ref[...], (((0,), (0,)), ((), ())), preferred_element_type=F32)
        parts.append(x_ref[0, t * ATT_TILE:(t + 1) * ATT_TILE, :] + y)
    return jnp.concatenate(parts, axis=0)


def _inproj_kernel(x_ref, *refs, width):
    _project_rows(x_ref[0], pl.program_id(1), *refs, width)


def _mid_kernel(x_ref, m_ref, d_ref, wo_ref, *refs, width):
    xo_ref, out_refs = refs[9], refs[10:]
    x_new = _merge_rows(x_ref, m_ref, d_ref, wo_ref)
    xo_ref[0] = x_new
    _project_rows(x_new, pl.program_id(1), *refs[:9], *out_refs, width)


def _inproj(x, g, wa, wvt, wgt, gsum, wqm, wkm, wqd, wkd, merge=None):
    B, S, D = x.shape
    width = wa.shape[1] // 4
    rows = _proj_rows(S)
    nblk = S // MOBA_BLOCK
    blocks = rows // MOBA_BLOCK
    full = lambda shape: pl.BlockSpec(shape, lambda b, t: (0,) * len(shape))
    row_spec = lambda w: pl.BlockSpec((1, rows, w), lambda b, t: (b, t, 0))
    act = lambda w: jax.ShapeDtypeStruct((B, S, w), BF16)
    t_spec = pl.BlockSpec((1, blocks, 2 * width, MOBA_BLOCK), lambda b, t: (b, t, 0, 0))
    t_shape = jax.ShapeDtypeStruct((B, nblk, 2 * width, MOBA_BLOCK), BF16)
    weights = (g, wa, wvt, wgt, gsum, wqm, wkm, wqd, wkd)
    in_specs = [row_spec(D)] + [full(w.shape) for w in weights]
    out_specs = [row_spec(2 * width), row_spec(width), row_spec(2 * width), row_spec(width),
                 t_spec, t_spec, pl.BlockSpec((1, nblk, width), lambda b, t: (b, 0, 0))]
    out_shape = [act(2 * width), act(width), act(2 * width), act(width), t_shape, t_shape,
                 jax.ShapeDtypeStruct((B, nblk, width), F32)]
    params = pltpu.CompilerParams(dimension_semantics=("arbitrary", "arbitrary"))
    if merge is None:
        return pl.pallas_call(
            functools.partial(_inproj_kernel, width=width), grid=(B, S // rows),
            in_specs=in_specs, out_specs=out_specs, out_shape=out_shape,
            compiler_params=params, name="inproj")(x, *weights)
    m_att, d_att, w_out = merge
    in_specs = [row_spec(D), t_spec_of(m_att, blocks), t_spec_of(d_att, blocks), full(w_out.shape)] + in_specs[1:]
    outs = pl.pallas_call(
        functools.partial(_mid_kernel, width=width), grid=(B, S // rows),
        in_specs=in_specs, out_specs=[row_spec(D)] + out_specs,
        out_shape=[jax.ShapeDtypeStruct((B, S, D), F32)] + out_shape,
        compiler_params=params, name="outproj_inproj")(x, m_att, d_att, w_out, *weights)
    return outs[0], outs[1:]


def t_spec_of(a, tiles):
    return pl.BlockSpec((1, tiles, a.shape[2], ATT_TILE), lambda b, t: (b, t, 0, 0))


def _pair_table(n_tiles):
    rows = [(i, p, int(p == i // 2)) for i in range(n_tiles) for p in range(i // 2 + 1)]
    rows += [(rows[-1][0], rows[-1][1], 0)] * 2
    return np.ascontiguousarray(np.asarray(rows, dtype=np.int32).T)


def _item(i, t):
    t = jnp.asarray(t, jnp.int32)
    is_diag = t == 0
    j = jnp.where(is_diag, i, t - 1)
    is_pad = jnp.logical_and(t > 0, t - 1 >= i)
    return j, is_diag, is_pad


def _rows(i):
    return pl.ds(pl.multiple_of(i * ATT_TILE, ATT_TILE), ATT_TILE)


def _pad_off(is_pad):
    return jnp.where(is_pad, jnp.float32(NEG_INF), jnp.float32(0.0))


def _sweep_unroll(n_tiles):
    n_pairs = _pair_table(n_tiles).shape[1] - 2
    return next(u for u in SWEEP_UNROLLS if n_pairs % u == 0)


def _ring_slots(unroll):
    return min(unroll, MAX_RING_SLOTS)


def _sweep(tab_ref, n_maps, unroll, score_map, soft, value_map, finish):
    n_pairs = tab_ref.shape[1] - 2
    ring = _ring_slots(unroll)
    assert n_pairs % unroll == 0 and unroll % ring == 0
    pair = lambda g: (tab_ref[0, g], tab_ref[1, g])

    def score_all(g, slot):
        return tuple(x for c in range(n_maps) for x in score_map(*pair(g), slot, c))

    def half_step(g, k, carry):
        colmax, m, alpha = carry
        colmax1 = []
        for c in range(n_maps):
            colmax1 += score_map(*pair(g), k, c)
            value_map(*pair(g - 2), (k - 2) % ring, c, alpha[c])
        m1, alpha1 = soft(*pair(g - 1), (k - 1) % ring, colmax, m)
        return tuple(colmax1), m1, alpha1

    colmax = score_all(0, 0)
    m_init = tuple(jnp.full((1, ATT_TILE), NEG_INF, F32) for _ in range(n_maps))
    m, alpha = soft(*pair(0), 0, colmax, m_init)
    carry = (score_all(1, 1), m, alpha)

    def step(h, carry):
        g = 2 + unroll * h
        for k in range(unroll):
            carry = half_step(g + k, (2 + k) % ring, carry)
        for k in range(unroll):
            @pl.when(tab_ref[2, g + k - 2] == 1)
            def _(k=k):
                finish(tab_ref[0, g + k - 2])
        return carry

    lax.fori_loop(0, n_pairs // unroll, step, carry)


def _score_map(i, p, slot, c, q_ref, k_ref, bias_of, s_ref):
    q = q_ref[0, _rows(i), :]
    items = [_item(i, 2 * p + u) for u in range(2)]
    kt = jnp.concatenate([k_ref[0, _rows(j), :] for j, _, _ in items], axis=0)
    s2 = _dot_nt(kt, q)
    colmax = []
    for u, (_, is_diag, _) in enumerate(items):
        s = s2[u * ATT_TILE:(u + 1) * ATT_TILE] + bias_of(is_diag.astype(jnp.int32))
        s_ref[slot, u, c] = s.astype(BF16)
        colmax.append(jnp.max(s, axis=0, keepdims=True))
    return colmax


def _soft(p, slot, colmax, offs, m, s_ref, p_ref):
    m_out, alpha_out = [], []
    for c in range(len(m)):
        m_old = jnp.where(p == 0, jnp.float32(NEG_INF), m[c])
        m_new = jnp.maximum(m_old, jnp.maximum(colmax[2 * c] + offs[0][c], colmax[2 * c + 1] + offs[1][c]))
        for u in range(2):
            shift = (m_new - offs[u][c]).astype(BF16)
            p_ref[slot, c, u * ATT_TILE:(u + 1) * ATT_TILE, :] = jnp.exp2(s_ref[slot, u, c] - shift)
        m_out.append(m_new)
        alpha_out.append(jnp.exp2(m_old - m_new))
    return tuple(m_out), tuple(alpha_out)


def _value_map(i, slot, c, alpha_c, vts, p_ref, acc_ref):
    vt = jnp.concatenate(vts, axis=1)
    vt1 = jnp.concatenate([vt, jnp.ones((BF16_ROWS, vt.shape[1]), BF16)], axis=0)
    par = i % acc_ref.shape[0]
    acc_ref[par, c] = alpha_c * acc_ref[par, c] + _dot(vt1, p_ref[slot, c])


def _moba_kernel(tab_ref, qa_ref, qb_ref, k_ref, vt_ref, gt_ref, kmean_ref, bias_ref, offs_ref, o_ref,
                 sel_ref, s_ref, p_ref, acc_ref):
    nblk = kmean_ref.shape[1]
    n_tiles = k_ref.shape[1] // ATT_TILE
    q_refs = (qa_ref, qb_ref)
    acc_ref[...] = jnp.zeros(acc_ref.shape, F32)

    km = kmean_ref[0]
    km_hi = km.astype(BF16)
    km_lo = (km - km_hi.astype(F32)).astype(BF16)
    blk = lax.broadcasted_iota(jnp.int32, (nblk, ATT_TILE), 0)
    for i in range(n_tiles):
        valid = blk < i
        for hh in range(2):
            q = q_refs[hh][0, i * ATT_TILE:(i + 1) * ATT_TILE, :]
            g = _dot_nt(km_hi, q) + _dot_nt(km_lo, q)
            g = jnp.where(valid, g, -jnp.inf)
            sel = blk == i
            for _ in range(min(MOBA_TOPK, nblk - 1)):
                mx = jnp.max(g, axis=0, keepdims=True)
                first = jnp.min(jnp.where(g == mx, blk, nblk), axis=0, keepdims=True)
                pick = blk == first
                sel = jnp.logical_or(sel, jnp.logical_and(pick, valid))
                g = jnp.where(pick, -jnp.inf, g)
            sel_ref[i, hh] = jnp.where(sel, 0.0, NEG_INF).astype(F32)

    def score_map(i, p, slot, hh):
        return _score_map(i, p, slot, hh, q_refs[hh], k_ref, lambda d: bias_ref[hh, d], s_ref)

    def soft(i, p, slot, colmax, m):
        offs = []
        for u in range(2):
            j, _, is_pad = _item(i, 2 * p + u)
            pad = _pad_off(is_pad)
            offs.append([offs_ref[hh, pl.ds(i - j, 1), :] + sel_ref[i, hh, pl.ds(j, 1), :] + pad
                         for hh in range(2)])
        return _soft(p, slot, colmax, offs, m, s_ref, p_ref)

    def value_map(i, p, slot, hh, alpha_c):
        vts = []
        for u in range(2):
            j, _, _ = _item(i, 2 * p + u)
            vts.append(vt_ref[0, j, hh * HEAD_DIM:(hh + 1) * HEAD_DIM, :])
        _value_map(i, slot, hh, alpha_c, vts, p_ref, acc_ref)

    def finish(i):
        par = i % acc_ref.shape[0]
        o_t = jnp.concatenate([acc_ref[par, hh, 0:HEAD_DIM] * (1.0 / acc_ref[par, hh, HEAD_DIM:HEAD_DIM + 1])
                               for hh in range(2)], axis=0)
        o_ref[0, i] = (o_t * gt_ref[0, i].astype(F32)).astype(BF16)

    _sweep(tab_ref, 2, acc_ref.shape[0], score_map, soft, value_map, finish)


def _moba(q_split, km, vt, gt, kmean, bias, offs, tab):
    B, S, width = km.shape
    pairs = width // LANES
    nt = S // ATT_TILE
    nblk = kmean.shape[1]
    unroll = _sweep_unroll(nt)
    ring = _ring_slots(unroll)
    seq_spec = lambda lane_block0: pl.BlockSpec((1, S, LANES), lambda b, p: (b, 0, lane_block0 + p))
    t_spec = pl.BlockSpec((1, nblk, LANES, ATT_TILE), lambda b, p: (b, 0, p, 0))
    return pl.pallas_call(
        _moba_kernel,
        grid=(B, pairs),
        in_specs=[pl.BlockSpec(memory_space=pltpu.SMEM),
                  seq_spec(0), seq_spec(pairs), seq_spec(0), t_spec, t_spec,
                  pl.BlockSpec((1, nblk, LANES), lambda b, p: (b, 0, p)),
                  pl.BlockSpec((2, 2, ATT_TILE, ATT_TILE), lambda b, p: (p, 0, 0, 0)),
                  pl.BlockSpec((2, nt, ATT_TILE), lambda b, p: (p, 0, 0))],
        out_specs=t_spec,
        out_shape=jax.ShapeDtypeStruct((B, nt, width, ATT_TILE), BF16),
        scratch_shapes=[pltpu.VMEM((nt, 2, nblk, ATT_TILE), F32),
                        pltpu.VMEM((ring, 2, 2, ATT_TILE, ATT_TILE), BF16),
                        pltpu.VMEM((ring, 2, 2 * ATT_TILE, ATT_TILE), BF16),
                        pltpu.VMEM((unroll, 2, HEAD_DIM + BF16_ROWS, ATT_TILE), F32)],
        compiler_params=pltpu.CompilerParams(dimension_semantics=("arbitrary", "arbitrary")),
        name="moba_attn",
    )(tab, q_split, q_split, km, vt, gt, kmean, bias, offs)


def _diff_kernel(tab_ref, qa_ref, qb_ref, k_ref, vt_ref, gt_ref, bias_ref, offs_ref, lamv_ref, subw_ref,
                 o_ref, s_ref, p_ref, acc_ref, *, lambda_init):
    q_refs = (qa_ref, qb_ref)
    acc_ref[...] = jnp.zeros(acc_ref.shape, F32)
    lv = lamv_ref[...]
    lam = (jnp.exp(jnp.sum(lv[0:1] * lv[1:2], axis=-1, keepdims=True))
           - jnp.exp(jnp.sum(lv[2:3] * lv[3:4], axis=-1, keepdims=True)) + lambda_init)
    lam = jnp.broadcast_to(lam, (1, ATT_TILE))

    def score_map(i, p, slot, c):
        return _score_map(i, p, slot, c, q_refs[c], k_ref, lambda d: bias_ref[0, d], s_ref)

    def soft(i, p, slot, colmax, m):
        offs = []
        for u in range(2):
            j, _, is_pad = _item(i, 2 * p + u)
            off = offs_ref[0, pl.ds(i - j, 1), :] + _pad_off(is_pad)
            offs.append([off, off])
        return _soft(p, slot, colmax, offs, m, s_ref, p_ref)

    def value_map(i, p, slot, c, alpha_c):
        vts = [vt_ref[0, _item(i, 2 * p + u)[0]] for u in range(2)]
        _value_map(i, slot, c, alpha_c, vts, p_ref, acc_ref)

    def finish(i):
        par = i % acc_ref.shape[0]
        inv = [1.0 / acc_ref[par, c, LANES:LANES + 1] for c in range(2)]
        o_t = acc_ref[par, 0, 0:LANES] * inv[0] - acc_ref[par, 1, 0:LANES] * (lam * inv[1])
        ms = jnp.mean(o_t * o_t, axis=0, keepdims=True)
        o_n = o_t * lax.rsqrt(ms + NORM_EPS) * subw_ref[...]
        o_ref[0, i] = (o_n * gt_ref[0, i].astype(F32)).astype(BF16)

    _sweep(tab_ref, 2, acc_ref.shape[0], score_map, soft, value_map, finish)


def _diff(q_split, dk, vt, gt, bias, offs, lamv, subw, lambda_init, v_row_block0, tab):
    B, S, width = dk.shape
    heads = width // LANES
    nt = S // ATT_TILE
    nblk = vt.shape[1]
    unroll = _sweep_unroll(nt)
    ring = _ring_slots(unroll)
    seq_spec = lambda lane_block0: pl.BlockSpec((1, S, LANES), lambda b, h: (b, 0, lane_block0 + h))
    t_spec = lambda row_block0: pl.BlockSpec((1, nblk, LANES, ATT_TILE), lambda b, h: (b, 0, row_block0 + h, 0))
    return pl.pallas_call(
        functools.partial(_diff_kernel, lambda_init=lambda_init),
        grid=(B, heads),
        in_specs=[pl.BlockSpec(memory_space=pltpu.SMEM),
                  seq_spec(0), seq_spec(heads), seq_spec(0), t_spec(v_row_block0), t_spec(v_row_block0),
                  pl.BlockSpec((1, 2, ATT_TILE, ATT_TILE), lambda b, h: (h, 0, 0, 0)),
                  pl.BlockSpec((1, nt, ATT_TILE), lambda b, h: (h, 0, 0)),
                  pl.BlockSpec(lamv.shape, lambda b, h: (0, 0)),
                  pl.BlockSpec((LANES, ATT_TILE), lambda b, h: (0, 0))],
        out_specs=t_spec(0),
        out_shape=jax.ShapeDtypeStruct((B, nt, width, ATT_TILE), BF16),
        scratch_shapes=[pltpu.VMEM((ring, 2, 2, ATT_TILE, ATT_TILE), BF16),
                        pltpu.VMEM((ring, 2, 2 * ATT_TILE, ATT_TILE), BF16),
                        pltpu.VMEM((unroll, 2, LANES + BF16_ROWS, ATT_TILE), F32)],
        compiler_params=pltpu.CompilerParams(dimension_semantics=("arbitrary", "arbitrary")),
        name="diff_attn",
    )(tab, q_split, q_split, dk, vt, gt, bias, offs, lamv, subw)


def _outproj_kernel(x_ref, m_ref, d_ref, w_ref, o_ref):
    o_ref[0] = _merge_rows(x_ref, m_ref, d_ref, w_ref)


def _outproj(x, m_att, d_att, w_out):
    B, S, D = x.shape
    rows = _proj_rows(S)
    tiles = rows // ATT_TILE
    row_spec = pl.BlockSpec((1, rows, D), lambda b, t: (b, t, 0))
    return pl.pallas_call(
        _outproj_kernel,
        grid=(B, S // rows),
        in_specs=[row_spec, t_spec_of(m_att, tiles), t_spec_of(d_att, tiles),
                  pl.BlockSpec(w_out.shape, lambda b, t: (0, 0))],
        out_specs=row_spec,
        out_shape=jax.ShapeDtypeStruct((B, S, D), F32),
        compiler_params=pltpu.CompilerParams(dimension_semantics=("arbitrary", "arbitrary")),
        name="outproj",
    )(x, m_att, d_att, w_out)


def kernel(x, norm_g, w_in, moba_q_norm, moba_k_norm, diff_q_norm, diff_k_norm,
           lambda_q1, lambda_k1, lambda_q2, lambda_k2, diff_subln, w_out):
    B, S, D = x.shape
    depth = w_in.shape[0]
    width = D // 2
    moba_heads = width // HEAD_DIM
    diff_heads = width // (2 * HEAD_DIM)
    assert S % MOBA_BLOCK == 0 and width % MXU_TILE == 0 and S // MOBA_BLOCK > 1
    nt = S // ATT_TILE
    scale = HEAD_DIM ** -0.5 * LOG2E

    m_bias, m_offs = _bias_tables(moba_heads, nt)
    d_bias, d_offs = _bias_tables(diff_heads, nt)
    r = lax.broadcasted_iota(jnp.int32, (MXU_TILE, MXU_TILE), 0) // HEAD_DIM
    c = lax.broadcasted_iota(jnp.int32, (MXU_TILE, MXU_TILE), 1) // HEAD_DIM
    gsum = (r == c).astype(BF16)
    reps = width // HEAD_DIM
    tab = jnp.asarray(_pair_table(nt))

    def layer_weights(layer):
        w = w_in[layer]
        cols = [w[:, k * width:(k + 1) * width] for k in range(8)]
        wa = jnp.concatenate([cols[0], cols[1], cols[4], cols[5]], axis=1).astype(BF16)
        wvt = jnp.concatenate([cols[2], cols[6]], axis=1).astype(BF16).T
        wgt = jnp.concatenate([cols[3], cols[7]], axis=1).astype(BF16).T
        tile_w = lambda v, s: (jnp.tile(v, reps) * s)[None, :].astype(F32)
        return (norm_g[layer][None, :], wa, wvt, wgt, gsum,
                tile_w(moba_q_norm[layer], scale), tile_w(moba_k_norm[layer], 1.0),
                tile_w(diff_q_norm[layer], scale), tile_w(diff_k_norm[layer], 1.0))

    proj = _inproj(x, *layer_weights(0))
    for layer in range(depth):
        qm, km, dq, dk, gt, vt, kmean = proj
        m_att = _moba(qm, km, vt, gt, kmean, m_bias, m_offs, tab)
        lambda_init = 0.8 - 0.6 * math.exp(-0.3 * layer)
        lamv = jnp.stack([lambda_q1[layer], lambda_k1[layer], lambda_q2[layer], lambda_k2[layer]])
        subw = jnp.broadcast_to((diff_subln[layer] * (1.0 - lambda_init))[:, None],
                                (2 * HEAD_DIM, ATT_TILE)).astype(F32)
        d_att = _diff(dq, dk, vt, gt, d_bias, d_offs, lamv, subw, lambda_init, width // LANES, tab)
        wo = w_out[layer].astype(BF16)
        if layer + 1 < depth:
            x, proj = _inproj(x, *layer_weights(layer + 1), merge=(m_att, d_att, wo))
        else:
            x = _outproj(x, m_att, d_att, wo)
    return x
```

```python
import functools
import math

import numpy as np
import jax
import jax.numpy as jnp
from jax import lax
from jax.experimental import pallas as pl
from jax.experimental.pallas import tpu as pltpu

HEAD_DIM = 64
MOBA_BLOCK = 256
MOBA_TOPK = 3
NORM_EPS = 1e-6
NEG_INF = -1e30

LANES = 128
MXU_TILE = 256
ATT_TILE = MOBA_BLOCK
PROJ_ROWS = 1024
LOG2E = math.log2(math.e)
BF16_ROWS = 16
SWEEP_UNROLLS = (8, 4, 2)
MAX_RING_SLOTS = 4

F32 = jnp.float32
BF16 = jnp.bfloat16


def _dot(a, b):
    return jnp.dot(a, b, preferred_element_type=F32)


def _dot_nt(a, b):
    return lax.dot_general(a, b, (((1,), (1,)), ((), ())), preferred_element_type=F32)


def _proj_rows(seq):
    return max(r for r in range(MOBA_BLOCK, PROJ_ROWS + 1, MOBA_BLOCK) if seq % r == 0)


def _alibi_slopes(n_heads):
    return np.asarray(2.0 ** (-8.0 * np.arange(1, n_heads + 1) / n_heads), dtype=np.float32)


def _bias_tables(n_heads, n_tiles):
    slopes = _alibi_slopes(n_heads) * np.float32(LOG2E)
    k_loc = np.arange(ATT_TILE, dtype=np.int32)[:, None]
    q_loc = np.arange(ATT_TILE, dtype=np.int32)[None, :]
    rel = (k_loc - q_loc).astype(np.float32)
    off = slopes[:, None, None] * rel[None]
    diag = np.where((k_loc <= q_loc)[None], off, np.float32(NEG_INF))
    tiles = np.stack([off, diag], axis=1).astype(np.float32)
    d = np.arange(n_tiles, dtype=np.float32)
    offs = -slopes[:, None] * (ATT_TILE * d)[None, :]
    offs = np.broadcast_to(offs[:, :, None], (n_heads, n_tiles, ATT_TILE)).astype(np.float32)
    return jnp.asarray(tiles), jnp.asarray(offs)


def _project_rows(x, t, g_ref, wa_ref, wvt_ref, wgt_ref, gsum_ref, wqm_ref, wkm_ref, wqd_ref, wkd_ref,
                  qm_ref, km_ref, dq_ref, dk_ref, gt_ref, vt_ref, kmean_ref, width):
    rows = x.shape[0]
    blocks = rows // MOBA_BLOCK
    ms = jnp.mean(x * x, axis=-1, keepdims=True)
    h = (x * lax.rsqrt(ms + NORM_EPS) * g_ref[...]).astype(BF16)

    gsum = gsum_ref[...]
    low_half = lax.broadcasted_iota(jnp.int32, (rows, width), 1) % LANES < HEAD_DIM

    def head_norm(y, w):
        sq = (y * y).astype(BF16)
        parts = [_dot(sq[:, c * MXU_TILE:(c + 1) * MXU_TILE], gsum) for c in range(width // MXU_TILE)]
        ssq = jnp.concatenate(parts, axis=1)
        return y * lax.rsqrt(ssq * (1.0 / HEAD_DIM) + NORM_EPS) * w

    def store_split(ref, q):
        ref[0, :, 0:width] = jnp.where(low_half, q, 0.0).astype(BF16)
        ref[0, :, width:2 * width] = jnp.where(low_half, 0.0, q).astype(BF16)

    def silu(y):
        return y * (1.0 / (1.0 + jnp.exp(-y)))

    y = _dot(h, wa_ref[:, 0 * width:1 * width])
    store_split(qm_ref, head_norm(y, wqm_ref[...]))

    y = _dot(h, wa_ref[:, 1 * width:2 * width])
    kn = head_norm(y, wkm_ref[...])
    km_ref[0] = kn.astype(BF16)
    for s in range(blocks):
        kmean_ref[0, pl.ds(t * blocks + s, 1), :] = jnp.mean(
            kn[s * MOBA_BLOCK:(s + 1) * MOBA_BLOCK], axis=0, keepdims=True)

    y = _dot(h, wa_ref[:, 2 * width:3 * width])
    store_split(dq_ref, head_norm(y, wqd_ref[...]))

    y = _dot(h, wa_ref[:, 3 * width:4 * width])
    dk_ref[0] = head_norm(y, wkd_ref[...]).astype(BF16)

    vt = _dot_nt(wvt_ref[...], h)
    gt = silu(_dot_nt(wgt_ref[...], h))
    for s in range(blocks):
        vt_ref[0, s] = vt[:, s * MOBA_BLOCK:(s + 1) * MOBA_BLOCK].astype(BF16)
        gt_ref[0, s] = gt[:, s * MOBA_BLOCK:(s + 1) * MOBA_BLOCK].astype(BF16)


def _merge_rows(x_ref, m_ref, d_ref, w_ref):
    parts = []
    for t in range(m_ref.shape[1]):
        mixed_t = jnp.concatenate([m_ref[0, t], d_ref[0, t]], axis=0)
        y = lax.dot_general(mixed_t, w_ref[...], (((0,), (0,)), ((), ())), preferred_element_type=F32)
        parts.append(x_ref[0, t * ATT_TILE:(t + 1) * ATT_TILE, :] + y)
    return jnp.concatenate(parts, axis=0)


def _inproj_kernel(x_ref, *refs, width):
    _project_rows(x_ref[0], pl.program_id(1), *refs, width)


def _mid_kernel(x_ref, m_ref, d_ref, wo_ref, *refs, width):
    xo_ref, out_refs = refs[9], refs[10:]
    x_new = _merge_rows(x_ref, m_ref, d_ref, wo_ref)
    xo_ref[0] = x_new
    _project_rows(x_new, pl.program_id(1), *refs[:9], *out_refs, width)


def _inproj(x, g, wa, wvt, wgt, gsum, wqm, wkm, wqd, wkd, merge=None):
    B, S, D = x.shape
    width = wa.shape[1] // 4
    rows = _proj_rows(S)
    nblk = S // MOBA_BLOCK
    blocks = rows // MOBA_BLOCK
    full = lambda shape: pl.BlockSpec(shape, lambda b, t: (0,) * len(shape))
    row_spec = lambda w: pl.BlockSpec((1, rows, w), lambda b, t: (b, t, 0))
    act = lambda w: jax.ShapeDtypeStruct((B, S, w), BF16)
    t_spec = pl.BlockSpec((1, blocks, 2 * width, MOBA_BLOCK), lambda b, t: (b, t, 0, 0))
    t_shape = jax.ShapeDtypeStruct((B, nblk, 2 * width, MOBA_BLOCK), BF16)
    weights = (g, wa, wvt, wgt, gsum, wqm, wkm, wqd, wkd)
    in_specs = [row_spec(D)] + [full(w.shape) for w in weights]
    out_specs = [row_spec(2 * width), row_spec(width), row_spec(2 * width), row_spec(width),
                 t_spec, t_spec, pl.BlockSpec((1, nblk, width), lambda b, t: (b, 0, 0))]
    out_shape = [act(2 * width), act(width), act(2 * width), act(width), t_shape, t_shape,
                 jax.ShapeDtypeStruct((B, nblk, width), F32)]
    params = pltpu.CompilerParams(dimension_semantics=("arbitrary", "arbitrary"))
    if merge is None:
        return pl.pallas_call(
            functools.partial(_inproj_kernel, width=width), grid=(B, S // rows),
            in_specs=in_specs, out_specs=out_specs, out_shape=out_shape,
            compiler_params=params, name="inproj")(x, *weights)
    m_att, d_att, w_out = merge
    in_specs = [row_spec(D), t_spec_of(m_att, blocks), t_spec_of(d_att, blocks), full(w_out.shape)] + in_specs[1:]
    outs = pl.pallas_call(
        functools.partial(_mid_kernel, width=width), grid=(B, S // rows),
        in_specs=in_specs, out_specs=[row_spec(D)] + out_specs,
        out_shape=[jax.ShapeDtypeStruct((B, S, D), F32)] + out_shape,
        compiler_params=params, name="outproj_inproj")(x, m_att, d_att, w_out, *weights)
    return outs[0], outs[1:]


def t_spec_of(a, tiles):
    return pl.BlockSpec((1, tiles, a.shape[2], ATT_TILE), lambda b, t: (b, t, 0, 0))


def _pair_table(n_tiles):
    rows = [(i, p, int(p == i // 2)) for i in range(n_tiles) for p in range(i // 2 + 1)]
    rows += [(rows[-1][0], rows[-1][1], 0)] * 2
    return np.ascontiguousarray(np.asarray(rows, dtype=np.int32).T)


def _item(i, t):
    t = jnp.asarray(t, jnp.int32)
    is_diag = t == 0
    j = jnp.where(is_diag, i, t - 1)
    is_pad = jnp.logical_and(t > 0, t - 1 >= i)
    return j, is_diag, is_pad


def _rows(i):
    return pl.ds(pl.multiple_of(i * ATT_TILE, ATT_TILE), ATT_TILE)


def _pad_off(is_pad):
    return jnp.where(is_pad, jnp.float32(NEG_INF), jnp.float32(0.0))


def _sweep_unroll(n_tiles):
    n_pairs = _pair_table(n_tiles).shape[1] - 2
    return next(u for u in SWEEP_UNROLLS if n_pairs % u == 0)


def _ring_slots(unroll):
    return min(unroll, MAX_RING_SLOTS)


def _sweep(tab_ref, n_maps, unroll, score_map, soft, value_map, finish):
    n_pairs = tab_ref.shape[1] - 2
    ring = _ring_slots(unroll)
    assert n_pairs % unroll == 0 and unroll % ring == 0
    pair = lambda g: (tab_ref[0, g], tab_ref[1, g])

    def score_all(g, slot):
        return tuple(x for c in range(n_maps) for x in score_map(*pair(g), slot, c))

    def half_step(g, k, carry):
        colmax, m, alpha = carry
        colmax1 = []
        for c in range(n_maps):
            colmax1 += score_map(*pair(g), k, c)
            value_map(*pair(g - 2), (k - 2) % ring, c, alpha[c])
        m1, alpha1 = soft(*pair(g - 1), (k - 1) % ring, colmax, m)
        return tuple(colmax1), m1, alpha1

    colmax = score_all(0, 0)
    m_init = tuple(jnp.full((1, ATT_TILE), NEG_INF, F32) for _ in range(n_maps))
    m, alpha = soft(*pair(0), 0, colmax, m_init)
    carry = (score_all(1, 1), m, alpha)

    def step(h, carry):
        g = 2 + unroll * h
        for k in range(unroll):
            carry = half_step(g + k, (2 + k) % ring, carry)
        for k in range(unroll):
            @pl.when(tab_ref[2, g + k - 2] == 1)
            def _(k=k):
                finish(tab_ref[0, g + k - 2])
        return carry

    lax.fori_loop(0, n_pairs // unroll, step, carry)


def _score_map(i, p, slot, c, q_ref, k_ref, bias_of, s_ref):
    q = q_ref[0, _rows(i), :]
    items = [_item(i, 2 * p + u) for u in range(2)]
    kt = jnp.concatenate([k_ref[0, _rows(j), :] for j, _, _ in items], axis=0)
    s2 = _dot_nt(kt, q)
    colmax = []
    for u, (_, is_diag, _) in enumerate(items):
        s = s2[u * ATT_TILE:(u + 1) * ATT_TILE] + bias_of(is_diag.astype(jnp.int32))
        s_ref[slot, u, c] = s.astype(BF16)
        colmax.append(jnp.max(s, axis=0, keepdims=True))
    return colmax


def _soft(p, slot, colmax, offs, m, s_ref, p_ref):
    m_out, alpha_out = [], []
    for c in range(len(m)):
        m_old = jnp.where(p == 0, jnp.float32(NEG_INF), m[c])
        m_new = jnp.maximum(m_old, jnp.maximum(colmax[2 * c] + offs[0][c], colmax[2 * c + 1] + offs[1][c]))
        for u in range(2):
            shift = (m_new - offs[u][c]).astype(BF16)
            p_ref[slot, c, u * ATT_TILE:(u + 1) * ATT_TILE, :] = jnp.exp2(s_ref[slot, u, c] - shift)
        m_out.append(m_new)
        alpha_out.append(jnp.exp2(m_old - m_new))
    return tuple(m_out), tuple(alpha_out)


def _value_map(i, slot, c, alpha_c, vts, p_ref, acc_ref):
    vt = jnp.concatenate(vts, axis=1)
    vt1 = jnp.concatenate([vt, jnp.ones((BF16_ROWS, vt.shape[1]), BF16)], axis=0)
    par = i % acc_ref.shape[0]
    acc_ref[par, c] = alpha_c * acc_ref[par, c] + _dot(vt1, p_ref[slot, c])


def _moba_kernel(tab_ref, qa_ref, qb_ref, k_ref, vt_ref, gt_ref, kmean_ref, bias_ref, offs_ref, o_ref,
                 sel_ref, s_ref, p_ref, acc_ref):
    nblk = kmean_ref.shape[1]
    n_tiles = k_ref.shape[1] // ATT_TILE
    q_refs = (qa_ref, qb_ref)
    acc_ref[...] = jnp.zeros(acc_ref.shape, F32)

    km = kmean_ref[0]
    km_hi = km.astype(BF16)
    km_lo = (km - km_hi.astype(F32)).astype(BF16)
    km_parts = jnp.concatenate([km_hi, km_lo], axis=0)
    blk = lax.broadcasted_iota(jnp.int32, (nblk, ATT_TILE), 0)
    for i in range(n_tiles):
        valid = blk < i
        for hh in range(2):
            q = q_refs[hh][0, i * ATT_TILE:(i + 1) * ATT_TILE, :]
            g2 = _dot_nt(km_parts, q)
            g = g2[0:nblk] + g2[nblk:2 * nblk]
            g = jnp.where(valid, g, -jnp.inf)
            sel = blk == i
            for _ in range(min(MOBA_TOPK, nblk - 1)):
                mx = jnp.max(g, axis=0, keepdims=True)
                first = jnp.min(jnp.where(g == mx, blk, nblk), axis=0, keepdims=True)
                pick = blk == first
                sel = jnp.logical_or(sel, jnp.logical_and(pick, valid))
                g = jnp.where(pick, -jnp.inf, g)
            sel_ref[i, hh] = jnp.where(sel, 0.0, NEG_INF).astype(F32)

    def score_map(i, p, slot, hh):
        return _score_map(i, p, slot, hh, q_refs[hh], k_ref, lambda d: bias_ref[hh, d], s_ref)

    def soft(i, p, slot, colmax, m):
        offs = []
        for u in range(2):
            j, _, is_pad = _item(i, 2 * p + u)
            pad = _pad_off(is_pad)
            offs.append([offs_ref[hh, pl.ds(i - j, 1), :] + sel_ref[i, hh, pl.ds(j, 1), :] + pad
                         for hh in range(2)])
        return _soft(p, slot, colmax, offs, m, s_ref, p_ref)

    def value_map(i, p, slot, hh, alpha_c):
        vts = []
        for u in range(2):
            j, _, _ = _item(i, 2 * p + u)
            vts.append(vt_ref[0, j, hh * HEAD_DIM:(hh + 1) * HEAD_DIM, :])
        _value_map(i, slot, hh, alpha_c, vts, p_ref, acc_ref)

    def finish(i):
        par = i % acc_ref.shape[0]
        o_t = jnp.concatenate([acc_ref[par, hh, 0:HEAD_DIM] * (1.0 / acc_ref[par, hh, HEAD_DIM:HEAD_DIM + 1])
                               for hh in range(2)], axis=0)
        o_ref[0, i] = (o_t * gt_ref[0, i].astype(F32)).astype(BF16)

    _sweep(tab_ref, 2, acc_ref.shape[0], score_map, soft, value_map, finish)


def _moba(q_split, km, vt, gt, kmean, bias, offs, tab):
    B, S, width = km.shape
    pairs = width // LANES
    nt = S // ATT_TILE
    nblk = kmean.shape[1]
    unroll = _sweep_unroll(nt)
    ring = _ring_slots(unroll)
    seq_spec = lambda lane_block0: pl.BlockSpec((1, S, LANES), lambda b, p: (b, 0, lane_block0 + p))
    t_spec = pl.BlockSpec((1, nblk, LANES, ATT_TILE), lambda b, p: (b, 0, p, 0))
    return pl.pallas_call(
        _moba_kernel,
        grid=(B, pairs),
        in_specs=[pl.BlockSpec(memory_space=pltpu.SMEM),
                  seq_spec(0), seq_spec(pairs), seq_spec(0), t_spec, t_spec,
                  pl.BlockSpec((1, nblk, LANES), lambda b, p: (b, 0, p)),
                  pl.BlockSpec((2, 2, ATT_TILE, ATT_TILE), lambda b, p: (p, 0, 0, 0)),
                  pl.BlockSpec((2, nt, ATT_TILE), lambda b, p: (p, 0, 0))],
        out_specs=t_spec,
        out_shape=jax.ShapeDtypeStruct((B, nt, width, ATT_TILE), BF16),
        scratch_shapes=[pltpu.VMEM((nt, 2, nblk, ATT_TILE), F32),
                        pltpu.VMEM((ring, 2, 2, ATT_TILE, ATT_TILE), BF16),
                        pltpu.VMEM((ring, 2, 2 * ATT_TILE, ATT_TILE), BF16),
                        pltpu.VMEM((unroll, 2, HEAD_DIM + BF16_ROWS, ATT_TILE), F32)],
        compiler_params=pltpu.CompilerParams(dimension_semantics=("arbitrary", "arbitrary")),
        name="moba_attn",
    )(tab, q_split, q_split, km, vt, gt, kmean, bias, offs)


def _diff_kernel(tab_ref, qa_ref, qb_ref, k_ref, vt_ref, gt_ref, bias_ref, offs_ref, lamv_ref, subw_ref,
                 o_ref, s_ref, p_ref, acc_ref, *, lambda_init):
    q_refs = (qa_ref, qb_ref)
    acc_ref[...] = jnp.zeros(acc_ref.shape, F32)
    lv = lamv_ref[...]
    lam = (jnp.exp(jnp.sum(lv[0:1] * lv[1:2], axis=-1, keepdims=True))
           - jnp.exp(jnp.sum(lv[2:3] * lv[3:4], axis=-1, keepdims=True)) + lambda_init)

    def score_map(i, p, slot, c):
        return _score_map(i, p, slot, c, q_refs[c], k_ref, lambda d: bias_ref[0, d], s_ref)

    def soft(i, p, slot, colmax, m):
        offs = []
        for u in range(2):
            j, _, is_pad = _item(i, 2 * p + u)
            off = offs_ref[0, pl.ds(i - j, 1), :] + _pad_off(is_pad)
            offs.append([off, off])
        return _soft(p, slot, colmax, offs, m, s_ref, p_ref)

    def value_map(i, p, slot, c, alpha_c):
        vts = [vt_ref[0, _item(i, 2 * p + u)[0]] for u in range(2)]
        _value_map(i, slot, c, alpha_c, vts, p_ref, acc_ref)

    def finish(i):
        par = i % acc_ref.shape[0]
        o = [acc_ref[par, c, 0:LANES] * (1.0 / acc_ref[par, c, LANES:LANES + 1]) for c in range(2)]
        o_t = o[0] - lam * o[1]
        ms = jnp.mean(o_t * o_t, axis=0, keepdims=True)
        o_n = o_t * lax.rsqrt(ms + NORM_EPS) * subw_ref[...]
        o_ref[0, i] = (o_n * gt_ref[0, i].astype(F32)).astype(BF16)

    _sweep(tab_ref, 2, acc_ref.shape[0], score_map, soft, value_map, finish)


def _diff(q_split, dk, vt, gt, bias, offs, lamv, subw, lambda_init, v_row_block0, tab):
    B, S, width = dk.shape
    heads = width // LANES
    nt = S // ATT_TILE
    nblk = vt.shape[1]
    unroll = _sweep_unroll(nt)
    ring = _ring_slots(unroll)
    seq_spec = lambda lane_block0: pl.BlockSpec((1, S, LANES), lambda b, h: (b, 0, lane_block0 + h))
    t_spec = lambda row_block0: pl.BlockSpec((1, nblk, LANES, ATT_TILE), lambda b, h: (b, 0, row_block0 + h, 0))
    return pl.pallas_call(
        functools.partial(_diff_kernel, lambda_init=lambda_init),
        grid=(B, heads),
        in_specs=[pl.BlockSpec(memory_space=pltpu.SMEM),
                  seq_spec(0), seq_spec(heads), seq_spec(0), t_spec(v_row_block0), t_spec(v_row_block0),
                  pl.BlockSpec((1, 2, ATT_TILE, ATT_TILE), lambda b, h: (h, 0, 0, 0)),
                  pl.BlockSpec((1, nt, ATT_TILE), lambda b, h: (h, 0, 0)),
                  pl.BlockSpec(lamv.shape, lambda b, h: (0, 0)),
                  pl.BlockSpec((LANES, ATT_TILE), lambda b, h: (0, 0))],
        out_specs=t_spec(0),
        out_shape=jax.ShapeDtypeStruct((B, nt, width, ATT_TILE), BF16),
        scratch_shapes=[pltpu.VMEM((ring, 2, 2, ATT_TILE, ATT_TILE), BF16),
                        pltpu.VMEM((ring, 2, 2 * ATT_TILE, ATT_TILE), BF16),
                        pltpu.VMEM((unroll, 2, LANES + BF16_ROWS, ATT_TILE), F32)],
        compiler_params=pltpu.CompilerParams(dimension_semantics=("arbitrary", "arbitrary")),
        name="diff_attn",
    )(tab, q_split, q_split, dk, vt, gt, bias, offs, lamv, subw)


def _outproj_kernel(x_ref, m_ref, d_ref, w_ref, o_ref):
    o_ref[0] = _merge_rows(x_ref, m_ref, d_ref, w_ref)


def _outproj(x, m_att, d_att, w_out):
    B, S, D = x.shape
    rows = _proj_rows(S)
    tiles = rows // ATT_TILE
    row_spec = pl.BlockSpec((1, rows, D), lambda b, t: (b, t, 0))
    return pl.pallas_call(
        _outproj_kernel,
        grid=(B, S // rows),
        in_specs=[row_spec, t_spec_of(m_att, tiles), t_spec_of(d_att, tiles),
                  pl.BlockSpec(w_out.shape, lambda b, t: (0, 0))],
        out_specs=row_spec,
        out_shape=jax.ShapeDtypeStruct((B, S, D), F32),
        compiler_params=pltpu.CompilerParams(dimension_semantics=("arbitrary", "arbitrary")),
        name="outproj",
    )(x, m_att, d_att, w_out)


def kernel(x, norm_g, w_in, moba_q_norm, moba_k_norm, diff_q_norm, diff_k_norm,
           lambda_q1, lambda_k1, lambda_q2, lambda_k2, diff_subln, w_out):
    B, S, D = x.shape
    depth = w_in.shape[0]
    width = D // 2
    moba_heads = width // HEAD_DIM
    diff_heads = width // (2 * HEAD_DIM)
    assert S % MOBA_BLOCK == 0 and width % MXU_TILE == 0 and S // MOBA_BLOCK > 1
    nt = S // ATT_TILE
    scale = HEAD_DIM ** -0.5 * LOG2E

    m_bias, m_offs = _bias_tables(moba_heads, nt)
    d_bias, d_offs = _bias_tables(diff_heads, nt)
    r = lax.broadcasted_iota(jnp.int32, (MXU_TILE, MXU_TILE), 0) // HEAD_DIM
    c = lax.broadcasted_iota(jnp.int32, (MXU_TILE, MXU_TILE), 1) // HEAD_DIM
    gsum = (r == c).astype(BF16)
    reps = width // HEAD_DIM
    tab = jnp.asarray(_pair_table(nt))

    def layer_weights(layer):
        w = w_in[layer]
        cols = [w[:, k * width:(k + 1) * width] for k in range(8)]
        wa = jnp.concatenate([cols[0], cols[1], cols[4], cols[5]], axis=1).astype(BF16)
        wvt = jnp.concatenate([cols[2], cols[6]], axis=1).astype(BF16).T
        wgt = jnp.concatenate([cols[3], cols[7]], axis=1).astype(BF16).T
        tile_w = lambda v, s: (jnp.tile(v, reps) * s)[None, :].astype(F32)
        return (norm_g[layer][None, :], wa, wvt, wgt, gsum,
                tile_w(moba_q_norm[layer], scale), tile_w(moba_k_norm[layer], 1.0),
                tile_w(diff_q_norm[layer], scale), tile_w(diff_k_norm[layer], 1.0))

    proj = _inproj(x, *layer_weights(0))
    for layer in range(depth):
        qm, km, dq, dk, gt, vt, kmean = proj
        m_att = _moba(qm, km, vt, gt, kmean, m_bias, m_offs, tab)
        lambda_init = 0.8 - 0.6 * math.exp(-0.3 * layer)
        lamv = jnp.stack([lambda_q1[layer], lambda_k1[layer], lambda_q2[layer], lambda_k2[layer]])
        subw = jnp.broadcast_to((diff_subln[layer] * (1.0 - lambda_init))[:, None],
                                (2 * HEAD_DIM, ATT_TILE)).astype(F32)
        d_att = _diff(dq, dk, vt, gt, d_bias, d_offs, lamv, subw, lambda_init, width // LANES, tab)
        wo = w_out[layer].astype(BF16)
        if layer + 1 < depth:
            x, proj = _inproj(x, *layer_weights(layer + 1), merge=(m_att, d_att, wo))
        else:
            x = _outproj(x, m_att, d_att, wo)
    return x
```

```python
import functools
import math

import numpy as np
import jax
import jax.numpy as jnp
from jax import lax
from jax.experimental import pallas as pl
from jax.experimental.pallas import tpu as pltpu

HEAD_DIM = 64
MOBA_BLOCK = 256
MOBA_TOPK = 3
NORM_EPS = 1e-6
NEG_INF = -1e30

LANES = 128
MXU_TILE = 256
ATT_TILE = MOBA_BLOCK
PROJ_ROWS = 1024
LOG2E = math.log2(math.e)
BF16_ROWS = 16
SWEEP_UNROLLS = (8, 4, 2)
MAX_RING_SLOTS = 8

F32 = jnp.float32
BF16 = jnp.bfloat16


def _dot(a, b):
    return jnp.dot(a, b, preferred_element_type=F32)


def _dot_nt(a, b):
    return lax.dot_general(a, b, (((1,), (1,)), ((), ())), preferred_element_type=F32)


def _proj_rows(seq):
    return max(r for r in range(MOBA_BLOCK, PROJ_ROWS + 1, MOBA_BLOCK) if seq % r == 0)


def _alibi_slopes(n_heads):
    return np.asarray(2.0 ** (-8.0 * np.arange(1, n_heads + 1) / n_heads), dtype=np.float32)


def _bias_tables(n_heads, n_tiles):
    slopes = _alibi_slopes(n_heads) * np.float32(LOG2E)
    k_loc = np.arange(ATT_TILE, dtype=np.int32)[:, None]
    q_loc = np.arange(ATT_TILE, dtype=np.int32)[None, :]
    rel = (k_loc - q_loc).astype(np.float32)
    off = slopes[:, None, None] * rel[None]
    diag = np.where((k_loc <= q_loc)[None], off, np.float32(NEG_INF))
    tiles = np.stack([off, diag], axis=1).astype(np.float32)
    d = np.arange(n_tiles, dtype=np.float32)
    offs = -slopes[:, None] * (ATT_TILE * d)[None, :]
    offs = np.broadcast_to(offs[:, :, None], (n_heads, n_tiles, ATT_TILE)).astype(np.float32)
    return jnp.asarray(tiles), jnp.asarray(offs)


def _project_rows(x, t, g_ref, wa_ref, wvt_ref, wgt_ref, gsum_ref, wqm_ref, wkm_ref, wqd_ref, wkd_ref,
                  qm_ref, km_ref, dq_ref, dk_ref, gt_ref, vt_ref, kmean_ref, width):
    rows = x.shape[0]
    blocks = rows // MOBA_BLOCK
    ms = jnp.mean(x * x, axis=-1, keepdims=True)
    h = (x * lax.rsqrt(ms + NORM_EPS) * g_ref[...]).astype(BF16)

    gsum = gsum_ref[...]
    low_half = lax.broadcasted_iota(jnp.int32, (rows, width), 1) % LANES < HEAD_DIM

    def head_norm(y, w):
        sq = (y * y).astype(BF16)
        parts = [_dot(sq[:, c * MXU_TILE:(c + 1) * MXU_TILE], gsum) for c in range(width // MXU_TILE)]
        ssq = jnp.concatenate(parts, axis=1)
        return y * lax.rsqrt(ssq * (1.0 / HEAD_DIM) + NORM_EPS) * w

    def store_split(ref, q):
        ref[0, :, 0:width] = jnp.where(low_half, q, 0.0).astype(BF16)
        ref[0, :, width:2 * width] = jnp.where(low_half, 0.0, q).astype(BF16)

    def silu(y):
        return y * (1.0 / (1.0 + jnp.exp(-y)))

    y = _dot(h, wa_ref[:, 0 * width:1 * width])
    store_split(qm_ref, head_norm(y, wqm_ref[...]))

    y = _dot(h, wa_ref[:, 1 * width:2 * width])
    kn = head_norm(y, wkm_ref[...])
    km_ref[0] = kn.astype(BF16)
    for s in range(blocks):
        kmean_ref[0, pl.ds(t * blocks + s, 1), :] = jnp.mean(
            kn[s * MOBA_BLOCK:(s + 1) * MOBA_BLOCK], axis=0, keepdims=True)

    y = _dot(h, wa_ref[:, 2 * width:3 * width])
    store_split(dq_ref, head_norm(y, wqd_ref[...]))

    y = _dot(h, wa_ref[:, 3 * width:4 * width])
    dk_ref[0] = head_norm(y, wkd_ref[...]).astype(BF16)

    vt = _dot_nt(wvt_ref[...], h)
    gt = silu(_dot_nt(wgt_ref[...], h))
    for s in range(blocks):
        vt_ref[0, s] = vt[:, s * MOBA_BLOCK:(s + 1) * MOBA_BLOCK].astype(BF16)
        gt_ref[0, s] = gt[:, s * MOBA_BLOCK:(s + 1) * MOBA_BLOCK].astype(BF16)


def _merge_rows(x_ref, m_ref, d_ref, w_ref):
    parts = []
    for t in range(m_ref.shape[1]):
        mixed_t = jnp.concatenate([m_ref[0, t], d_ref[0, t]], axis=0)
        y = lax.dot_general(mixed_t, w_ref[...], (((0,), (0,)), ((), ())), preferred_element_type=F32)
        parts.append(x_ref[0, t * ATT_TILE:(t + 1) * ATT_TILE, :] + y)
    return jnp.concatenate(parts, axis=0)


def _inproj_kernel(x_ref, *refs, width):
    _project_rows(x_ref[0], pl.program_id(1), *refs, width)


def _mid_kernel(x_ref, m_ref, d_ref, wo_ref, *refs, width):
    xo_ref, out_refs = refs[9], refs[10:]
    x_new = _merge_rows(x_ref, m_ref, d_ref, wo_ref)
    xo_ref[0] = x_new
    _project_rows(x_new, pl.program_id(1), *refs[:9], *out_refs, width)


def _inproj(x, g, wa, wvt, wgt, gsum, wqm, wkm, wqd, wkd, merge=None):
    B, S, D = x.shape
    width = wa.shape[1] // 4
    rows = _proj_rows(S)
    nblk = S // MOBA_BLOCK
    blocks = rows // MOBA_BLOCK
    full = lambda shape: pl.BlockSpec(shape, lambda b, t: (0,) * len(shape))
    row_spec = lambda w: pl.BlockSpec((1, rows, w), lambda b, t: (b, t, 0))
    act = lambda w: jax.ShapeDtypeStruct((B, S, w), BF16)
    t_spec = pl.BlockSpec((1, blocks, 2 * width, MOBA_BLOCK), lambda b, t: (b, t, 0, 0))
    t_shape = jax.ShapeDtypeStruct((B, nblk, 2 * width, MOBA_BLOCK), BF16)
    weights = (g, wa, wvt, wgt, gsum, wqm, wkm, wqd, wkd)
    in_specs = [row_spec(D)] + [full(w.shape) for w in weights]
    out_specs = [row_spec(2 * width), row_spec(width), row_spec(2 * width), row_spec(width),
                 t_spec, t_spec, pl.BlockSpec((1, nblk, width), lambda b, t: (b, 0, 0))]
    out_shape = [act(2 * width), act(width), act(2 * width), act(width), t_shape, t_shape,
                 jax.ShapeDtypeStruct((B, nblk, width), F32)]
    params = pltpu.CompilerParams(dimension_semantics=("arbitrary", "arbitrary"))
    if merge is None:
        return pl.pallas_call(
            functools.partial(_inproj_kernel, width=width), grid=(B, S // rows),
            in_specs=in_specs, out_specs=out_specs, out_shape=out_shape,
            compiler_params=params, name="inproj")(x, *weights)
    m_att, d_att, w_out = merge
    in_specs = [row_spec(D), t_spec_of(m_att, blocks), t_spec_of(d_att, blocks), full(w_out.shape)] + in_specs[1:]
    outs = pl.pallas_call(
        functools.partial(_mid_kernel, width=width), grid=(B, S // rows),
        in_specs=in_specs, out_specs=[row_spec(D)] + out_specs,
        out_shape=[jax.ShapeDtypeStruct((B, S, D), F32)] + out_shape,
        compiler_params=params, name="outproj_inproj")(x, m_att, d_att, w_out, *weights)
    return outs[0], outs[1:]


def t_spec_of(a, tiles):
    return pl.BlockSpec((1, tiles, a.shape[2], ATT_TILE), lambda b, t: (b, t, 0, 0))


def _pair_table(n_tiles):
    rows = [(i, p, int(p == i // 2)) for i in range(n_tiles) for p in range(i // 2 + 1)]
    rows += [(rows[-1][0], rows[-1][1], 0)] * 2
    return np.ascontiguousarray(np.asarray(rows, dtype=np.int32).T)


def _item(i, t):
    t = jnp.asarray(t, jnp.int32)
    is_diag = t == 0
    j = jnp.where(is_diag, i, t - 1)
    is_pad = jnp.logical_and(t > 0, t - 1 >= i)
    return j, is_diag, is_pad


def _rows(i):
    return pl.ds(pl.multiple_of(i * ATT_TILE, ATT_TILE), ATT_TILE)


def _pad_off(is_pad):
    return jnp.where(is_pad, jnp.float32(NEG_INF), jnp.float32(0.0))


def _sweep_unroll(n_tiles):
    n_pairs = _pair_table(n_tiles).shape[1] - 2
    return next(u for u in SWEEP_UNROLLS if n_pairs % u == 0)


def _ring_slots(unroll):
    return min(unroll, MAX_RING_SLOTS)


def _sweep(tab_ref, n_maps, unroll, score_map, soft, value_map, finish):
    n_pairs = tab_ref.shape[1] - 2
    ring = _ring_slots(unroll)
    assert n_pairs % unroll == 0 and unroll % ring == 0
    pair = lambda g: (tab_ref[0, g], tab_ref[1, g])

    def score_all(g, slot):
        return tuple(x for c in range(n_maps) for x in score_map(*pair(g), slot, c))

    def half_step(g, k, carry):
        colmax, m, alpha = carry
        colmax1 = []
        for c in range(n_maps):
            colmax1 += score_map(*pair(g), k, c)
            value_map(*pair(g - 2), (k - 2) % ring, c, alpha[c])
        m1, alpha1 = soft(*pair(g - 1), (k - 1) % ring, colmax, m)
        return tuple(colmax1), m1, alpha1

    colmax = score_all(0, 0)
    m_init = tuple(jnp.full((1, ATT_TILE), NEG_INF, F32) for _ in range(n_maps))
    m, alpha = soft(*pair(0), 0, colmax, m_init)
    carry = (score_all(1, 1), m, alpha)

    def step(h, carry):
        g = 2 + unroll * h
        for k in range(unroll):
            carry = half_step(g + k, (2 + k) % ring, carry)
        for k in range(unroll):
            @pl.when(tab_ref[2, g + k - 2] == 1)
            def _(k=k):
                finish(tab_ref[0, g + k - 2])
        return carry

    lax.fori_loop(0, n_pairs // unroll, step, carry)


def _score_map(i, p, slot, c, q_ref, k_ref, bias_of, s_ref):
    q = q_ref[0, _rows(i), :]
    items = [_item(i, 2 * p + u) for u in range(2)]
    kt = jnp.concatenate([k_ref[0, _rows(j), :] for j, _, _ in items], axis=0)
    s2 = _dot_nt(kt, q)
    colmax = []
    for u, (_, is_diag, _) in enumerate(items):
        s = s2[u * ATT_TILE:(u + 1) * ATT_TILE] + bias_of(is_diag.astype(jnp.int32))
        s_ref[slot, u, c] = s.astype(BF16)
        colmax.append(jnp.max(s, axis=0, keepdims=True))
    return colmax


def _soft(p, slot, colmax, offs, m, s_ref, p_ref):
    m_out, alpha_out = [], []
    for c in range(len(m)):
        m_old = jnp.where(p == 0, jnp.float32(NEG_INF), m[c])
        m_new = jnp.maximum(m_old, jnp.maximum(colmax[2 * c] + offs[0][c], colmax[2 * c + 1] + offs[1][c]))
        for u in range(2):
            shift = (m_new - offs[u][c]).astype(BF16)
            p_ref[slot, c, u * ATT_TILE:(u + 1) * ATT_TILE, :] = jnp.exp2(s_ref[slot, u, c] - shift)
        m_out.append(m_new)
        alpha_out.append(jnp.exp2(m_old - m_new))
    return tuple(m_out), tuple(alpha_out)


def _value_map(i, slot, c, alpha_c, vts, p_ref, acc_ref):
    vt = jnp.concatenate(vts, axis=1)
    vt1 = jnp.concatenate([vt, jnp.ones((BF16_ROWS, vt.shape[1]), BF16)], axis=0)
    par = i % acc_ref.shape[0]
    acc_ref[par, c] = alpha_c * acc_ref[par, c] + _dot(vt1, p_ref[slot, c])


def _moba_kernel(tab_ref, qa_ref, qb_ref, k_ref, vt_ref, gt_ref, kmean_ref, bias_ref, offs_ref, o_ref,
                 sel_ref, s_ref, p_ref, acc_ref):
    nblk = kmean_ref.shape[1]
    n_tiles = k_ref.shape[1] // ATT_TILE
    q_refs = (qa_ref, qb_ref)
    acc_ref[...] = jnp.zeros(acc_ref.shape, F32)

    km = kmean_ref[0]
    km_hi = km.astype(BF16)
    km_lo = (km - km_hi.astype(F32)).astype(BF16)
    km_parts = jnp.concatenate([km_hi, km_lo], axis=0)
    blk = lax.broadcasted_iota(jnp.int32, (nblk, ATT_TILE), 0)
    for i in range(n_tiles):
        valid = blk < i
        for hh in range(2):
            q = q_refs[hh][0, i * ATT_TILE:(i + 1) * ATT_TILE, :]
            g2 = _dot_nt(km_parts, q)
            g = g2[0:nblk] + g2[nblk:2 * nblk]
            g = jnp.where(valid, g, -jnp.inf)
            sel = blk == i
            for _ in range(min(MOBA_TOPK, nblk - 1)):
                mx = jnp.max(g, axis=0, keepdims=True)
                first = jnp.min(jnp.where(g == mx, blk, nblk), axis=0, keepdims=True)
                pick = blk == first
                sel = jnp.logical_or(sel, jnp.logical_and(pick, valid))
                g = jnp.where(pick, -jnp.inf, g)
            sel_ref[i, hh] = jnp.where(sel, 0.0, NEG_INF).astype(F32)

    def score_map(i, p, slot, hh):
        return _score_map(i, p, slot, hh, q_refs[hh], k_ref, lambda d: bias_ref[hh, d], s_ref)

    def soft(i, p, slot, colmax, m):
        offs = []
        for u in range(2):
            j, _, is_pad = _item(i, 2 * p + u)
            pad = _pad_off(is_pad)
            offs.append([offs_ref[hh, pl.ds(i - j, 1), :] + sel_ref[i, hh, pl.ds(j, 1), :] + pad
                         for hh in range(2)])
        return _soft(p, slot, colmax, offs, m, s_ref, p_ref)

    def value_map(i, p, slot, hh, alpha_c):
        vts = []
        for u in range(2):
            j, _, _ = _item(i, 2 * p + u)
            vts.append(vt_ref[0, j, hh * HEAD_DIM:(hh + 1) * HEAD_DIM, :])
        _value_map(i, slot, hh, alpha_c, vts, p_ref, acc_ref)

    def finish(i):
        par = i % acc_ref.shape[0]
        o_t = jnp.concatenate([acc_ref[par, hh, 0:HEAD_DIM] * (1.0 / acc_ref[par, hh, HEAD_DIM:HEAD_DIM + 1])
                               for hh in range(2)], axis=0)
        o_ref[0, i] = (o_t * gt_ref[0, i].astype(F32)).astype(BF16)

    _sweep(tab_ref, 2, acc_ref.shape[0], score_map, soft, value_map, finish)


def _moba(q_split, km, vt, gt, kmean, bias, offs, tab):
    B, S, width = km.shape
    pairs = width // LANES
    nt = S // ATT_TILE
    nblk = kmean.shape[1]
    unroll = _sweep_unroll(nt)
    ring = _ring_slots(unroll)
    seq_spec = lambda lane_block0: pl.BlockSpec((1, S, LANES), lambda b, p: (b, 0, lane_block0 + p))
    t_spec = pl.BlockSpec((1, nblk, LANES, ATT_TILE), lambda b, p: (b, 0, p, 0))
    return pl.pallas_call(
        _moba_kernel,
        grid=(B, pairs),
        in_specs=[pl.BlockSpec(memory_space=pltpu.SMEM),
                  seq_spec(0), seq_spec(pairs), seq_spec(0), t_spec, t_spec,
                  pl.BlockSpec((1, nblk, LANES), lambda b, p: (b, 0, p)),
                  pl.BlockSpec((2, 2, ATT_TILE, ATT_TILE), lambda b, p: (p, 0, 0, 0)),
                  pl.BlockSpec((2, nt, ATT_TILE), lambda b, p: (p, 0, 0))],
        out_specs=t_spec,
        out_shape=jax.ShapeDtypeStruct((B, nt, width, ATT_TILE), BF16),
        scratch_shapes=[pltpu.VMEM((nt, 2, nblk, ATT_TILE), F32),
                        pltpu.VMEM((ring, 2, 2, ATT_TILE, ATT_TILE), BF16),
                        pltpu.VMEM((ring, 2, 2 * ATT_TILE, ATT_TILE), BF16),
                        pltpu.VMEM((unroll, 2, HEAD_DIM + BF16_ROWS, ATT_TILE), F32)],
        compiler_params=pltpu.CompilerParams(dimension_semantics=("arbitrary", "arbitrary")),
        name="moba_attn",
    )(tab, q_split, q_split, km, vt, gt, kmean, bias, offs)


def _diff_kernel(tab_ref, qa_ref, qb_ref, k_ref, vt_ref, gt_ref, bias_ref, offs_ref, lamv_ref, subw_ref,
                 o_ref, s_ref, p_ref, acc_ref, *, lambda_init):
    q_refs = (qa_ref, qb_ref)
    acc_ref[...] = jnp.zeros(acc_ref.shape, F32)
    lv = lamv_ref[...]
    lam = (jnp.exp(jnp.sum(lv[0:1] * lv[1:2], axis=-1, keepdims=True))
           - jnp.exp(jnp.sum(lv[2:3] * lv[3:4], axis=-1, keepdims=True)) + lambda_init)

    def score_map(i, p, slot, c):
        return _score_map(i, p, slot, c, q_refs[c], k_ref, lambda d: bias_ref[0, d], s_ref)

    def soft(i, p, slot, colmax, m):
        offs = []
        for u in range(2):
            j, _, is_pad = _item(i, 2 * p + u)
            off = offs_ref[0, pl.ds(i - j, 1), :] + _pad_off(is_pad)
            offs.append([off, off])
        return _soft(p, slot, colmax, offs, m, s_ref, p_ref)

    def value_map(i, p, slot, c, alpha_c):
        vts = [vt_ref[0, _item(i, 2 * p + u)[0]] for u in range(2)]
        _value_map(i, slot, c, alpha_c, vts, p_ref, acc_ref)

    def finish(i):
        par = i % acc_ref.shape[0]
        o = [acc_ref[par, c, 0:LANES] * (1.0 / acc_ref[par, c, LANES:LANES + 1]) for c in range(2)]
        o_t = o[0] - lam * o[1]
        ms = jnp.mean(o_t * o_t, axis=0, keepdims=True)
        o_n = o_t * lax.rsqrt(ms + NORM_EPS) * subw_ref[...]
        o_ref[0, i] = (o_n * gt_ref[0, i].astype(F32)).astype(BF16)

    _sweep(tab_ref, 2, acc_ref.shape[0], score_map, soft, value_map, finish)


def _diff(q_split, dk, vt, gt, bias, offs, lamv, subw, lambda_init, v_row_block0, tab):
    B, S, width = dk.shape
    heads = width // LANES
    nt = S // ATT_TILE
    nblk = vt.shape[1]
    unroll = _sweep_unroll(nt)
    ring = _ring_slots(unroll)
    seq_spec = lambda lane_block0: pl.BlockSpec((1, S, LANES), lambda b, h: (b, 0, lane_block0 + h))
    t_spec = lambda row_block0: pl.BlockSpec((1, nblk, LANES, ATT_TILE), lambda b, h: (b, 0, row_block0 + h, 0))
    return pl.pallas_call(
        functools.partial(_diff_kernel, lambda_init=lambda_init),
        grid=(B, heads),
        in_specs=[pl.BlockSpec(memory_space=pltpu.SMEM),
                  seq_spec(0), seq_spec(heads), seq_spec(0), t_spec(v_row_block0), t_spec(v_row_block0),
                  pl.BlockSpec((1, 2, ATT_TILE, ATT_TILE), lambda b, h: (h, 0, 0, 0)),
                  pl.BlockSpec((1, nt, ATT_TILE), lambda b, h: (h, 0, 0)),
                  pl.BlockSpec(lamv.shape, lambda b, h: (0, 0)),
                  pl.BlockSpec((LANES, ATT_TILE), lambda b, h: (0, 0))],
        out_specs=t_spec(0),
        out_shape=jax.ShapeDtypeStruct((B, nt, width, ATT_TILE), BF16),
        scratch_shapes=[pltpu.VMEM((ring, 2, 2, ATT_TILE, ATT_TILE), BF16),
                        pltpu.VMEM((ring, 2, 2 * ATT_TILE, ATT_TILE), BF16),
                        pltpu.VMEM((unroll, 2, LANES + BF16_ROWS, ATT_TILE), F32)],
        compiler_params=pltpu.CompilerParams(dimension_semantics=("arbitrary", "arbitrary")),
        name="diff_attn",
    )(tab, q_split, q_split, dk, vt, gt, bias, offs, lamv, subw)


def _outproj_kernel(x_ref, m_ref, d_ref, w_ref, o_ref):
    o_ref[0] = _merge_rows(x_ref, m_ref, d_ref, w_ref)


def _outproj(x, m_att, d_att, w_out):
    B, S, D = x.shape
    rows = _proj_rows(S)
    tiles = rows // ATT_TILE
    row_spec = pl.BlockSpec((1, rows, D), lambda b, t: (b, t, 0))
    return pl.pallas_call(
        _outproj_kernel,
        grid=(B, S // rows),
        in_specs=[row_spec, t_spec_of(m_att, tiles), t_spec_of(d_att, tiles),
                  pl.BlockSpec(w_out.shape, lambda b, t: (0, 0))],
        out_specs=row_spec,
        out_shape=jax.ShapeDtypeStruct((B, S, D), F32),
        compiler_params=pltpu.CompilerParams(dimension_semantics=("arbitrary", "arbitrary")),
        name="outproj",
    )(x, m_att, d_att, w_out)


def kernel(x, norm_g, w_in, moba_q_norm, moba_k_norm, diff_q_norm, diff_k_norm,
           lambda_q1, lambda_k1, lambda_q2, lambda_k2, diff_subln, w_out):
    B, S, D = x.shape
    depth = w_in.shape[0]
    width = D // 2
    moba_heads = width // HEAD_DIM
    diff_heads = width // (2 * HEAD_DIM)
    assert S % MOBA_BLOCK == 0 and width % MXU_TILE == 0 and S // MOBA_BLOCK > 1
    nt = S // ATT_TILE
    scale = HEAD_DIM ** -0.5 * LOG2E

    m_bias, m_offs = _bias_tables(moba_heads, nt)
    d_bias, d_offs = _bias_tables(diff_heads, nt)
    r = lax.broadcasted_iota(jnp.int32, (MXU_TILE, MXU_TILE), 0) // HEAD_DIM
    c = lax.broadcasted_iota(jnp.int32, (MXU_TILE, MXU_TILE), 1) // HEAD_DIM
    gsum = (r == c).astype(BF16)
    reps = width // HEAD_DIM
    tab = jnp.asarray(_pair_table(nt))

    def layer_weights(layer):
        w = w_in[layer]
        cols = [w[:, k * width:(k + 1) * width] for k in range(8)]
        wa = jnp.concatenate([cols[0], cols[1], cols[4], cols[5]], axis=1).astype(BF16)
        wvt = jnp.concatenate([cols[2], cols[6]], axis=1).astype(BF16).T
        wgt = jnp.concatenate([cols[3], cols[7]], axis=1).astype(BF16).T
        tile_w = lambda v, s: (jnp.tile(v, reps) * s)[None, :].astype(F32)
        return (norm_g[layer][None, :], wa, wvt, wgt, gsum,
                tile_w(moba_q_norm[layer], scale), tile_w(moba_k_norm[layer], 1.0),
                tile_w(diff_q_norm[layer], scale), tile_w(diff_k_norm[layer], 1.0))

    proj = _inproj(x, *layer_weights(0))
    for layer in range(depth):
        qm, km, dq, dk, gt, vt, kmean = proj
        m_att = _moba(qm, km, vt, gt, kmean, m_bias, m_offs, tab)
        lambda_init = 0.8 - 0.6 * math.exp(-0.3 * layer)
        lamv = jnp.stack([lambda_q1[layer], lambda_k1[layer], lambda_q2[layer], lambda_k2[layer]])
        subw = jnp.broadcast_to((diff_subln[layer] * (1.0 - lambda_init))[:, None],
                                (2 * HEAD_DIM, ATT_TILE)).astype(F32)
        d_att = _diff(dq, dk, vt, gt, d_bias, d_offs, lamv, subw, lambda_init, width // LANES, tab)
        wo = w_out[layer].astype(BF16)
        if layer + 1 < depth:
            x, proj = _inproj(x, *layer_weights(layer + 1), merge=(m_att, d_att, wo))
        else:
            x = _outproj(x, m_att, d_att, wo)
    return x
```

```python
import functools
import math

import numpy as np
import jax
import jax.numpy as jnp
from jax import lax
from jax.experimental import pallas as pl
from jax.experimental.pallas import tpu as pltpu

HEAD_DIM = 64
MOBA_BLOCK = 256
MOBA_TOPK = 3
NORM_EPS = 1e-6
NEG_INF = -1e30

LANES = 128
MXU_TILE = 256
ATT_TILE = MOBA_BLOCK
PROJ_ROWS = 1024
LOG2E = math.log2(math.e)
BF16_ROWS = 16
SWEEP_UNROLLS = (8, 4, 2)
MAX_RING_SLOTS = 8

F32 = jnp.float32
BF16 = jnp.bfloat16


def _dot(a, b):
    return jnp.dot(a, b, preferred_element_type=F32)


def _dot_nt(a, b):
    return lax.dot_general(a, b, (((1,), (1,)), ((), ())), preferred_element_type=F32)


def _proj_rows(seq):
    return max(r for r in range(MOBA_BLOCK, PROJ_ROWS + 1, MOBA_BLOCK) if seq % r == 0)


def _alibi_slopes(n_heads):
    return np.asarray(2.0 ** (-8.0 * np.arange(1, n_heads + 1) / n_heads), dtype=np.float32)


def _bias_tables(n_heads, n_tiles):
    slopes = _alibi_slopes(n_heads) * np.float32(LOG2E)
    k_loc = np.arange(ATT_TILE, dtype=np.int32)[:, None]
    q_loc = np.arange(ATT_TILE, dtype=np.int32)[None, :]
    rel = (k_loc - q_loc).astype(np.float32)
    off = slopes[:, None, None] * rel[None]
    diag = np.where((k_loc <= q_loc)[None], off, np.float32(NEG_INF))
    tiles = np.stack([off, diag], axis=1).astype(np.float32)
    d = np.arange(n_tiles, dtype=np.float32)
    offs = -slopes[:, None] * (ATT_TILE * d)[None, :]
    offs = np.broadcast_to(offs[:, :, None], (n_heads, n_tiles, ATT_TILE)).astype(np.float32)
    return jnp.asarray(tiles), jnp.asarray(offs)


def _second_item_tables(n_heads, operand_of_head):
    slopes = _alibi_slopes(n_heads) * np.float32(LOG2E)
    loc = np.arange(ATT_TILE, dtype=np.float32)
    feat = np.zeros((n_heads, 2 * ATT_TILE, LANES), np.float32)
    for h in range(n_heads):
        lane0 = HEAD_DIM if operand_of_head[h] == 0 else 0
        col = slopes[h] * loc
        hi = col.astype(BF16).astype(np.float32)
        feat[h, ATT_TILE:, lane0] = hi
        feat[h, ATT_TILE:, lane0 + 1] = col - hi
    qrow = (-slopes[:, None] * loc[None, :])[:, None, :].astype(np.float32)
    return jnp.asarray(feat.astype(BF16)), jnp.asarray(qrow)


def _project_rows(x, t, g_ref, wa_ref, wvt_ref, wgt_ref, gsum_ref, wqm_ref, wkm_ref, wqd_ref, wkd_ref,
                  qm_ref, km_ref, dq_ref, dk_ref, gt_ref, vt_ref, kmean_ref, width):
    rows = x.shape[0]
    blocks = rows // MOBA_BLOCK
    ms = jnp.mean(x * x, axis=-1, keepdims=True)
    h = (x * lax.rsqrt(ms + NORM_EPS) * g_ref[...]).astype(BF16)

    gsum = gsum_ref[...]
    low_half = lax.broadcasted_iota(jnp.int32, (rows, width), 1) % LANES < HEAD_DIM

    def head_norm(y, w):
        sq = (y * y).astype(BF16)
        parts = [_dot(sq[:, c * MXU_TILE:(c + 1) * MXU_TILE], gsum) for c in range(width // MXU_TILE)]
        ssq = jnp.concatenate(parts, axis=1)
        return y * lax.rsqrt(ssq * (1.0 / HEAD_DIM) + NORM_EPS) * w

    lane128 = lax.broadcasted_iota(jnp.int32, (rows, width), 1) % LANES
    ones_a = jnp.where(lane128 // 2 == HEAD_DIM // 2, 1.0, 0.0)
    ones_b = jnp.where(lane128 // 2 == 0, 1.0, 0.0)

    def store_split(ref, q):
        ref[0, :, 0:width] = jnp.where(low_half, q, ones_a).astype(BF16)
        ref[0, :, width:2 * width] = jnp.where(low_half, ones_b, q).astype(BF16)

    def silu(y):
        return y * (1.0 / (1.0 + jnp.exp(-y)))

    y = _dot(h, wa_ref[:, 0 * width:1 * width])
    store_split(qm_ref, head_norm(y, wqm_ref[...]))

    y = _dot(h, wa_ref[:, 1 * width:2 * width])
    kn = head_norm(y, wkm_ref[...])
    km_ref[0] = kn.astype(BF16)
    for s in range(blocks):
        kmean_ref[0, pl.ds(t * blocks + s, 1), :] = jnp.mean(
            kn[s * MOBA_BLOCK:(s + 1) * MOBA_BLOCK], axis=0, keepdims=True)

    y = _dot(h, wa_ref[:, 2 * width:3 * width])
    store_split(dq_ref, head_norm(y, wqd_ref[...]))

    y = _dot(h, wa_ref[:, 3 * width:4 * width])
    dk_ref[0] = head_norm(y, wkd_ref[...]).astype(BF16)

    vt = _dot_nt(wvt_ref[...], h)
    gt = silu(_dot_nt(wgt_ref[...], h))
    for s in range(blocks):
        vt_ref[0, s] = vt[:, s * MOBA_BLOCK:(s + 1) * MOBA_BLOCK].astype(BF16)
        gt_ref[0, s] = gt[:, s * MOBA_BLOCK:(s + 1) * MOBA_BLOCK].astype(BF16)


def _merge_rows(x_ref, m_ref, d_ref, w_ref):
    parts = []
    for t in range(m_ref.shape[1]):
        mixed_t = jnp.concatenate([m_ref[0, t], d_ref[0, t]], axis=0)
        y = lax.dot_general(mixed_t, w_ref[...], (((0,), (0,)), ((), ())), preferred_element_type=F32)
        parts.append(x_ref[0, t * ATT_TILE:(t + 1) * ATT_TILE, :] + y)
    return jnp.concatenate(parts, axis=0)


def _inproj_kernel(x_ref, *refs, width):
    _project_rows(x_ref[0], pl.program_id(1), *refs, width)


def _mid_kernel(x_ref, m_ref, d_ref, wo_ref, *refs, width):
    xo_ref, out_refs = refs[9], refs[10:]
    x_new = _merge_rows(x_ref, m_ref, d_ref, wo_ref)
    xo_ref[0] = x_new
    _project_rows(x_new, pl.program_id(1), *refs[:9], *out_refs, width)


def _inproj(x, g, wa, wvt, wgt, gsum, wqm, wkm, wqd, wkd, merge=None):
    B, S, D = x.shape
    width = wa.shape[1] // 4
    rows = _proj_rows(S)
    nblk = S // MOBA_BLOCK
    blocks = rows // MOBA_BLOCK
    full = lambda shape: pl.BlockSpec(shape, lambda b, t: (0,) * len(shape))
    row_spec = lambda w: pl.BlockSpec((1, rows, w), lambda b, t: (b, t, 0))
    act = lambda w: jax.ShapeDtypeStruct((B, S, w), BF16)
    t_spec = pl.BlockSpec((1, blocks, 2 * width, MOBA_BLOCK), lambda b, t: (b, t, 0, 0))
    t_shape = jax.ShapeDtypeStruct((B, nblk, 2 * width, MOBA_BLOCK), BF16)
    weights = (g, wa, wvt, wgt, gsum, wqm, wkm, wqd, wkd)
    in_specs = [row_spec(D)] + [full(w.shape) for w in weights]
    out_specs = [row_spec(2 * width), row_spec(width), row_spec(2 * width), row_spec(width),
                 t_spec, t_spec, pl.BlockSpec((1, nblk, width), lambda b, t: (b, 0, 0))]
    out_shape = [act(2 * width), act(width), act(2 * width), act(width), t_shape, t_shape,
                 jax.ShapeDtypeStruct((B, nblk, width), F32)]
    params = pltpu.CompilerParams(dimension_semantics=("arbitrary", "arbitrary"))
    if merge is None:
        return pl.pallas_call(
            functools.partial(_inproj_kernel, width=width), grid=(B, S // rows),
            in_specs=in_specs, out_specs=out_specs, out_shape=out_shape,
            compiler_params=params, name="inproj")(x, *weights)
    m_att, d_att, w_out = merge
    in_specs = [row_spec(D), t_spec_of(m_att, blocks), t_spec_of(d_att, blocks), full(w_out.shape)] + in_specs[1:]
    outs = pl.pallas_call(
        functools.partial(_mid_kernel, width=width), grid=(B, S // rows),
        in_specs=in_specs, out_specs=[row_spec(D)] + out_specs,
        out_shape=[jax.ShapeDtypeStruct((B, S, D), F32)] + out_shape,
        compiler_params=params, name="outproj_inproj")(x, m_att, d_att, w_out, *weights)
    return outs[0], outs[1:]


def t_spec_of(a, tiles):
    return pl.BlockSpec((1, tiles, a.shape[2], ATT_TILE), lambda b, t: (b, t, 0, 0))


def _pair_table(n_tiles):
    rows = [(i, p, int(p == i // 2)) for i in range(n_tiles) for p in range(i // 2 + 1)]
    rows += [(rows[-1][0], rows[-1][1], 0)] * 2
    return np.ascontiguousarray(np.asarray(rows, dtype=np.int32).T)


def _item(i, t):
    t = jnp.asarray(t, jnp.int32)
    is_diag = t == 0
    j = jnp.where(is_diag, i, t - 1)
    is_pad = jnp.logical_and(t > 0, t - 1 >= i)
    return j, is_diag, is_pad


def _rows(i):
    return pl.ds(pl.multiple_of(i * ATT_TILE, ATT_TILE), ATT_TILE)


def _pad_off(is_pad):
    return jnp.where(is_pad, jnp.float32(NEG_INF), jnp.float32(0.0))


def _sweep_unroll(n_tiles):
    n_pairs = _pair_table(n_tiles).shape[1] - 2
    return next(u for u in SWEEP_UNROLLS if n_pairs % u == 0)


def _ring_slots(unroll):
    return min(unroll, MAX_RING_SLOTS)


def _sweep(tab_ref, n_maps, unroll, score_map, soft, value_map, finish):
    n_pairs = tab_ref.shape[1] - 2
    ring = _ring_slots(unroll)
    assert n_pairs % unroll == 0 and unroll % ring == 0
    pair = lambda g: (tab_ref[0, g], tab_ref[1, g])

    def score_all(g, slot):
        return tuple(x for c in range(n_maps) for x in score_map(*pair(g), slot, c))

    def half_step(g, k, carry):
        colmax, m, alpha = carry
        colmax1 = []
        for c in range(n_maps):
            colmax1 += score_map(*pair(g), k, c)
            value_map(*pair(g - 2), (k - 2) % ring, c, alpha[c])
        m1, alpha1 = soft(*pair(g - 1), (k - 1) % ring, colmax, m)
        return tuple(colmax1), m1, alpha1

    colmax = score_all(0, 0)
    m_init = tuple(jnp.full((1, ATT_TILE), NEG_INF, F32) for _ in range(n_maps))
    m, alpha = soft(*pair(0), 0, colmax, m_init)
    carry = (score_all(1, 1), m, alpha)

    def step(h, carry):
        g = 2 + unroll * h
        for k in range(unroll):
            carry = half_step(g + k, (2 + k) % ring, carry)
        for k in range(unroll):
            @pl.when(tab_ref[2, g + k - 2] == 1)
            def _(k=k):
                finish(tab_ref[0, g + k - 2])
        return carry

    lax.fori_loop(0, n_pairs // unroll, step, carry)


def _score_map(i, p, slot, c, q_ref, k_ref, feat, real_lanes, bias_of, s_ref):
    q = q_ref[0, _rows(i), :]
    items = [_item(i, 2 * p + u) for u in range(2)]
    kt = jnp.concatenate([k_ref[0, _rows(j), :] for j, _, _ in items], axis=0)
    s2 = _dot_nt(jnp.where(real_lanes, kt, feat), q)
    s_first = s2[0:ATT_TILE] + bias_of(items[0][1].astype(jnp.int32))
    colmax = []
    for u, s in enumerate((s_first, s2[ATT_TILE:])):
        s_ref[slot, u, c] = s.astype(BF16)
        colmax.append(jnp.max(s, axis=0, keepdims=True))
    return colmax


def _soft(p, slot, colmax, offs, m, s_ref, p_ref):
    m_out, alpha_out = [], []
    for c in range(len(m)):
        m_old = jnp.where(p == 0, jnp.float32(NEG_INF), m[c])
        m_new = jnp.maximum(m_old, jnp.maximum(colmax[2 * c] + offs[0][c], colmax[2 * c + 1] + offs[1][c]))
        for u in range(2):
            shift = (m_new - offs[u][c]).astype(BF16)
            p_ref[slot, c, u * ATT_TILE:(u + 1) * ATT_TILE, :] = jnp.exp2(s_ref[slot, u, c] - shift)
        m_out.append(m_new)
        alpha_out.append(jnp.exp2(m_old - m_new))
    return tuple(m_out), tuple(alpha_out)


def _value_map(i, slot, c, alpha_c, vts, p_ref, acc_ref):
    vt = jnp.concatenate(vts, axis=1)
    vt1 = jnp.concatenate([vt, jnp.ones((BF16_ROWS, vt.shape[1]), BF16)], axis=0)
    par = i % acc_ref.shape[0]
    acc_ref[par, c] = alpha_c * acc_ref[par, c] + _dot(vt1, p_ref[slot, c])


def _moba_kernel(tab_ref, qa_ref, qb_ref, k_ref, vt_ref, gt_ref, kmean_ref, bias_ref, offs_ref, feat_ref,
                 qrow_ref, o_ref, sel_ref, s_ref, p_ref, acc_ref):
    nblk = kmean_ref.shape[1]
    n_tiles = k_ref.shape[1] // ATT_TILE
    q_refs = (qa_ref, qb_ref)
    acc_ref[...] = jnp.zeros(acc_ref.shape, F32)

    km = kmean_ref[0]
    km_lane = lax.broadcasted_iota(jnp.int32, km.shape, 1)
    lane = lax.broadcasted_iota(jnp.int32, (2 * ATT_TILE, LANES), 1)
    real_lanes = [lane < HEAD_DIM, lane >= HEAD_DIM]
    km_parts = []
    for hh in range(2):
        km_h = jnp.where((km_lane < HEAD_DIM) == (hh == 0), km, 0.0)
        km_hi = km_h.astype(BF16)
        km_lo = (km_h - km_hi.astype(F32)).astype(BF16)
        km_parts.append(jnp.concatenate([km_hi, km_lo], axis=0))
    blk = lax.broadcasted_iota(jnp.int32, (nblk, ATT_TILE), 0)
    for i in range(n_tiles):
        valid = blk < i
        for hh in range(2):
            q = q_refs[hh][0, i * ATT_TILE:(i + 1) * ATT_TILE, :]
            g2 = _dot_nt(km_parts[hh], q)
            g = g2[0:nblk] + g2[nblk:2 * nblk]
            g = jnp.where(valid, g, -jnp.inf)
            sel = blk == i
            for _ in range(min(MOBA_TOPK, nblk - 1)):
                mx = jnp.max(g, axis=0, keepdims=True)
                first = jnp.min(jnp.where(g == mx, blk, nblk), axis=0, keepdims=True)
                pick = blk == first
                sel = jnp.logical_or(sel, jnp.logical_and(pick, valid))
                g = jnp.where(pick, -jnp.inf, g)
            sel_ref[i, hh] = jnp.where(sel, 0.0, NEG_INF).astype(F32)

    def score_map(i, p, slot, hh):
        return _score_map(i, p, slot, hh, q_refs[hh], k_ref, feat_ref[hh], real_lanes[hh],
                          lambda d: bias_ref[hh, d], s_ref)

    def soft(i, p, slot, colmax, m):
        offs = []
        for u in range(2):
            j, _, is_pad = _item(i, 2 * p + u)
            pad = _pad_off(is_pad)
            offs.append([offs_ref[hh, pl.ds(i - j, 1), :] + sel_ref[i, hh, pl.ds(j, 1), :] + pad
                         + (qrow_ref[hh] if u == 1 else 0.0) for hh in range(2)])
        return _soft(p, slot, colmax, offs, m, s_ref, p_ref)

    def value_map(i, p, slot, hh, alpha_c):
        vts = []
        for u in range(2):
            j, _, _ = _item(i, 2 * p + u)
            vts.append(vt_ref[0, j, hh * HEAD_DIM:(hh + 1) * HEAD_DIM, :])
        _value_map(i, slot, hh, alpha_c, vts, p_ref, acc_ref)

    def finish(i):
        par = i % acc_ref.shape[0]
        o_t = jnp.concatenate([acc_ref[par, hh, 0:HEAD_DIM] * (1.0 / acc_ref[par, hh, HEAD_DIM:HEAD_DIM + 1])
                               for hh in range(2)], axis=0)
        o_ref[0, i] = (o_t * gt_ref[0, i].astype(F32)).astype(BF16)

    _sweep(tab_ref, 2, acc_ref.shape[0], score_map, soft, value_map, finish)


def _moba(q_split, km, vt, gt, kmean, bias, offs, feat, qrow, tab):
    B, S, width = km.shape
    pairs = width // LANES
    nt = S // ATT_TILE
    nblk = kmean.shape[1]
    unroll = _sweep_unroll(nt)
    ring = _ring_slots(unroll)
    seq_spec = lambda lane_block0: pl.BlockSpec((1, S, LANES), lambda b, p: (b, 0, lane_block0 + p))
    t_spec = pl.BlockSpec((1, nblk, LANES, ATT_TILE), lambda b, p: (b, 0, p, 0))
    return pl.pallas_call(
        _moba_kernel,
        grid=(B, pairs),
        in_specs=[pl.BlockSpec(memory_space=pltpu.SMEM),
                  seq_spec(0), seq_spec(pairs), seq_spec(0), t_spec, t_spec,
                  pl.BlockSpec((1, nblk, LANES), lambda b, p: (b, 0, p)),
                  pl.BlockSpec((2, 2, ATT_TILE, ATT_TILE), lambda b, p: (p, 0, 0, 0)),
                  pl.BlockSpec((2, nt, ATT_TILE), lambda b, p: (p, 0, 0)),
                  pl.BlockSpec((2, 2 * ATT_TILE, LANES), lambda b, p: (p, 0, 0)),
                  pl.BlockSpec((2, 1, ATT_TILE), lambda b, p: (p, 0, 0))],
        out_specs=t_spec,
        out_shape=jax.ShapeDtypeStruct((B, nt, width, ATT_TILE), BF16),
        scratch_shapes=[pltpu.VMEM((nt, 2, nblk, ATT_TILE), F32),
                        pltpu.VMEM((ring, 2, 2, ATT_TILE, ATT_TILE), BF16),
                        pltpu.VMEM((ring, 2, 2 * ATT_TILE, ATT_TILE), BF16),
                        pltpu.VMEM((unroll, 2, HEAD_DIM + BF16_ROWS, ATT_TILE), F32)],
        compiler_params=pltpu.CompilerParams(dimension_semantics=("arbitrary", "arbitrary")),
        name="moba_attn",
    )(tab, q_split, q_split, km, vt, gt, kmean, bias, offs, feat, qrow)


def _diff_kernel(tab_ref, qa_ref, qb_ref, k_ref, vt_ref, gt_ref, bias_ref, offs_ref, feat_ref, qrow_ref,
                 lamv_ref, subw_ref, o_ref, s_ref, p_ref, acc_ref, *, lambda_init):
    q_refs = (qa_ref, qb_ref)
    lane = lax.broadcasted_iota(jnp.int32, (2 * ATT_TILE, LANES), 1)
    real_lanes = [lane < HEAD_DIM, lane >= HEAD_DIM]
    acc_ref[...] = jnp.zeros(acc_ref.shape, F32)
    lv = lamv_ref[...]
    lam = (jnp.exp(jnp.sum(lv[0:1] * lv[1:2], axis=-1, keepdims=True))
           - jnp.exp(jnp.sum(lv[2:3] * lv[3:4], axis=-1, keepdims=True)) + lambda_init)

    def score_map(i, p, slot, c):
        return _score_map(i, p, slot, c, q_refs[c], k_ref, feat_ref[0, c], real_lanes[c],
                          lambda d: bias_ref[0, d], s_ref)

    def soft(i, p, slot, colmax, m):
        offs = []
        for u in range(2):
            j, _, is_pad = _item(i, 2 * p + u)
            off = offs_ref[0, pl.ds(i - j, 1), :] + _pad_off(is_pad) + (qrow_ref[0] if u == 1 else 0.0)
            offs.append([off, off])
        return _soft(p, slot, colmax, offs, m, s_ref, p_ref)

    def value_map(i, p, slot, c, alpha_c):
        vts = [vt_ref[0, _item(i, 2 * p + u)[0]] for u in range(2)]
        _value_map(i, slot, c, alpha_c, vts, p_ref, acc_ref)

    def finish(i):
        par = i % acc_ref.shape[0]
        o = [acc_ref[par, c, 0:LANES] * (1.0 / acc_ref[par, c, LANES:LANES + 1]) for c in range(2)]
        o_t = o[0] - lam * o[1]
        ms = jnp.mean(o_t * o_t, axis=0, keepdims=True)
        o_n = o_t * lax.rsqrt(ms + NORM_EPS) * subw_ref[...]
        o_ref[0, i] = (o_n * gt_ref[0, i].astype(F32)).astype(BF16)

    _sweep(tab_ref, 2, acc_ref.shape[0], score_map, soft, value_map, finish)


def _diff(q_split, dk, vt, gt, bias, offs, feat, qrow, lamv, subw, lambda_init, v_row_block0, tab):
    B, S, width = dk.shape
    heads = width // LANES
    nt = S // ATT_TILE
    nblk = vt.shape[1]
    unroll = _sweep_unroll(nt)
    ring = _ring_slots(unroll)
    seq_spec = lambda lane_block0: pl.BlockSpec((1, S, LANES), lambda b, h: (b, 0, lane_block0 + h))
    t_spec = lambda row_block0: pl.BlockSpec((1, nblk, LANES, ATT_TILE), lambda b, h: (b, 0, row_block0 + h, 0))
    return pl.pallas_call(
        functools.partial(_diff_kernel, lambda_init=lambda_init),
        grid=(B, heads),
        in_specs=[pl.BlockSpec(memory_space=pltpu.SMEM),
                  seq_spec(0), seq_spec(heads), seq_spec(0), t_spec(v_row_block0), t_spec(v_row_block0),
                  pl.BlockSpec((1, 2, ATT_TILE, ATT_TILE), lambda b, h: (h, 0, 0, 0)),
                  pl.BlockSpec((1, nt, ATT_TILE), lambda b, h: (h, 0, 0)),
                  pl.BlockSpec((1, 2, 2 * ATT_TILE, LANES), lambda b, h: (h, 0, 0, 0)),
                  pl.BlockSpec((1, 1, ATT_TILE), lambda b, h: (h, 0, 0)),
                  pl.BlockSpec(lamv.shape, lambda b, h: (0, 0)),
                  pl.BlockSpec((LANES, ATT_TILE), lambda b, h: (0, 0))],
        out_specs=t_spec(0),
        out_shape=jax.ShapeDtypeStruct((B, nt, width, ATT_TILE), BF16),
        scratch_shapes=[pltpu.VMEM((ring, 2, 2, ATT_TILE, ATT_TILE), BF16),
                        pltpu.VMEM((ring, 2, 2 * ATT_TILE, ATT_TILE), BF16),
                        pltpu.VMEM((unroll, 2, LANES + BF16_ROWS, ATT_TILE), F32)],
        compiler_params=pltpu.CompilerParams(dimension_semantics=("arbitrary", "arbitrary")),
        name="diff_attn",
    )(tab, q_split, q_split, dk, vt, gt, bias, offs, feat, qrow, lamv, subw)


def _outproj_kernel(x_ref, m_ref, d_ref, w_ref, o_ref):
    o_ref[0] = _merge_rows(x_ref, m_ref, d_ref, w_ref)


def _outproj(x, m_att, d_att, w_out):
    B, S, D = x.shape
    rows = _proj_rows(S)
    tiles = rows // ATT_TILE
    row_spec = pl.BlockSpec((1, rows, D), lambda b, t: (b, t, 0))
    return pl.pallas_call(
        _outproj_kernel,
        grid=(B, S // rows),
        in_specs=[row_spec, t_spec_of(m_att, tiles), t_spec_of(d_att, tiles),
                  pl.BlockSpec(w_out.shape, lambda b, t: (0, 0))],
        out_specs=row_spec,
        out_shape=jax.ShapeDtypeStruct((B, S, D), F32),
        compiler_params=pltpu.CompilerParams(dimension_semantics=("arbitrary", "arbitrary")),
        name="outproj",
    )(x, m_att, d_att, w_out)


def kernel(x, norm_g, w_in, moba_q_norm, moba_k_norm, diff_q_norm, diff_k_norm,
           lambda_q1, lambda_k1, lambda_q2, lambda_k2, diff_subln, w_out):
    B, S, D = x.shape
    depth = w_in.shape[0]
    width = D // 2
    moba_heads = width // HEAD_DIM
    diff_heads = width // (2 * HEAD_DIM)
    assert S % MOBA_BLOCK == 0 and width % MXU_TILE == 0 and S // MOBA_BLOCK > 1
    nt = S // ATT_TILE
    scale = HEAD_DIM ** -0.5 * LOG2E

    m_bias, m_offs = _bias_tables(moba_heads, nt)
    d_bias, d_offs = _bias_tables(diff_heads, nt)
    m_feat, m_qrow = _second_item_tables(moba_heads, [h % 2 for h in range(moba_heads)])
    d_feat = [_second_item_tables(diff_heads, [c] * diff_heads) for c in range(2)]
    d_feat, d_qrow = jnp.stack([d_feat[0][0], d_feat[1][0]], axis=1), d_feat[0][1]
    r = lax.broadcasted_iota(jnp.int32, (MXU_TILE, MXU_TILE), 0) // HEAD_DIM
    c = lax.broadcasted_iota(jnp.int32, (MXU_TILE, MXU_TILE), 1) // HEAD_DIM
    gsum = (r == c).astype(BF16)
    reps = width // HEAD_DIM
    tab = jnp.asarray(_pair_table(nt))

    def layer_weights(layer):
        w = w_in[layer]
        cols = [w[:, k * width:(k + 1) * width] for k in range(8)]
        wa = jnp.concatenate([cols[0], cols[1], cols[4], cols[5]], axis=1).astype(BF16)
        wvt = jnp.concatenate([cols[2], cols[6]], axis=1).astype(BF16).T
        wgt = jnp.concatenate([cols[3], cols[7]], axis=1).astype(BF16).T
        tile_w = lambda v, s: (jnp.tile(v, reps) * s)[None, :].astype(F32)
        return (norm_g[layer][None, :], wa, wvt, wgt, gsum,
                tile_w(moba_q_norm[layer], scale), tile_w(moba_k_norm[layer], 1.0),
                tile_w(diff_q_norm[layer], scale), tile_w(diff_k_norm[layer], 1.0))

    proj = _inproj(x, *layer_weights(0))
    for layer in range(depth):
        qm, km, dq, dk, gt, vt, kmean = proj
        m_att = _moba(qm, km, vt, gt, kmean, m_bias, m_offs, m_feat, m_qrow, tab)
        lambda_init = 0.8 - 0.6 * math.exp(-0.3 * layer)
        lamv = jnp.stack([lambda_q1[layer], lambda_k1[layer], lambda_q2[layer], lambda_k2[layer]])
        subw = jnp.broadcast_to((diff_subln[layer] * (1.0 - lambda_init))[:, None],
                                (2 * HEAD_DIM, ATT_TILE)).astype(F32)
        d_att = _diff(dq, dk, vt, gt, d_bias, d_offs, d_feat, d_qrow, lamv, subw, lambda_init, width // LANES, tab)
        wo = w_out[layer].astype(BF16)
        if layer + 1 < depth:
            x, proj = _inproj(x, *layer_weights(layer + 1), merge=(m_att, d_att, wo))
        else:
            x = _outproj(x, m_att, d_att, wo)
    return x
```

```python
import functools
import math

import numpy as np
import jax
import jax.numpy as jnp
from jax import lax
from jax.experimental import pallas as pl
from jax.experimental.pallas import tpu as pltpu

HEAD_DIM = 64
MOBA_BLOCK = 256
MOBA_TOPK = 3
NORM_EPS = 1e-6
NEG_INF = -1e30

LANES = 128
MXU_TILE = 256
ATT_TILE = MOBA_BLOCK
PROJ_ROWS = 1024
LOG2E = math.log2(math.e)
BF16_ROWS = 16
SWEEP_UNROLLS = (12, 8, 4, 2)
MAX_RING_SLOTS = 12

F32 = jnp.float32
BF16 = jnp.bfloat16


def _dot(a, b):
    return jnp.dot(a, b, preferred_element_type=F32)


def _dot_nt(a, b):
    return lax.dot_general(a, b, (((1,), (1,)), ((), ())), preferred_element_type=F32)


def _proj_rows(seq):
    return max(r for r in range(MOBA_BLOCK, PROJ_ROWS + 1, MOBA_BLOCK) if seq % r == 0)


def _alibi_slopes(n_heads):
    return np.asarray(2.0 ** (-8.0 * np.arange(1, n_heads + 1) / n_heads), dtype=np.float32)


def _bias_tables(n_heads, n_tiles):
    slopes = _alibi_slopes(n_heads) * np.float32(LOG2E)
    k_loc = np.arange(ATT_TILE, dtype=np.int32)[:, None]
    q_loc = np.arange(ATT_TILE, dtype=np.int32)[None, :]
    rel = (k_loc - q_loc).astype(np.float32)
    off = slopes[:, None, None] * rel[None]
    diag = np.where((k_loc <= q_loc)[None], off, np.float32(NEG_INF))
    tiles = np.stack([off, diag], axis=1).astype(np.float32)
    d = np.arange(n_tiles, dtype=np.float32)
    offs = -slopes[:, None] * (ATT_TILE * d)[None, :]
    offs = np.broadcast_to(offs[:, :, None], (n_heads, n_tiles, ATT_TILE)).astype(np.float32)
    return jnp.asarray(tiles), jnp.asarray(offs)


def _project_rows(x, t, g_ref, wa_ref, wvt_ref, wgt_ref, gsum_ref, wqm_ref, wkm_ref, wqd_ref, wkd_ref,
                  qm_ref, km_ref, dq_ref, dk_ref, gt_ref, vt_ref, kmean_ref, width):
    rows = x.shape[0]
    blocks = rows // MOBA_BLOCK
    ms = jnp.mean(x * x, axis=-1, keepdims=True)
    h = (x * lax.rsqrt(ms + NORM_EPS) * g_ref[...]).astype(BF16)

    gsum = gsum_ref[...]
    low_half = lax.broadcasted_iota(jnp.int32, (rows, width), 1) % LANES < HEAD_DIM

    def head_norm(y, w):
        sq = (y * y).astype(BF16)
        parts = [_dot(sq[:, c * MXU_TILE:(c + 1) * MXU_TILE], gsum) for c in range(width // MXU_TILE)]
        ssq = jnp.concatenate(parts, axis=1)
        return y * lax.rsqrt(ssq * (1.0 / HEAD_DIM) + NORM_EPS) * w

    def store_split(ref, q):
        ref[0, :, 0:width] = jnp.where(low_half, q, 0.0).astype(BF16)
        ref[0, :, width:2 * width] = jnp.where(low_half, 0.0, q).astype(BF16)

    def silu(y):
        return y * (1.0 / (1.0 + jnp.exp(-y)))

    y = _dot(h, wa_ref[:, 0 * width:1 * width])
    store_split(qm_ref, head_norm(y, wqm_ref[...]))

    y = _dot(h, wa_ref[:, 1 * width:2 * width])
    kn = head_norm(y, wkm_ref[...])
    km_ref[0] = kn.astype(BF16)
    for s in range(blocks):
        kmean_ref[0, pl.ds(t * blocks + s, 1), :] = jnp.mean(
            kn[s * MOBA_BLOCK:(s + 1) * MOBA_BLOCK], axis=0, keepdims=True)

    y = _dot(h, wa_ref[:, 2 * width:3 * width])
    store_split(dq_ref, head_norm(y, wqd_ref[...]))

    y = _dot(h, wa_ref[:, 3 * width:4 * width])
    dk_ref[0] = head_norm(y, wkd_ref[...]).astype(BF16)

    vt = _dot_nt(wvt_ref[...], h)
    gt = silu(_dot_nt(wgt_ref[...], h))
    for s in range(blocks):
        vt_ref[0, s] = vt[:, s * MOBA_BLOCK:(s + 1) * MOBA_BLOCK].astype(BF16)
        gt_ref[0, s] = gt[:, s * MOBA_BLOCK:(s + 1) * MOBA_BLOCK].astype(BF16)


def _merge_rows(x_ref, m_ref, d_ref, w_ref):
    parts = []
    for t in range(m_ref.shape[1]):
        mixed_t = jnp.concatenate([m_ref[0, t], d_ref[0, t]], axis=0)
        y = lax.dot_general(mixed_t, w_ref[...], (((0,), (0,)), ((), ())), preferred_element_type=F32)
        parts.append(x_ref[0, t * ATT_TILE:(t + 1) * ATT_TILE, :] + y)
    return jnp.concatenate(parts, axis=0)


def _inproj_kernel(x_ref, *refs, width):
    _project_rows(x_ref[0], pl.program_id(1), *refs, width)


def _mid_kernel(x_ref, m_ref, d_ref, wo_ref, *refs, width):
    xo_ref, out_refs = refs[9], refs[10:]
    x_new = _merge_rows(x_ref, m_ref, d_ref, wo_ref)
    xo_ref[0] = x_new
    _project_rows(x_new, pl.program_id(1), *refs[:9], *out_refs, width)


def _inproj(x, g, wa, wvt, wgt, gsum, wqm, wkm, wqd, wkd, merge=None):
    B, S, D = x.shape
    width = wa.shape[1] // 4
    rows = _proj_rows(S)
    nblk = S // MOBA_BLOCK
    blocks = rows // MOBA_BLOCK
    full = lambda shape: pl.BlockSpec(shape, lambda b, t: (0,) * len(shape))
    row_spec = lambda w: pl.BlockSpec((1, rows, w), lambda b, t: (b, t, 0))
    act = lambda w: jax.ShapeDtypeStruct((B, S, w), BF16)
    t_spec = pl.BlockSpec((1, blocks, 2 * width, MOBA_BLOCK), lambda b, t: (b, t, 0, 0))
    t_shape = jax.ShapeDtypeStruct((B, nblk, 2 * width, MOBA_BLOCK), BF16)
    weights = (g, wa, wvt, wgt, gsum, wqm, wkm, wqd, wkd)
    in_specs = [row_spec(D)] + [full(w.shape) for w in weights]
    out_specs = [row_spec(2 * width), row_spec(width), row_spec(2 * width), row_spec(width),
                 t_spec, t_spec, pl.BlockSpec((1, nblk, width), lambda b, t: (b, 0, 0))]
    out_shape = [act(2 * width), act(width), act(2 * width), act(width), t_shape, t_shape,
                 jax.ShapeDtypeStruct((B, nblk, width), F32)]
    params = pltpu.CompilerParams(dimension_semantics=("arbitrary", "arbitrary"))
    if merge is None:
        return pl.pallas_call(
            functools.partial(_inproj_kernel, width=width), grid=(B, S // rows),
            in_specs=in_specs, out_specs=out_specs, out_shape=out_shape,
            compiler_params=params, name="inproj")(x, *weights)
    m_att, d_att, w_out = merge
    in_specs = [row_spec(D), t_spec_of(m_att, blocks), t_spec_of(d_att, blocks), full(w_out.shape)] + in_specs[1:]
    outs = pl.pallas_call(
        functools.partial(_mid_kernel, width=width), grid=(B, S // rows),
        in_specs=in_specs, out_specs=[row_spec(D)] + out_specs,
        out_shape=[jax.ShapeDtypeStruct((B, S, D), F32)] + out_shape,
        compiler_params=params, name="outproj_inproj")(x, m_att, d_att, w_out, *weights)
    return outs[0], outs[1:]


def t_spec_of(a, tiles):
    return pl.BlockSpec((1, tiles, a.shape[2], ATT_TILE), lambda b, t: (b, t, 0, 0))


def _pair_table(n_tiles):
    rows = [(i, p, int(p == i // 2)) for i in range(n_tiles) for p in range(i // 2 + 1)]
    rows += [(rows[-1][0], rows[-1][1], 0)] * 2
    return np.ascontiguousarray(np.asarray(rows, dtype=np.int32).T)


def _item(i, t):
    t = jnp.asarray(t, jnp.int32)
    is_diag = t == 0
    j = jnp.where(is_diag, i, t - 1)
    is_pad = jnp.logical_and(t > 0, t - 1 >= i)
    return j, is_diag, is_pad


def _rows(i):
    return pl.ds(pl.multiple_of(i * ATT_TILE, ATT_TILE), ATT_TILE)


def _pad_off(is_pad):
    return jnp.where(is_pad, jnp.float32(NEG_INF), jnp.float32(0.0))


def _sweep_unroll(n_tiles):
    n_pairs = _pair_table(n_tiles).shape[1] - 2
    return next(u for u in SWEEP_UNROLLS if n_pairs % u == 0)


def _ring_slots(unroll):
    return min(unroll, MAX_RING_SLOTS)


def _sweep(tab_ref, n_maps, unroll, score_map, soft, value_map, finish):
    n_pairs = tab_ref.shape[1] - 2
    ring = _ring_slots(unroll)
    assert n_pairs % unroll == 0 and unroll % ring == 0
    pair = lambda g: (tab_ref[0, g], tab_ref[1, g])

    def score_all(g, slot):
        return tuple(x for c in range(n_maps) for x in score_map(*pair(g), slot, c))

    def half_step(g, k, carry):
        colmax, m, alpha = carry
        colmax1 = []
        for c in range(n_maps):
            colmax1 += score_map(*pair(g), k, c)
            value_map(*pair(g - 2), (k - 2) % ring, c, alpha[c])
        m1, alpha1 = soft(*pair(g - 1), (k - 1) % ring, colmax, m)
        return tuple(colmax1), m1, alpha1

    colmax = score_all(0, 0)
    m_init = tuple(jnp.full((1, ATT_TILE), NEG_INF, F32) for _ in range(n_maps))
    m, alpha = soft(*pair(0), 0, colmax, m_init)
    carry = (score_all(1, 1), m, alpha)

    def step(h, carry):
        g = 2 + unroll * h
        for k in range(unroll):
            carry = half_step(g + k, (2 + k) % ring, carry)
        for k in range(unroll):
            @pl.when(tab_ref[2, g + k - 2] == 1)
            def _(k=k):
                finish(tab_ref[0, g + k - 2])
        return carry

    lax.fori_loop(0, n_pairs // unroll, step, carry)


def _score_map(i, p, slot, c, q_ref, k_ref, bias_of, s_ref):
    q = q_ref[0, _rows(i), :]
    items = [_item(i, 2 * p + u) for u in range(2)]
    kt = jnp.concatenate([k_ref[0, _rows(j), :] for j, _, _ in items], axis=0)
    s2 = _dot_nt(kt, q)
    colmax = []
    for u, (_, is_diag, _) in enumerate(items):
        s = s2[u * ATT_TILE:(u + 1) * ATT_TILE] + bias_of(is_diag.astype(jnp.int32))
        s_ref[slot, u, c] = s.astype(BF16)
        colmax.append(jnp.max(s, axis=0, keepdims=True))
    return colmax


def _soft(p, slot, colmax, offs, m, s_ref, p_ref):
    m_out, alpha_out = [], []
    for c in range(len(m)):
        m_old = jnp.where(p == 0, jnp.float32(NEG_INF), m[c])
        m_new = jnp.maximum(m_old, jnp.maximum(colmax[2 * c] + offs[0][c], colmax[2 * c + 1] + offs[1][c]))
        for u in range(2):
            shift = (m_new - offs[u][c]).astype(BF16)
            p_ref[slot, c, u * ATT_TILE:(u + 1) * ATT_TILE, :] = jnp.exp2(s_ref[slot, u, c] - shift)
        m_out.append(m_new)
        alpha_out.append(jnp.exp2(m_old - m_new))
    return tuple(m_out), tuple(alpha_out)


def _value_map(i, slot, c, alpha_c, vts, p_ref, acc_ref):
    vt = jnp.concatenate(vts, axis=1)
    vt1 = jnp.concatenate([vt, jnp.ones((BF16_ROWS, vt.shape[1]), BF16)], axis=0)
    par = i % acc_ref.shape[0]
    acc_ref[par, c] = alpha_c * acc_ref[par, c] + _dot(vt1, p_ref[slot, c])


def _moba_kernel(tab_ref, qa_ref, qb_ref, k_ref, vt_ref, gt_ref, kmean_ref, bias_ref, offs_ref, o_ref,
                 sel_ref, s_ref, p_ref, acc_ref):
    nblk = kmean_ref.shape[1]
    n_tiles = k_ref.shape[1] // ATT_TILE
    q_refs = (qa_ref, qb_ref)
    acc_ref[...] = jnp.zeros(acc_ref.shape, F32)

    km = kmean_ref[0]
    km_hi = km.astype(BF16)
    km_lo = (km - km_hi.astype(F32)).astype(BF16)
    km_parts = jnp.concatenate([km_hi, km_lo], axis=0)
    blk = lax.broadcasted_iota(jnp.int32, (nblk, ATT_TILE), 0)
    for i in range(n_tiles):
        valid = blk < i
        for hh in range(2):
            q = q_refs[hh][0, i * ATT_TILE:(i + 1) * ATT_TILE, :]
            g2 = _dot_nt(km_parts, q)
            g = g2[0:nblk] + g2[nblk:2 * nblk]
            g = jnp.where(valid, g, -jnp.inf)
            sel = blk == i
            for _ in range(min(MOBA_TOPK, nblk - 1)):
                mx = jnp.max(g, axis=0, keepdims=True)
                first = jnp.min(jnp.where(g == mx, blk, nblk), axis=0, keepdims=True)
                pick = blk == first
                sel = jnp.logical_or(sel, jnp.logical_and(pick, valid))
                g = jnp.where(pick, -jnp.inf, g)
            sel_ref[i, hh] = jnp.where(sel, 0.0, NEG_INF).astype(F32)

    def score_map(i, p, slot, hh):
        return _score_map(i, p, slot, hh, q_refs[hh], k_ref, lambda d: bias_ref[hh, d], s_ref)

    def soft(i, p, slot, colmax, m):
        offs = []
        for u in range(2):
            j, _, is_pad = _item(i, 2 * p + u)
            pad = _pad_off(is_pad)
            offs.append([offs_ref[hh, pl.ds(i - j, 1), :] + sel_ref[i, hh, pl.ds(j, 1), :] + pad
                         for hh in range(2)])
        return _soft(p, slot, colmax, offs, m, s_ref, p_ref)

    def value_map(i, p, slot, hh, alpha_c):
        vts = []
        for u in range(2):
            j, _, _ = _item(i, 2 * p + u)
            vts.append(vt_ref[0, j, hh * HEAD_DIM:(hh + 1) * HEAD_DIM, :])
        _value_map(i, slot, hh, alpha_c, vts, p_ref, acc_ref)

    def finish(i):
        par = i % acc_ref.shape[0]
        o_t = jnp.concatenate([acc_ref[par, hh, 0:HEAD_DIM] * (1.0 / acc_ref[par, hh, HEAD_DIM:HEAD_DIM + 1])
                               for hh in range(2)], axis=0)
        o_ref[0, i] = (o_t * gt_ref[0, i].astype(F32)).astype(BF16)

    _sweep(tab_ref, 2, acc_ref.shape[0], score_map, soft, value_map, finish)


def _moba(q_split, km, vt, gt, kmean, bias, offs, tab):
    B, S, width = km.shape
    pairs = width // LANES
    nt = S // ATT_TILE
    nblk = kmean.shape[1]
    unroll = _sweep_unroll(nt)
    ring = _ring_slots(unroll)
    seq_spec = lambda lane_block0: pl.BlockSpec((1, S, LANES), lambda b, p: (b, 0, lane_block0 + p))
    t_spec = pl.BlockSpec((1, nblk, LANES, ATT_TILE), lambda b, p: (b, 0, p, 0))
    return pl.pallas_call(
        _moba_kernel,
        grid=(B, pairs),
        in_specs=[pl.BlockSpec(memory_space=pltpu.SMEM),
                  seq_spec(0), seq_spec(pairs), seq_spec(0), t_spec, t_spec,
                  pl.BlockSpec((1, nblk, LANES), lambda b, p: (b, 0, p)),
                  pl.BlockSpec((2, 2, ATT_TILE, ATT_TILE), lambda b, p: (p, 0, 0, 0)),
                  pl.BlockSpec((2, nt, ATT_TILE), lambda b, p: (p, 0, 0))],
        out_specs=t_spec,
        out_shape=jax.ShapeDtypeStruct((B, nt, width, ATT_TILE), BF16),
        scratch_shapes=[pltpu.VMEM((nt, 2, nblk, ATT_TILE), F32),
                        pltpu.VMEM((ring, 2, 2, ATT_TILE, ATT_TILE), BF16),
                        pltpu.VMEM((ring, 2, 2 * ATT_TILE, ATT_TILE), BF16),
                        pltpu.VMEM((unroll, 2, HEAD_DIM + BF16_ROWS, ATT_TILE), F32)],
        compiler_params=pltpu.CompilerParams(dimension_semantics=("arbitrary", "arbitrary")),
        name="moba_attn",
    )(tab, q_split, q_split, km, vt, gt, kmean, bias, offs)


def _diff_kernel(tab_ref, qa_ref, qb_ref, k_ref, vt_ref, gt_ref, bias_ref, offs_ref, lamv_ref, subw_ref,
                 o_ref, s_ref, p_ref, acc_ref, *, lambda_init):
    q_refs = (qa_ref, qb_ref)
    acc_ref[...] = jnp.zeros(acc_ref.shape, F32)
    lv = lamv_ref[...]
    lam = (jnp.exp(jnp.sum(lv[0:1] * lv[1:2], axis=-1, keepdims=True))
           - jnp.exp(jnp.sum(lv[2:3] * lv[3:4], axis=-1, keepdims=True)) + lambda_init)

    def score_map(i, p, slot, c):
        return _score_map(i, p, slot, c, q_refs[c], k_ref, lambda d: bias_ref[0, d], s_ref)

    def soft(i, p, slot, colmax, m):
        offs = []
        for u in range(2):
            j, _, is_pad = _item(i, 2 * p + u)
            off = offs_ref[0, pl.ds(i - j, 1), :] + _pad_off(is_pad)
            offs.append([off, off])
        return _soft(p, slot, colmax, offs, m, s_ref, p_ref)

    def value_map(i, p, slot, c, alpha_c):
        vts = [vt_ref[0, _item(i, 2 * p + u)[0]] for u in range(2)]
        _value_map(i, slot, c, alpha_c, vts, p_ref, acc_ref)

    def finish(i):
        par = i % acc_ref.shape[0]
        o = [acc_ref[par, c, 0:LANES] * (1.0 / acc_ref[par, c, LANES:LANES + 1]) for c in range(2)]
        o_t = o[0] - lam * o[1]
        ms = jnp.mean(o_t * o_t, axis=0, keepdims=True)
        o_n = o_t * lax.rsqrt(ms + NORM_EPS) * subw_ref[...]
        o_ref[0, i] = (o_n * gt_ref[0, i].astype(F32)).astype(BF16)

    _sweep(tab_ref, 2, acc_ref.shape[0], score_map, soft, value_map, finish)


def _diff(q_split, dk, vt, gt, bias, offs, lamv, subw, lambda_init, v_row_block0, tab):
    B, S, width = dk.shape
    heads = width // LANES
    nt = S // ATT_TILE
    nblk = vt.shape[1]
    unroll = _sweep_unroll(nt)
    ring = _ring_slots(unroll)
    seq_spec = lambda lane_block0: pl.BlockSpec((1, S, LANES), lambda b, h: (b, 0, lane_block0 + h))
    t_spec = lambda row_block0: pl.BlockSpec((1, nblk, LANES, ATT_TILE), lambda b, h: (b, 0, row_block0 + h, 0))
    return pl.pallas_call(
        functools.partial(_diff_kernel, lambda_init=lambda_init),
        grid=(B, heads),
        in_specs=[pl.BlockSpec(memory_space=pltpu.SMEM),
                  seq_spec(0), seq_spec(heads), seq_spec(0), t_spec(v_row_block0), t_spec(v_row_block0),
                  pl.BlockSpec((1, 2, ATT_TILE, ATT_TILE), lambda b, h: (h, 0, 0, 0)),
                  pl.BlockSpec((1, nt, ATT_TILE), lambda b, h: (h, 0, 0)),
                  pl.BlockSpec(lamv.shape, lambda b, h: (0, 0)),
                  pl.BlockSpec((LANES, ATT_TILE), lambda b, h: (0, 0))],
        out_specs=t_spec(0),
        out_shape=jax.ShapeDtypeStruct((B, nt, width, ATT_TILE), BF16),
        scratch_shapes=[pltpu.VMEM((ring, 2, 2, ATT_TILE, ATT_TILE), BF16),
                        pltpu.VMEM((ring, 2, 2 * ATT_TILE, ATT_TILE), BF16),
                        pltpu.VMEM((unroll, 2, LANES + BF16_ROWS, ATT_TILE), F32)],
        compiler_params=pltpu.CompilerParams(dimension_semantics=("arbitrary", "arbitrary")),
        name="diff_attn",
    )(tab, q_split, q_split, dk, vt, gt, bias, offs, lamv, subw)


def _outproj_kernel(x_ref, m_ref, d_ref, w_ref, o_ref):
    o_ref[0] = _merge_rows(x_ref, m_ref, d_ref, w_ref)


def _outproj(x, m_att, d_att, w_out):
    B, S, D = x.shape
    rows = _proj_rows(S)
    tiles = rows // ATT_TILE
    row_spec = pl.BlockSpec((1, rows, D), lambda b, t: (b, t, 0))
    return pl.pallas_call(
        _outproj_kernel,
        grid=(B, S // rows),
        in_specs=[row_spec, t_spec_of(m_att, tiles), t_spec_of(d_att, tiles),
                  pl.BlockSpec(w_out.shape, lambda b, t: (0, 0))],
        out_specs=row_spec,
        out_shape=jax.ShapeDtypeStruct((B, S, D), F32),
        compiler_params=pltpu.CompilerParams(dimension_semantics=("arbitrary", "arbitrary")),
        name="outproj",
    )(x, m_att, d_att, w_out)


def kernel(x, norm_g, w_in, moba_q_norm, moba_k_norm, diff_q_norm, diff_k_norm,
           lambda_q1, lambda_k1, lambda_q2, lambda_k2, diff_subln, w_out):
    B, S, D = x.shape
    depth = w_in.shape[0]
    width = D // 2
    moba_heads = width // HEAD_DIM
    diff_heads = width // (2 * HEAD_DIM)
    assert S % MOBA_BLOCK == 0 and width % MXU_TILE == 0 and S // MOBA_BLOCK > 1
    nt = S // ATT_TILE
    scale = HEAD_DIM ** -0.5 * LOG2E

    m_bias, m_offs = _bias_tables(moba_heads, nt)
    d_bias, d_offs = _bias_tables(diff_heads, nt)
    r = lax.broadcasted_iota(jnp.int32, (MXU_TILE, MXU_TILE), 0) // HEAD_DIM
    c = lax.broadcasted_iota(jnp.int32, (MXU_TILE, MXU_TILE), 1) // HEAD_DIM
    gsum = (r == c).astype(BF16)
    reps = width // HEAD_DIM
    tab = jnp.asarray(_pair_table(nt))

    def layer_weights(layer):
        w = w_in[layer]
        cols = [w[:, k * width:(k + 1) * width] for k in range(8)]
        wa = jnp.concatenate([cols[0], cols[1], cols[4], cols[5]], axis=1).astype(BF16)
        wvt = jnp.concatenate([cols[2], cols[6]], axis=1).astype(BF16).T
        wgt = jnp.concatenate([cols[3], cols[7]], axis=1).astype(BF16).T
        tile_w = lambda v, s: (jnp.tile(v, reps) * s)[None, :].astype(F32)
        return (norm_g[layer][None, :], wa, wvt, wgt, gsum,
                tile_w(moba_q_norm[layer], scale), tile_w(moba_k_norm[layer], 1.0),
                tile_w(diff_q_norm[layer], scale), tile_w(diff_k_norm[layer], 1.0))

    proj = _inproj(x, *layer_weights(0))
    for layer in range(depth):
        qm, km, dq, dk, gt, vt, kmean = proj
        m_att = _moba(qm, km, vt, gt, kmean, m_bias, m_offs, tab)
        lambda_init = 0.8 - 0.6 * math.exp(-0.3 * layer)
        lamv = jnp.stack([lambda_q1[layer], lambda_k1[layer], lambda_q2[layer], lambda_k2[layer]])
        subw = jnp.broadcast_to((diff_subln[layer] * (1.0 - lambda_init))[:, None],
                                (2 * HEAD_DIM, ATT_TILE)).astype(F32)
        d_att = _diff(dq, dk, vt, gt, d_bias, d_offs, lamv, subw, lambda_init, width // LANES, tab)
        wo = w_out[layer].astype(BF16)
        if layer + 1 < depth:
            x, proj = _inproj(x, *layer_weights(layer + 1), merge=(m_att, d_att, wo))
        else:
            x = _outproj(x, m_att, d_att, wo)
    return x
```

```python
import functools
import math

import numpy as np
import jax
import jax.numpy as jnp
from jax import lax
from jax.experimental import pallas as pl
from jax.experimental.pallas import tpu as pltpu

HEAD_DIM = 64
MOBA_BLOCK = 256
MOBA_TOPK = 3
NORM_EPS = 1e-6
NEG_INF = -1e30

LANES = 128
MXU_TILE = 256
ATT_TILE = MOBA_BLOCK
PROJ_ROWS = 1024
LOG2E = math.log2(math.e)
BF16_ROWS = 16
SWEEP_UNROLLS = (24, 12, 8, 4, 2)
MAX_RING_SLOTS = 24

F32 = jnp.float32
BF16 = jnp.bfloat16


def _dot(a, b):
    return jnp.dot(a, b, preferred_element_type=F32)


def _dot_nt(a, b):
    return lax.dot_general(a, b, (((1,), (1,)), ((), ())), preferred_element_type=F32)


def _proj_rows(seq):
    return max(r for r in range(MOBA_BLOCK, PROJ_ROWS + 1, MOBA_BLOCK) if seq % r == 0)


def _alibi_slopes(n_heads):
    return np.asarray(2.0 ** (-8.0 * np.arange(1, n_heads + 1) / n_heads), dtype=np.float32)


def _bias_tables(n_heads, n_tiles):
    slopes = _alibi_slopes(n_heads) * np.float32(LOG2E)
    k_loc = np.arange(ATT_TILE, dtype=np.int32)[:, None]
    q_loc = np.arange(ATT_TILE, dtype=np.int32)[None, :]
    rel = (k_loc - q_loc).astype(np.float32)
    off = slopes[:, None, None] * rel[None]
    diag = np.where((k_loc <= q_loc)[None], off, np.float32(NEG_INF))
    tiles = np.stack([off, diag], axis=1).astype(np.float32)
    d = np.arange(n_tiles, dtype=np.float32)
    offs = -slopes[:, None] * (ATT_TILE * d)[None, :]
    offs = np.broadcast_to(offs[:, :, None], (n_heads, n_tiles, ATT_TILE)).astype(np.float32)
    return jnp.asarray(tiles), jnp.asarray(offs)


def _project_rows(x, t, g_ref, wa_ref, wvt_ref, wgt_ref, gsum_ref, wqm_ref, wkm_ref, wqd_ref, wkd_ref,
                  qm_ref, km_ref, dq_ref, dk_ref, gt_ref, vt_ref, kmean_ref, width):
    rows = x.shape[0]
    blocks = rows // MOBA_BLOCK
    ms = jnp.mean(x * x, axis=-1, keepdims=True)
    h = (x * lax.rsqrt(ms + NORM_EPS) * g_ref[...]).astype(BF16)

    gsum = gsum_ref[...]
    low_half = lax.broadcasted_iota(jnp.int32, (rows, width), 1) % LANES < HEAD_DIM

    def head_norm(y, w):
        sq = (y * y).astype(BF16)
        parts = [_dot(sq[:, c * MXU_TILE:(c + 1) * MXU_TILE], gsum) for c in range(width // MXU_TILE)]
        ssq = jnp.concatenate(parts, axis=1)
        return y * lax.rsqrt(ssq * (1.0 / HEAD_DIM) + NORM_EPS) * w

    def store_split(ref, q):
        ref[0, :, 0:width] = jnp.where(low_half, q, 0.0).astype(BF16)
        ref[0, :, width:2 * width] = jnp.where(low_half, 0.0, q).astype(BF16)

    def silu(y):
        return y * (1.0 / (1.0 + jnp.exp(-y)))

    y = _dot(h, wa_ref[:, 0 * width:1 * width])
    store_split(qm_ref, head_norm(y, wqm_ref[...]))

    y = _dot(h, wa_ref[:, 1 * width:2 * width])
    kn = head_norm(y, wkm_ref[...])
    km_ref[0] = kn.astype(BF16)
    for s in range(blocks):
        kmean_ref[0, pl.ds(t * blocks + s, 1), :] = jnp.mean(
            kn[s * MOBA_BLOCK:(s + 1) * MOBA_BLOCK], axis=0, keepdims=True)

    y = _dot(h, wa_ref[:, 2 * width:3 * width])
    store_split(dq_ref, head_norm(y, wqd_ref[...]))

    y = _dot(h, wa_ref[:, 3 * width:4 * width])
    dk_ref[0] = head_norm(y, wkd_ref[...]).astype(BF16)

    vt = _dot_nt(wvt_ref[...], h)
    gt = silu(_dot_nt(wgt_ref[...], h))
    for s in range(blocks):
        vt_ref[0, s] = vt[:, s * MOBA_BLOCK:(s + 1) * MOBA_BLOCK].astype(BF16)
        gt_ref[0, s] = gt[:, s * MOBA_BLOCK:(s + 1) * MOBA_BLOCK].astype(BF16)


def _merge_rows(x_ref, m_ref, d_ref, w_ref):
    parts = []
    for t in range(m_ref.shape[1]):
        mixed_t = jnp.concatenate([m_ref[0, t], d_ref[0, t]], axis=0)
        y = lax.dot_general(mixed_t, w_ref[...], (((0,), (0,)), ((), ())), preferred_element_type=F32)
        parts.append(x_ref[0, t * ATT_TILE:(t + 1) * ATT_TILE, :] + y)
    return jnp.concatenate(parts, axis=0)


def _inproj_kernel(x_ref, *refs, width):
    _project_rows(x_ref[0], pl.program_id(1), *refs, width)


def _mid_kernel(x_ref, m_ref, d_ref, wo_ref, *refs, width):
    xo_ref, out_refs = refs[9], refs[10:]
    x_new = _merge_rows(x_ref, m_ref, d_ref, wo_ref)
    xo_ref[0] = x_new
    _project_rows(x_new, pl.program_id(1), *refs[:9], *out_refs, width)


def _inproj(x, g, wa, wvt, wgt, gsum, wqm, wkm, wqd, wkd, merge=None):
    B, S, D = x.shape
    width = wa.shape[1] // 4
    rows = _proj_rows(S)
    nblk = S // MOBA_BLOCK
    blocks = rows // MOBA_BLOCK
    full = lambda shape: pl.BlockSpec(shape, lambda b, t: (0,) * len(shape))
    row_spec = lambda w: pl.BlockSpec((1, rows, w), lambda b, t: (b, t, 0))
    act = lambda w: jax.ShapeDtypeStruct((B, S, w), BF16)
    t_spec = pl.BlockSpec((1, blocks, 2 * width, MOBA_BLOCK), lambda b, t: (b, t, 0, 0))
    t_shape = jax.ShapeDtypeStruct((B, nblk, 2 * width, MOBA_BLOCK), BF16)
    weights = (g, wa, wvt, wgt, gsum, wqm, wkm, wqd, wkd)
    in_specs = [row_spec(D)] + [full(w.shape) for w in weights]
    out_specs = [row_spec(2 * width), row_spec(width), row_spec(2 * width), row_spec(width),
                 t_spec, t_spec, pl.BlockSpec((1, nblk, width), lambda b, t: (b, 0, 0))]
    out_shape = [act(2 * width), act(width), act(2 * width), act(width), t_shape, t_shape,
                 jax.ShapeDtypeStruct((B, nblk, width), F32)]
    params = pltpu.CompilerParams(dimension_semantics=("arbitrary", "arbitrary"))
    if merge is None:
        return pl.pallas_call(
            functools.partial(_inproj_kernel, width=width), grid=(B, S // rows),
            in_specs=in_specs, out_specs=out_specs, out_shape=out_shape,
            compiler_params=params, name="inproj")(x, *weights)
    m_att, d_att, w_out = merge
    in_specs = [row_spec(D), t_spec_of(m_att, blocks), t_spec_of(d_att, blocks), full(w_out.shape)] + in_specs[1:]
    outs = pl.pallas_call(
        functools.partial(_mid_kernel, width=width), grid=(B, S // rows),
        in_specs=in_specs, out_specs=[row_spec(D)] + out_specs,
        out_shape=[jax.ShapeDtypeStruct((B, S, D), F32)] + out_shape,
        compiler_params=params, name="outproj_inproj")(x, m_att, d_att, w_out, *weights)
    return outs[0], outs[1:]


def t_spec_of(a, tiles):
    return pl.BlockSpec((1, tiles, a.shape[2], ATT_TILE), lambda b, t: (b, t, 0, 0))


def _pair_table(n_tiles):
    rows = [(i, p, int(p == i // 2)) for i in range(n_tiles) for p in range(i // 2 + 1)]
    rows += [(rows[-1][0], rows[-1][1], 0)] * 2
    return np.ascontiguousarray(np.asarray(rows, dtype=np.int32).T)


def _item(i, t):
    t = jnp.asarray(t, jnp.int32)
    is_diag = t == 0
    j = jnp.where(is_diag, i, t - 1)
    is_pad = jnp.logical_and(t > 0, t - 1 >= i)
    return j, is_diag, is_pad


def _rows(i):
    return pl.ds(pl.multiple_of(i * ATT_TILE, ATT_TILE), ATT_TILE)


def _pad_off(is_pad):
    return jnp.where(is_pad, jnp.float32(NEG_INF), jnp.float32(0.0))


def _sweep_unroll(n_tiles):
    n_pairs = _pair_table(n_tiles).shape[1] - 2
    return next(u for u in SWEEP_UNROLLS if n_pairs % u == 0)


def _ring_slots(unroll):
    return min(unroll, MAX_RING_SLOTS)


def _sweep(tab_ref, n_maps, unroll, score_map, soft, value_map, finish):
    n_pairs = tab_ref.shape[1] - 2
    ring = _ring_slots(unroll)
    assert n_pairs % unroll == 0 and unroll % ring == 0
    pair = lambda g: (tab_ref[0, g], tab_ref[1, g])

    def score_all(g, slot):
        return tuple(x for c in range(n_maps) for x in score_map(*pair(g), slot, c))

    def half_step(g, k, carry):
        colmax, m, alpha = carry
        colmax1 = []
        for c in range(n_maps):
            colmax1 += score_map(*pair(g), k, c)
            value_map(*pair(g - 2), (k - 2) % ring, c, alpha[c])
        m1, alpha1 = soft(*pair(g - 1), (k - 1) % ring, colmax, m)
        return tuple(colmax1), m1, alpha1

    colmax = score_all(0, 0)
    m_init = tuple(jnp.full((1, ATT_TILE), NEG_INF, F32) for _ in range(n_maps))
    m, alpha = soft(*pair(0), 0, colmax, m_init)
    carry = (score_all(1, 1), m, alpha)

    def step(h, carry):
        g = 2 + unroll * h
        for k in range(unroll):
            carry = half_step(g + k, (2 + k) % ring, carry)
        for k in range(unroll):
            @pl.when(tab_ref[2, g + k - 2] == 1)
            def _(k=k):
                finish(tab_ref[0, g + k - 2])
        return carry

    lax.fori_loop(0, n_pairs // unroll, step, carry)


def _score_map(i, p, slot, c, q_ref, k_ref, bias_of, s_ref):
    q = q_ref[0, _rows(i), :]
    items = [_item(i, 2 * p + u) for u in range(2)]
    kt = jnp.concatenate([k_ref[0, _rows(j), :] for j, _, _ in items], axis=0)
    s2 = _dot_nt(kt, q)
    colmax = []
    for u, (_, is_diag, _) in enumerate(items):
        s = s2[u * ATT_TILE:(u + 1) * ATT_TILE] + bias_of(is_diag.astype(jnp.int32))
        s_ref[slot, u, c] = s.astype(BF16)
        colmax.append(jnp.max(s, axis=0, keepdims=True))
    return colmax


def _soft(p, slot, colmax, offs, m, s_ref, p_ref):
    m_out, alpha_out = [], []
    for c in range(len(m)):
        m_old = jnp.where(p == 0, jnp.float32(NEG_INF), m[c])
        m_new = jnp.maximum(m_old, jnp.maximum(colmax[2 * c] + offs[0][c], colmax[2 * c + 1] + offs[1][c]))
        for u in range(2):
            shift = (m_new - offs[u][c]).astype(BF16)
            p_ref[slot, c, u * ATT_TILE:(u + 1) * ATT_TILE, :] = jnp.exp2(s_ref[slot, u, c] - shift)
        m_out.append(m_new)
        alpha_out.append(jnp.exp2(m_old - m_new))
    return tuple(m_out), tuple(alpha_out)


def _value_map(i, slot, c, alpha_c, vts, p_ref, acc_ref):
    vt = jnp.concatenate(vts, axis=1)
    vt1 = jnp.concatenate([vt, jnp.ones((BF16_ROWS, vt.shape[1]), BF16)], axis=0)
    par = i % acc_ref.shape[0]
    acc_ref[par, c] = alpha_c * acc_ref[par, c] + _dot(vt1, p_ref[slot, c])


def _moba_kernel(tab_ref, qa_ref, qb_ref, k_ref, vt_ref, gt_ref, kmean_ref, bias_ref, offs_ref, o_ref,
                 sel_ref, s_ref, p_ref, acc_ref):
    nblk = kmean_ref.shape[1]
    n_tiles = k_ref.shape[1] // ATT_TILE
    q_refs = (qa_ref, qb_ref)
    acc_ref[...] = jnp.zeros(acc_ref.shape, F32)

    km = kmean_ref[0]
    km_hi = km.astype(BF16)
    km_lo = (km - km_hi.astype(F32)).astype(BF16)
    km_parts = jnp.concatenate([km_hi, km_lo], axis=0)
    blk = lax.broadcasted_iota(jnp.int32, (nblk, ATT_TILE), 0)
    for i in range(n_tiles):
        valid = blk < i
        for hh in range(2):
            q = q_refs[hh][0, i * ATT_TILE:(i + 1) * ATT_TILE, :]
            g2 = _dot_nt(km_parts, q)
            g = g2[0:nblk] + g2[nblk:2 * nblk]
            g = jnp.where(valid, g, -jnp.inf)
            sel = blk == i
            for _ in range(min(MOBA_TOPK, nblk - 1)):
                mx = jnp.max(g, axis=0, keepdims=True)
                first = jnp.min(jnp.where(g == mx, blk, nblk), axis=0, keepdims=True)
                pick = blk == first
                sel = jnp.logical_or(sel, jnp.logical_and(pick, valid))
                g = jnp.where(pick, -jnp.inf, g)
            sel_ref[i, hh] = jnp.where(sel, 0.0, NEG_INF).astype(F32)

    def score_map(i, p, slot, hh):
        return _score_map(i, p, slot, hh, q_refs[hh], k_ref, lambda d: bias_ref[hh, d], s_ref)

    def soft(i, p, slot, colmax, m):
        offs = []
        for u in range(2):
            j, _, is_pad = _item(i, 2 * p + u)
            pad = _pad_off(is_pad)
            offs.append([offs_ref[hh, pl.ds(i - j, 1), :] + sel_ref[i, hh, pl.ds(j, 1), :] + pad
                         for hh in range(2)])
        return _soft(p, slot, colmax, offs, m, s_ref, p_ref)

    def value_map(i, p, slot, hh, alpha_c):
        vts = []
        for u in range(2):
            j, _, _ = _item(i, 2 * p + u)
            vts.append(vt_ref[0, j, hh * HEAD_DIM:(hh + 1) * HEAD_DIM, :])
        _value_map(i, slot, hh, alpha_c, vts, p_ref, acc_ref)

    def finish(i):
        par = i % acc_ref.shape[0]
        o_t = jnp.concatenate([acc_ref[par, hh, 0:HEAD_DIM] * (1.0 / acc_ref[par, hh, HEAD_DIM:HEAD_DIM + 1])
                               for hh in range(2)], axis=0)
        o_ref[0, i] = (o_t * gt_ref[0, i].astype(F32)).astype(BF16)

    _sweep(tab_ref, 2, acc_ref.shape[0], score_map, soft, value_map, finish)


def _moba(q_split, km, vt, gt, kmean, bias, offs, tab):
    B, S, width = km.shape
    pairs = width // LANES
    nt = S // ATT_TILE
    nblk = kmean.shape[1]
    unroll = _sweep_unroll(nt)
    ring = _ring_slots(unroll)
    seq_spec = lambda lane_block0: pl.BlockSpec((1, S, LANES), lambda b, p: (b, 0, lane_block0 + p))
    t_spec = pl.BlockSpec((1, nblk, LANES, ATT_TILE), lambda b, p: (b, 0, p, 0))
    return pl.pallas_call(
        _moba_kernel,
        grid=(B, pairs),
        in_specs=[pl.BlockSpec(memory_space=pltpu.SMEM),
                  seq_spec(0), seq_spec(pairs), seq_spec(0), t_spec, t_spec,
                  pl.BlockSpec((1, nblk, LANES), lambda b, p: (b, 0, p)),
                  pl.BlockSpec((2, 2, ATT_TILE, ATT_TILE), lambda b, p: (p, 0, 0, 0)),
                  pl.BlockSpec((2, nt, ATT_TILE), lambda b, p: (p, 0, 0))],
        out_specs=t_spec,
        out_shape=jax.ShapeDtypeStruct((B, nt, width, ATT_TILE), BF16),
        scratch_shapes=[pltpu.VMEM((nt, 2, nblk, ATT_TILE), F32),
                        pltpu.VMEM((ring, 2, 2, ATT_TILE, ATT_TILE), BF16),
                        pltpu.VMEM((ring, 2, 2 * ATT_TILE, ATT_TILE), BF16),
                        pltpu.VMEM((unroll, 2, HEAD_DIM + BF16_ROWS, ATT_TILE), F32)],
        compiler_params=pltpu.CompilerParams(dimension_semantics=("arbitrary", "arbitrary")),
        name="moba_attn",
    )(tab, q_split, q_split, km, vt, gt, kmean, bias, offs)


def _diff_kernel(tab_ref, qa_ref, qb_ref, k_ref, vt_ref, gt_ref, bias_ref, offs_ref, lamv_ref, subw_ref,
                 o_ref, s_ref, p_ref, acc_ref, *, lambda_init):
    q_refs = (qa_ref, qb_ref)
    acc_ref[...] = jnp.zeros(acc_ref.shape, F32)
    lv = lamv_ref[...]
    lam = (jnp.exp(jnp.sum(lv[0:1] * lv[1:2], axis=-1, keepdims=True))
           - jnp.exp(jnp.sum(lv[2:3] * lv[3:4], axis=-1, keepdims=True)) + lambda_init)

    def score_map(i, p, slot, c):
        return _score_map(i, p, slot, c, q_refs[c], k_ref, lambda d: bias_ref[0, d], s_ref)

    def soft(i, p, slot, colmax, m):
        offs = []
        for u in range(2):
            j, _, is_pad = _item(i, 2 * p + u)
            off = offs_ref[0, pl.ds(i - j, 1), :] + _pad_off(is_pad)
            offs.append([off, off])
        return _soft(p, slot, colmax, offs, m, s_ref, p_ref)

    def value_map(i, p, slot, c, alpha_c):
        vts = [vt_ref[0, _item(i, 2 * p + u)[0]] for u in range(2)]
        _value_map(i, slot, c, alpha_c, vts, p_ref, acc_ref)

    def finish(i):
        par = i % acc_ref.shape[0]
        o = [acc_ref[par, c, 0:LANES] * (1.0 / acc_ref[par, c, LANES:LANES + 1]) for c in range(2)]
        o_t = o[0] - lam * o[1]
        ms = jnp.mean(o_t * o_t, axis=0, keepdims=True)
        o_n = o_t * lax.rsqrt(ms + NORM_EPS) * subw_ref[...]
        o_ref[0, i] = (o_n * gt_ref[0, i].astype(F32)).astype(BF16)

    _sweep(tab_ref, 2, acc_ref.shape[0], score_map, soft, value_map, finish)


def _diff(q_split, dk, vt, gt, bias, offs, lamv, subw, lambda_init, v_row_block0, tab):
    B, S, width = dk.shape
    heads = width // LANES
    nt = S // ATT_TILE
    nblk = vt.shape[1]
    unroll = _sweep_unroll(nt)
    ring = _ring_slots(unroll)
    seq_spec = lambda lane_block0: pl.BlockSpec((1, S, LANES), lambda b, h: (b, 0, lane_block0 + h))
    t_spec = lambda row_block0: pl.BlockSpec((1, nblk, LANES, ATT_TILE), lambda b, h: (b, 0, row_block0 + h, 0))
    return pl.pallas_call(
        functools.partial(_diff_kernel, lambda_init=lambda_init),
        grid=(B, heads),
        in_specs=[pl.BlockSpec(memory_space=pltpu.SMEM),
                  seq_spec(0), seq_spec(heads), seq_spec(0), t_spec(v_row_block0), t_spec(v_row_block0),
                  pl.BlockSpec((1, 2, ATT_TILE, ATT_TILE), lambda b, h: (h, 0, 0, 0)),
                  pl.BlockSpec((1, nt, ATT_TILE), lambda b, h: (h, 0, 0)),
                  pl.BlockSpec(lamv.shape, lambda b, h: (0, 0)),
                  pl.BlockSpec((LANES, ATT_TILE), lambda b, h: (0, 0))],
        out_specs=t_spec(0),
        out_shape=jax.ShapeDtypeStruct((B, nt, width, ATT_TILE), BF16),
        scratch_shapes=[pltpu.VMEM((ring, 2, 2, ATT_TILE, ATT_TILE), BF16),
                        pltpu.VMEM((ring, 2, 2 * ATT_TILE, ATT_TILE), BF16),
                        pltpu.VMEM((unroll, 2, LANES + BF16_ROWS, ATT_TILE), F32)],
        compiler_params=pltpu.CompilerParams(dimension_semantics=("arbitrary", "arbitrary")),
        name="diff_attn",
    )(tab, q_split, q_split, dk, vt, gt, bias, offs, lamv, subw)


def _outproj_kernel(x_ref, m_ref, d_ref, w_ref, o_ref):
    o_ref[0] = _merge_rows(x_ref, m_ref, d_ref, w_ref)


def _outproj(x, m_att, d_att, w_out):
    B, S, D = x.shape
    rows = _proj_rows(S)
    tiles = rows // ATT_TILE
    row_spec = pl.BlockSpec((1, rows, D), lambda b, t: (b, t, 0))
    return pl.pallas_call(
        _outproj_kernel,
        grid=(B, S // rows),
        in_specs=[row_spec, t_spec_of(m_att, tiles), t_spec_of(d_att, tiles),
                  pl.BlockSpec(w_out.shape, lambda b, t: (0, 0))],
        out_specs=row_spec,
        out_shape=jax.ShapeDtypeStruct((B, S, D), F32),
        compiler_params=pltpu.CompilerParams(dimension_semantics=("arbitrary", "arbitrary")),
        name="outproj",
    )(x, m_att, d_att, w_out)


def kernel(x, norm_g, w_in, moba_q_norm, moba_k_norm, diff_q_norm, diff_k_norm,
           lambda_q1, lambda_k1, lambda_q2, lambda_k2, diff_subln, w_out):
    B, S, D = x.shape
    depth = w_in.shape[0]
    width = D // 2
    moba_heads = width // HEAD_DIM
    diff_heads = width // (2 * HEAD_DIM)
    assert S % MOBA_BLOCK == 0 and width % MXU_TILE == 0 and S // MOBA_BLOCK > 1
    nt = S // ATT_TILE
    scale = HEAD_DIM ** -0.5 * LOG2E

    m_bias, m_offs = _bias_tables(moba_heads, nt)
    d_bias, d_offs = _bias_tables(diff_heads, nt)
    r = lax.broadcasted_iota(jnp.int32, (MXU_TILE, MXU_TILE), 0) // HEAD_DIM
    c = lax.broadcasted_iota(jnp.int32, (MXU_TILE, MXU_TILE), 1) // HEAD_DIM
    gsum = (r == c).astype(BF16)
    reps = width // HEAD_DIM
    tab = jnp.asarray(_pair_table(nt))

    def layer_weights(layer):
        w = w_in[layer]
        cols = [w[:, k * width:(k + 1) * width] for k in range(8)]
        wa = jnp.concatenate([cols[0], cols[1], cols[4], cols[5]], axis=1).astype(BF16)
        wvt = jnp.concatenate([cols[2], cols[6]], axis=1).astype(BF16).T
        wgt = jnp.concatenate([cols[3], cols[7]], axis=1).astype(BF16).T
        tile_w = lambda v, s: (jnp.tile(v, reps) * s)[None, :].astype(F32)
        return (norm_g[layer][None, :], wa, wvt, wgt, gsum,
                tile_w(moba_q_norm[layer], scale), tile_w(moba_k_norm[layer], 1.0),
                tile_w(diff_q_norm[layer], scale), tile_w(diff_k_norm[layer], 1.0))

    proj = _inproj(x, *layer_weights(0))
    for layer in range(depth):
        qm, km, dq, dk, gt, vt, kmean = proj
        m_att = _moba(qm, km, vt, gt, kmean, m_bias, m_offs, tab)
        lambda_init = 0.8 - 0.6 * math.exp(-0.3 * layer)
        lamv = jnp.stack([lambda_q1[layer], lambda_k1[layer], lambda_q2[layer], lambda_k2[layer]])
        subw = jnp.broadcast_to((diff_subln[layer] * (1.0 - lambda_init))[:, None],
                                (2 * HEAD_DIM, ATT_TILE)).astype(F32)
        d_att = _diff(dq, dk, vt, gt, d_bias, d_offs, lamv, subw, lambda_init, width // LANES, tab)
        wo = w_out[layer].astype(BF16)
        if layer + 1 < depth:
            x, proj = _inproj(x, *layer_weights(layer + 1), merge=(m_att, d_att, wo))
        else:
            x = _outproj(x, m_att, d_att, wo)
    return x
```

```python
import functools
import math

import numpy as np
import jax
import jax.numpy as jnp
from jax import lax
from jax.experimental import pallas as pl
from jax.experimental.pallas import tpu as pltpu

HEAD_DIM = 64
MOBA_BLOCK = 256
MOBA_TOPK = 3
NORM_EPS = 1e-6
NEG_INF = -1e30

LANES = 128
MXU_TILE = 256
ATT_TILE = MOBA_BLOCK
PROJ_ROWS = 1024
LOG2E = math.log2(math.e)
BF16_ROWS = 16
SWEEP_UNROLLS = (36, 24, 12, 8, 4, 2)
MAX_RING_SLOTS = 18

F32 = jnp.float32
BF16 = jnp.bfloat16


def _dot(a, b):
    return jnp.dot(a, b, preferred_element_type=F32)


def _dot_nt(a, b):
    return lax.dot_general(a, b, (((1,), (1,)), ((), ())), preferred_element_type=F32)


def _proj_rows(seq):
    return max(r for r in range(MOBA_BLOCK, PROJ_ROWS + 1, MOBA_BLOCK) if seq % r == 0)


def _alibi_slopes(n_heads):
    return np.asarray(2.0 ** (-8.0 * np.arange(1, n_heads + 1) / n_heads), dtype=np.float32)


def _bias_tables(n_heads, n_tiles):
    slopes = _alibi_slopes(n_heads) * np.float32(LOG2E)
    k_loc = np.arange(ATT_TILE, dtype=np.int32)[:, None]
    q_loc = np.arange(ATT_TILE, dtype=np.int32)[None, :]
    rel = (k_loc - q_loc).astype(np.float32)
    off = slopes[:, None, None] * rel[None]
    diag = np.where((k_loc <= q_loc)[None], off, np.float32(NEG_INF))
    tiles = np.stack([off, diag], axis=1).astype(np.float32)
    d = np.arange(n_tiles, dtype=np.float32)
    offs = -slopes[:, None] * (ATT_TILE * d)[None, :]
    offs = np.broadcast_to(offs[:, :, None], (n_heads, n_tiles, ATT_TILE)).astype(np.float32)
    return jnp.asarray(tiles), jnp.asarray(offs)


def _project_rows(x, t, g_ref, wa_ref, wvt_ref, wgt_ref, gsum_ref, wqm_ref, wkm_ref, wqd_ref, wkd_ref,
                  qm_ref, km_ref, dq_ref, dk_ref, gt_ref, vt_ref, kmean_ref, width):
    rows = x.shape[0]
    blocks = rows // MOBA_BLOCK
    ms = jnp.mean(x * x, axis=-1, keepdims=True)
    h = (x * lax.rsqrt(ms + NORM_EPS) * g_ref[...]).astype(BF16)

    gsum = gsum_ref[...]
    low_half = lax.broadcasted_iota(jnp.int32, (rows, width), 1) % LANES < HEAD_DIM

    def head_norm(y, w):
        sq = (y * y).astype(BF16)
        parts = [_dot(sq[:, c * MXU_TILE:(c + 1) * MXU_TILE], gsum) for c in range(width // MXU_TILE)]
        ssq = jnp.concatenate(parts, axis=1)
        return y * lax.rsqrt(ssq * (1.0 / HEAD_DIM) + NORM_EPS) * w

    def store_split(ref, q):
        ref[0, :, 0:width] = jnp.where(low_half, q, 0.0).astype(BF16)
        ref[0, :, width:2 * width] = jnp.where(low_half, 0.0, q).astype(BF16)

    def silu(y):
        return y * (1.0 / (1.0 + jnp.exp(-y)))

    y = _dot(h, wa_ref[:, 0 * width:1 * width])
    store_split(qm_ref, head_norm(y, wqm_ref[...]))

    y = _dot(h, wa_ref[:, 1 * width:2 * width])
    kn = head_norm(y, wkm_ref[...])
    km_ref[0] = kn.astype(BF16)
    for s in range(blocks):
        kmean_ref[0, pl.ds(t * blocks + s, 1), :] = jnp.mean(
            kn[s * MOBA_BLOCK:(s + 1) * MOBA_BLOCK], axis=0, keepdims=True)

    y = _dot(h, wa_ref[:, 2 * width:3 * width])
    store_split(dq_ref, head_norm(y, wqd_ref[...]))

    y = _dot(h, wa_ref[:, 3 * width:4 * width])
    dk_ref[0] = head_norm(y, wkd_ref[...]).astype(BF16)

    vt = _dot_nt(wvt_ref[...], h)
    gt = silu(_dot_nt(wgt_ref[...], h))
    for s in range(blocks):
        vt_ref[0, s] = vt[:, s * MOBA_BLOCK:(s + 1) * MOBA_BLOCK].astype(BF16)
        gt_ref[0, s] = gt[:, s * MOBA_BLOCK:(s + 1) * MOBA_BLOCK].astype(BF16)


def _merge_rows(x_ref, m_ref, d_ref, w_ref):
    parts = []
    for t in range(m_ref.shape[1]):
        mixed_t = jnp.concatenate([m_ref[0, t], d_ref[0, t]], axis=0)
        y = lax.dot_general(mixed_t, w_ref[...], (((0,), (0,)), ((), ())), preferred_element_type=F32)
        parts.append(x_ref[0, t * ATT_TILE:(t + 1) * ATT_TILE, :] + y)
    return jnp.concatenate(parts, axis=0)


def _inproj_kernel(x_ref, *refs, width):
    _project_rows(x_ref[0], pl.program_id(1), *refs, width)


def _mid_kernel(x_ref, m_ref, d_ref, wo_ref, *refs, width):
    xo_ref, out_refs = refs[9], refs[10:]
    x_new = _merge_rows(x_ref, m_ref, d_ref, wo_ref)
    xo_ref[0] = x_new
    _project_rows(x_new, pl.program_id(1), *refs[:9], *out_refs, width)


def _inproj(x, g, wa, wvt, wgt, gsum, wqm, wkm, wqd, wkd, merge=None):
    B, S, D = x.shape
    width = wa.shape[1] // 4
    rows = _proj_rows(S)
    nblk = S // MOBA_BLOCK
    blocks = rows // MOBA_BLOCK
    full = lambda shape: pl.BlockSpec(shape, lambda b, t: (0,) * len(shape))
    row_spec = lambda w: pl.BlockSpec((1, rows, w), lambda b, t: (b, t, 0))
    act = lambda w: jax.ShapeDtypeStruct((B, S, w), BF16)
    t_spec = pl.BlockSpec((1, blocks, 2 * width, MOBA_BLOCK), lambda b, t: (b, t, 0, 0))
    t_shape = jax.ShapeDtypeStruct((B, nblk, 2 * width, MOBA_BLOCK), BF16)
    weights = (g, wa, wvt, wgt, gsum, wqm, wkm, wqd, wkd)
    in_specs = [row_spec(D)] + [full(w.shape) for w in weights]
    out_specs = [row_spec(2 * width), row_spec(width), row_spec(2 * width), row_spec(width),
                 t_spec, t_spec, pl.BlockSpec((1, nblk, width), lambda b, t: (b, 0, 0))]
    out_shape = [act(2 * width), act(width), act(2 * width), act(width), t_shape, t_shape,
                 jax.ShapeDtypeStruct((B, nblk, width), F32)]
    params = pltpu.CompilerParams(dimension_semantics=("arbitrary", "arbitrary"))
    if merge is None:
        return pl.pallas_call(
            functools.partial(_inproj_kernel, width=width), grid=(B, S // rows),
            in_specs=in_specs, out_specs=out_specs, out_shape=out_shape,
            compiler_params=params, name="inproj")(x, *weights)
    m_att, d_att, w_out = merge
    in_specs = [row_spec(D), t_spec_of(m_att, blocks), t_spec_of(d_att, blocks), full(w_out.shape)] + in_specs[1:]
    outs = pl.pallas_call(
        functools.partial(_mid_kernel, width=width), grid=(B, S // rows),
        in_specs=in_specs, out_specs=[row_spec(D)] + out_specs,
        out_shape=[jax.ShapeDtypeStruct((B, S, D), F32)] + out_shape,
        compiler_params=params, name="outproj_inproj")(x, m_att, d_att, w_out, *weights)
    return outs[0], outs[1:]


def t_spec_of(a, tiles):
    return pl.BlockSpec((1, tiles, a.shape[2], ATT_TILE), lambda b, t: (b, t, 0, 0))


def _pair_table(n_tiles):
    rows = [(i, p, int(p == i // 2)) for i in range(n_tiles) for p in range(i // 2 + 1)]
    rows += [(rows[-1][0], rows[-1][1], 0)] * 2
    return np.ascontiguousarray(np.asarray(rows, dtype=np.int32).T)


def _item(i, t):
    t = jnp.asarray(t, jnp.int32)
    is_diag = t == 0
    j = jnp.where(is_diag, i, t - 1)
    is_pad = jnp.logical_and(t > 0, t - 1 >= i)
    return j, is_diag, is_pad


def _rows(i):
    return pl.ds(pl.multiple_of(i * ATT_TILE, ATT_TILE), ATT_TILE)


def _pad_off(is_pad):
    return jnp.where(is_pad, jnp.float32(NEG_INF), jnp.float32(0.0))


def _sweep_unroll(n_tiles):
    n_pairs = _pair_table(n_tiles).shape[1] - 2
    return next(u for u in SWEEP_UNROLLS if n_pairs % u == 0)


def _ring_slots(unroll):
    return max(r for r in range(1, MAX_RING_SLOTS + 1) if unroll % r == 0)


def _sweep(tab_ref, n_maps, unroll, score_map, soft, value_map, finish):
    n_pairs = tab_ref.shape[1] - 2
    ring = _ring_slots(unroll)
    assert n_pairs % unroll == 0 and unroll % ring == 0
    pair = lambda g: (tab_ref[0, g], tab_ref[1, g])

    def score_all(g, slot):
        return tuple(x for c in range(n_maps) for x in score_map(*pair(g), slot, c))

    def half_step(g, k, carry):
        colmax, m, alpha = carry
        colmax1 = []
        for c in range(n_maps):
            colmax1 += score_map(*pair(g), k, c)
            value_map(*pair(g - 2), (k - 2) % ring, c, alpha[c])
        m1, alpha1 = soft(*pair(g - 1), (k - 1) % ring, colmax, m)
        return tuple(colmax1), m1, alpha1

    colmax = score_all(0, 0)
    m_init = tuple(jnp.full((1, ATT_TILE), NEG_INF, F32) for _ in range(n_maps))
    m, alpha = soft(*pair(0), 0, colmax, m_init)
    carry = (score_all(1, 1), m, alpha)

    def step(h, carry):
        g = 2 + unroll * h
        for k in range(unroll):
            carry = half_step(g + k, (2 + k) % ring, carry)
        for k in range(unroll):
            @pl.when(tab_ref[2, g + k - 2] == 1)
            def _(k=k):
                finish(tab_ref[0, g + k - 2])
        return carry

    lax.fori_loop(0, n_pairs // unroll, step, carry)


def _score_map(i, p, slot, c, q_ref, k_ref, bias_of, s_ref):
    q = q_ref[0, _rows(i), :]
    items = [_item(i, 2 * p + u) for u in range(2)]
    kt = jnp.concatenate([k_ref[0, _rows(j), :] for j, _, _ in items], axis=0)
    s2 = _dot_nt(kt, q)
    colmax = []
    for u, (_, is_diag, _) in enumerate(items):
        s = s2[u * ATT_TILE:(u + 1) * ATT_TILE] + bias_of(is_diag.astype(jnp.int32))
        s_ref[slot, u, c] = s.astype(BF16)
        colmax.append(jnp.max(s, axis=0, keepdims=True))
    return colmax


def _soft(p, slot, colmax, offs, m, s_ref, p_ref):
    m_out, alpha_out = [], []
    for c in range(len(m)):
        m_old = jnp.where(p == 0, jnp.float32(NEG_INF), m[c])
        m_new = jnp.maximum(m_old, jnp.maximum(colmax[2 * c] + offs[0][c], colmax[2 * c + 1] + offs[1][c]))
        for u in range(2):
            shift = (m_new - offs[u][c]).astype(BF16)
            p_ref[slot, c, u * ATT_TILE:(u + 1) * ATT_TILE, :] = jnp.exp2(s_ref[slot, u, c] - shift)
        m_out.append(m_new)
        alpha_out.append(jnp.exp2(m_old - m_new))
    return tuple(m_out), tuple(alpha_out)


def _value_map(i, slot, c, alpha_c, vts, p_ref, acc_ref):
    vt = jnp.concatenate(vts, axis=1)
    vt1 = jnp.concatenate([vt, jnp.ones((BF16_ROWS, vt.shape[1]), BF16)], axis=0)
    par = i % acc_ref.shape[0]
    acc_ref[par, c] = alpha_c * acc_ref[par, c] + _dot(vt1, p_ref[slot, c])


def _moba_kernel(tab_ref, qa_ref, qb_ref, k_ref, vt_ref, gt_ref, kmean_ref, bias_ref, offs_ref, o_ref,
                 sel_ref, s_ref, p_ref, acc_ref):
    nblk = kmean_ref.shape[1]
    n_tiles = k_ref.shape[1] // ATT_TILE
    q_refs = (qa_ref, qb_ref)
    acc_ref[...] = jnp.zeros(acc_ref.shape, F32)

    km = kmean_ref[0]
    km_hi = km.astype(BF16)
    km_lo = (km - km_hi.astype(F32)).astype(BF16)
    km_parts = jnp.concatenate([km_hi, km_lo], axis=0)
    blk = lax.broadcasted_iota(jnp.int32, (nblk, ATT_TILE), 0)
    for i in range(n_tiles):
        valid = blk < i
        for hh in range(2):
            q = q_refs[hh][0, i * ATT_TILE:(i + 1) * ATT_TILE, :]
            g2 = _dot_nt(km_parts, q)
            g = g2[0:nblk] + g2[nblk:2 * nblk]
            g = jnp.where(valid, g, -jnp.inf)
            sel = blk == i
            for _ in range(min(MOBA_TOPK, nblk - 1)):
                mx = jnp.max(g, axis=0, keepdims=True)
                first = jnp.min(jnp.where(g == mx, blk, nblk), axis=0, keepdims=True)
                pick = blk == first
                sel = jnp.logical_or(sel, jnp.logical_and(pick, valid))
                g = jnp.where(pick, -jnp.inf, g)
            sel_ref[i, hh] = jnp.where(sel, 0.0, NEG_INF).astype(F32)

    def score_map(i, p, slot, hh):
        return _score_map(i, p, slot, hh, q_refs[hh], k_ref, lambda d: bias_ref[hh, d], s_ref)

    def soft(i, p, slot, colmax, m):
        offs = []
        for u in range(2):
            j, _, is_pad = _item(i, 2 * p + u)
            pad = _pad_off(is_pad)
            offs.append([offs_ref[hh, pl.ds(i - j, 1), :] + sel_ref[i, hh, pl.ds(j, 1), :] + pad
                         for hh in range(2)])
        return _soft(p, slot, colmax, offs, m, s_ref, p_ref)

    def value_map(i, p, slot, hh, alpha_c):
        vts = []
        for u in range(2):
            j, _, _ = _item(i, 2 * p + u)
            vts.append(vt_ref[0, j, hh * HEAD_DIM:(hh + 1) * HEAD_DIM, :])
        _value_map(i, slot, hh, alpha_c, vts, p_ref, acc_ref)

    def finish(i):
        par = i % acc_ref.shape[0]
        o_t = jnp.concatenate([acc_ref[par, hh, 0:HEAD_DIM] * (1.0 / acc_ref[par, hh, HEAD_DIM:HEAD_DIM + 1])
                               for hh in range(2)], axis=0)
        o_ref[0, i] = (o_t * gt_ref[0, i].astype(F32)).astype(BF16)

    _sweep(tab_ref, 2, acc_ref.shape[0], score_map, soft, value_map, finish)


def _moba(q_split, km, vt, gt, kmean, bias, offs, tab):
    B, S, width = km.shape
    pairs = width // LANES
    nt = S // ATT_TILE
    nblk = kmean.shape[1]
    unroll = _sweep_unroll(nt)
    ring = _ring_slots(unroll)
    seq_spec = lambda lane_block0: pl.BlockSpec((1, S, LANES), lambda b, p: (b, 0, lane_block0 + p))
    t_spec = pl.BlockSpec((1, nblk, LANES, ATT_TILE), lambda b, p: (b, 0, p, 0))
    return pl.pallas_call(
        _moba_kernel,
        grid=(B, pairs),
        in_specs=[pl.BlockSpec(memory_space=pltpu.SMEM),
                  seq_spec(0), seq_spec(pairs), seq_spec(0), t_spec, t_spec,
                  pl.BlockSpec((1, nblk, LANES), lambda b, p: (b, 0, p)),
                  pl.BlockSpec((2, 2, ATT_TILE, ATT_TILE), lambda b, p: (p, 0, 0, 0)),
                  pl.BlockSpec((2, nt, ATT_TILE), lambda b, p: (p, 0, 0))],
        out_specs=t_spec,
        out_shape=jax.ShapeDtypeStruct((B, nt, width, ATT_TILE), BF16),
        scratch_shapes=[pltpu.VMEM((nt, 2, nblk, ATT_TILE), F32),
                        pltpu.VMEM((ring, 2, 2, ATT_TILE, ATT_TILE), BF16),
                        pltpu.VMEM((ring, 2, 2 * ATT_TILE, ATT_TILE), BF16),
                        pltpu.VMEM((unroll, 2, HEAD_DIM + BF16_ROWS, ATT_TILE), F32)],
        compiler_params=pltpu.CompilerParams(dimension_semantics=("arbitrary", "arbitrary")),
        name="moba_attn",
    )(tab, q_split, q_split, km, vt, gt, kmean, bias, offs)


def _diff_kernel(tab_ref, qa_ref, qb_ref, k_ref, vt_ref, gt_ref, bias_ref, offs_ref, lamv_ref, subw_ref,
                 o_ref, s_ref, p_ref, acc_ref, *, lambda_init):
    q_refs = (qa_ref, qb_ref)
    acc_ref[...] = jnp.zeros(acc_ref.shape, F32)
    lv = lamv_ref[...]
    lam = (jnp.exp(jnp.sum(lv[0:1] * lv[1:2], axis=-1, keepdims=True))
           - jnp.exp(jnp.sum(lv[2:3] * lv[3:4], axis=-1, keepdims=True)) + lambda_init)

    def score_map(i, p, slot, c):
        return _score_map(i, p, slot, c, q_refs[c], k_ref, lambda d: bias_ref[0, d], s_ref)

    def soft(i, p, slot, colmax, m):
        offs = []
        for u in range(2):
            j, _, is_pad = _item(i, 2 * p + u)
            off = offs_ref[0, pl.ds(i - j, 1), :] + _pad_off(is_pad)
            offs.append([off, off])
        return _soft(p, slot, colmax, offs, m, s_ref, p_ref)

    def value_map(i, p, slot, c, alpha_c):
        vts = [vt_ref[0, _item(i, 2 * p + u)[0]] for u in range(2)]
        _value_map(i, slot, c, alpha_c, vts, p_ref, acc_ref)

    def finish(i):
        par = i % acc_ref.shape[0]
        o = [acc_ref[par, c, 0:LANES] * (1.0 / acc_ref[par, c, LANES:LANES + 1]) for c in range(2)]
        o_t = o[0] - lam * o[1]
        ms = jnp.mean(o_t * o_t, axis=0, keepdims=True)
        o_n = o_t * lax.rsqrt(ms + NORM_EPS) * subw_ref[...]
        o_ref[0, i] = (o_n * gt_ref[0, i].astype(F32)).astype(BF16)

    _sweep(tab_ref, 2, acc_ref.shape[0], score_map, soft, value_map, finish)


def _diff(q_split, dk, vt, gt, bias, offs, lamv, subw, lambda_init, v_row_block0, tab):
    B, S, width = dk.shape
    heads = width // LANES
    nt = S // ATT_TILE
    nblk = vt.shape[1]
    unroll = _sweep_unroll(nt)
    ring = _ring_slots(unroll)
    seq_spec = lambda lane_block0: pl.BlockSpec((1, S, LANES), lambda b, h: (b, 0, lane_block0 + h))
    t_spec = lambda row_block0: pl.BlockSpec((1, nblk, LANES, ATT_TILE), lambda b, h: (b, 0, row_block0 + h, 0))
    return pl.pallas_call(
        functools.partial(_diff_kernel, lambda_init=lambda_init),
        grid=(B, heads),
        in_specs=[pl.BlockSpec(memory_space=pltpu.SMEM),
                  seq_spec(0), seq_spec(heads), seq_spec(0), t_spec(v_row_block0), t_spec(v_row_block0),
                  pl.BlockSpec((1, 2, ATT_TILE, ATT_TILE), lambda b, h: (h, 0, 0, 0)),
                  pl.BlockSpec((1, nt, ATT_TILE), lambda b, h: (h, 0, 0)),
                  pl.BlockSpec(lamv.shape, lambda b, h: (0, 0)),
                  pl.BlockSpec((LANES, ATT_TILE), lambda b, h: (0, 0))],
        out_specs=t_spec(0),
        out_shape=jax.ShapeDtypeStruct((B, nt, width, ATT_TILE), BF16),
        scratch_shapes=[pltpu.VMEM((ring, 2, 2, ATT_TILE, ATT_TILE), BF16),
                        pltpu.VMEM((ring, 2, 2 * ATT_TILE, ATT_TILE), BF16),
                        pltpu.VMEM((unroll, 2, LANES + BF16_ROWS, ATT_TILE), F32)],
        compiler_params=pltpu.CompilerParams(dimension_semantics=("arbitrary", "arbitrary")),
        name="diff_attn",
    )(tab, q_split, q_split, dk, vt, gt, bias, offs, lamv, subw)


def _outproj_kernel(x_ref, m_ref, d_ref, w_ref, o_ref):
    o_ref[0] = _merge_rows(x_ref, m_ref, d_ref, w_ref)


def _outproj(x, m_att, d_att, w_out):
    B, S, D = x.shape
    rows = _proj_rows(S)
    tiles = rows // ATT_TILE
    row_spec = pl.BlockSpec((1, rows, D), lambda b, t: (b, t, 0))
    return pl.pallas_call(
        _outproj_kernel,
        grid=(B, S // rows),
        in_specs=[row_spec, t_spec_of(m_att, tiles), t_spec_of(d_att, tiles),
                  pl.BlockSpec(w_out.shape, lambda b, t: (0, 0))],
        out_specs=row_spec,
        out_shape=jax.ShapeDtypeStruct((B, S, D), F32),
        compiler_params=pltpu.CompilerParams(dimension_semantics=("arbitrary", "arbitrary")),
        name="outproj",
    )(x, m_att, d_att, w_out)


def kernel(x, norm_g, w_in, moba_q_norm, moba_k_norm, diff_q_norm, diff_k_norm,
           lambda_q1, lambda_k1, lambda_q2, lambda_k2, diff_subln, w_out):
    B, S, D = x.shape
    depth = w_in.shape[0]
    width = D // 2
    moba_heads = width // HEAD_DIM
    diff_heads = width // (2 * HEAD_DIM)
    assert S % MOBA_BLOCK == 0 and width % MXU_TILE == 0 and S // MOBA_BLOCK > 1
    nt = S // ATT_TILE
    scale = HEAD_DIM ** -0.5 * LOG2E

    m_bias, m_offs = _bias_tables(moba_heads, nt)
    d_bias, d_offs = _bias_tables(diff_heads, nt)
    r = lax.broadcasted_iota(jnp.int32, (MXU_TILE, MXU_TILE), 0) // HEAD_DIM
    c = lax.broadcasted_iota(jnp.int32, (MXU_TILE, MXU_TILE), 1) // HEAD_DIM
    gsum = (r == c).astype(BF16)
    reps = width // HEAD_DIM
    tab = jnp.asarray(_pair_table(nt))

    def layer_weights(layer):
        w = w_in[layer]
        cols = [w[:, k * width:(k + 1) * width] for k in range(8)]
        wa = jnp.concatenate([cols[0], cols[1], cols[4], cols[5]], axis=1).astype(BF16)
        wvt = jnp.concatenate([cols[2], cols[6]], axis=1).astype(BF16).T
        wgt = jnp.concatenate([cols[3], cols[7]], axis=1).astype(BF16).T
        tile_w = lambda v, s: (jnp.tile(v, reps) * s)[None, :].astype(F32)
        return (norm_g[layer][None, :], wa, wvt, wgt, gsum,
                tile_w(moba_q_norm[layer], scale), tile_w(moba_k_norm[layer], 1.0),
                tile_w(diff_q_norm[layer], scale), tile_w(diff_k_norm[layer], 1.0))

    proj = _inproj(x, *layer_weights(0))
    for layer in range(depth):
        qm, km, dq, dk, gt, vt, kmean = proj
        m_att = _moba(qm, km, vt, gt, kmean, m_bias, m_offs, tab)
        lambda_init = 0.8 - 0.6 * math.exp(-0.3 * layer)
        lamv = jnp.stack([lambda_q1[layer], lambda_k1[layer], lambda_q2[layer], lambda_k2[layer]])
        subw = jnp.broadcast_to((diff_subln[layer] * (1.0 - lambda_init))[:, None],
                                (2 * HEAD_DIM, ATT_TILE)).astype(F32)
        d_att = _diff(dq, dk, vt, gt, d_bias, d_offs, lamv, subw, lambda_init, width // LANES, tab)
        wo = w_out[layer].astype(BF16)
        if layer + 1 < depth:
            x, proj = _inproj(x, *layer_weights(layer + 1), merge=(m_att, d_att, wo))
        else:
            x = _outproj(x, m_att, d_att, wo)
    return x
```
